```python
import jax, jax.numpy as jnp
from jax import lax
import numpy as np

D_MODEL = 2048
BATCH = 2
SEQ = 4096
DEPTH = 2
DEC_BATCH = 32
DEC_SEQ = 4
PAST_LEN = 16384
PAGE_SIZE = 128

CONV_DIM = D_MODEL // 2
ATT_DIM = D_MODEL - CONV_DIM
HEAD_DIM = 64
N_HEADS = ATT_DIM // HEAD_DIM
N_KV_HEADS = 4
GROUP = N_HEADS // N_KV_HEADS
ROT_DIM = HEAD_DIM // 4
ROPE_THETA = 500000.0
WINDOW = 128
CONV_W = 3
KV_DIM = N_KV_HEADS * HEAD_DIM
PROJ_DIM = 3 * CONV_DIM + ATT_DIM + 2 * KV_DIM
N_EXPERTS = 32
N_EXPERT_GROUPS = 4
EXPERTS_PER_GROUP = N_EXPERTS // N_EXPERT_GROUPS
TOP_K = 2
D_FF = D_MODEL // 2
MOE_BLOCK = 128
MIN_MOE_BLOCK = 8
LN_EPS = 1e-5

kernel_name = 'hybrid_conv_swa_sharedrouter_moe_step'

F32 = jnp.float32


def layer_norm(x, g, b):
    xf = x.astype(F32)
    mu = xf.mean(-1, keepdims=True)
    var = jnp.square(xf - mu).mean(-1, keepdims=True)
    return ((xf - mu) * lax.rsqrt(var + LN_EPS) * g.astype(F32) + b.astype(F32)).astype(x.dtype)


def split_proj(h):
    o = [0, CONV_DIM, 2 * CONV_DIM, 3 * CONV_DIM, 3 * CONV_DIM + ATT_DIM,
         3 * CONV_DIM + ATT_DIM + KV_DIM, PROJ_DIM]
    return tuple(h[..., o[i]:o[i + 1]] for i in range(6))


def causal_depthwise_conv(full, w):
    n = full.shape[1] - (CONV_W - 1)
    out = w[0] * full[:, 0:n]
    for j in range(1, CONV_W):
        out = out + w[j] * full[:, j:j + n]
    return out


def rope(x, pos):
    half = ROT_DIM // 2
    inv = ROPE_THETA ** (-jnp.arange(half, dtype=F32) * 2.0 / ROT_DIM)
    ang = pos.astype(F32)[:, None] * inv[None, :]
    cos = jnp.cos(ang)[:, None, :]
    sin = jnp.sin(ang)[:, None, :]
    x1 = x[..., :half].astype(F32)
    x2 = x[..., half:ROT_DIM].astype(F32)
    r1 = (x1 * cos - x2 * sin).astype(x.dtype)
    r2 = (x2 * cos + x1 * sin).astype(x.dtype)
    return jnp.concatenate([r1, r2, x[..., ROT_DIM:]], axis=-1)


def sink_probs(scores, mask, sinks):
    s = jnp.where(mask, scores, -jnp.inf)
    sk = sinks.astype(F32).reshape(N_KV_HEADS, GROUP, 1, 1)
    m = jnp.maximum(s.max(-1, keepdims=True), sk)
    p = jnp.exp(s - m)
    return p / (p.sum(-1, keepdims=True) + jnp.exp(sk - m))


def banded_window_attention(q, k, v, sinks):
    B, S = q.shape[0], q.shape[1]
    nb = S // WINDOW
    qb = q.reshape(B, nb, WINDOW, N_KV_HEADS, GROUP, HEAD_DIM)
    kb = k.reshape(B, nb, WINDOW, N_KV_HEADS, HEAD_DIM)
    vb = v.reshape(B, nb, WINDOW, N_KV_HEADS, HEAD_DIM)
    prev = lambda t: jnp.concatenate([jnp.zeros_like(t[:, :1]), t[:, :-1]], axis=1)
    kk = jnp.concatenate([prev(kb), kb], axis=2)
    vv = jnp.concatenate([prev(vb), vb], axis=2)
    s = jnp.einsum('bnqkgd,bnskd->bnkgqs', qb, kk, preferred_element_type=F32) * (HEAD_DIM ** -0.5)
    qi = jnp.arange(WINDOW)[:, None] + WINDOW
    si = jnp.arange(2 * WINDOW)[None, :]
    diff = qi - si
    band = (diff >= 0) & (diff < WINDOW)
    real = (jnp.arange(nb)[:, None, None] > 0) | (si[None] >= WINDOW)
    mask = (band[None] & real)[None, :, None, None]
    p = sink_probs(s, mask, sinks)
    o = jnp.einsum('bnkgqs,bnskd->bnqkgd', p.astype(v.dtype), vv)
    return o.reshape(B, S, ATT_DIM)


def window_decode_attention(q, k_new, v_new, k_buf, v_buf, sinks):
    Bd, T = q.shape[0], q.shape[1]
    kk = jnp.concatenate([k_buf.astype(k_new.dtype), k_new], axis=1)
    vv = jnp.concatenate([v_buf.astype(v_new.dtype), v_new], axis=1)
    qg = q.reshape(Bd, T, N_KV_HEADS, GROUP, HEAD_DIM)
    s = jnp.einsum('btkgd,bskd->bkgts', qg, kk, preferred_element_type=F32) * (HEAD_DIM ** -0.5)
    diff = (WINDOW + jnp.arange(T))[:, None] - jnp.arange(WINDOW + T)[None, :]
    mask = (diff >= 0) & (diff < WINDOW)
    p = sink_probs(s, mask, sinks)
    o = jnp.einsum('bkgts,bskd->btkgd', p.astype(vv.dtype), vv).reshape(Bd, T, ATT_DIM)
    return o, kk[:, -WINDOW:], vv[:, -WINDOW:]


def route(h, w_router, b_router):
    T = h.shape[0]
    scores = jax.nn.sigmoid((h @ w_router).astype(F32))
    sel = (scores + b_router.astype(F32)).reshape(T, N_EXPERT_GROUPS, EXPERTS_PER_GROUP)
    grp_score = lax.top_k(sel, 2)[0].sum(-1)
    g_idx = jnp.argmax(grp_score, axis=-1)
    in_grp = jnp.take_along_axis(sel, g_idx[:, None, None], axis=1)[:, 0]
    local = lax.top_k(in_grp, TOP_K)[1]
    idx = (g_idx[:, None] * EXPERTS_PER_GROUP + local).astype(jnp.int32)
    w = jnp.take_along_axis(scores, idx, axis=1)
    return idx, w / w.sum(-1, keepdims=True)


def moe_ffn(x, w_router, b_router, w_gate, w_up, w_down):
    shp = x.shape
    h = x.reshape(-1, D_MODEL)
    T = h.shape[0]
    idx, gate = route(h, w_router, b_router)
    A = T * TOP_K
    blk = max(MIN_MOE_BLOCK, min(MOE_BLOCK, A // N_EXPERTS))
    n_blocks = -(-A // blk) + N_EXPERTS
    L = n_blocks * blk
    flat_e = idx.reshape(-1)
    order = jnp.argsort(flat_e)
    e_sorted = flat_e[order]
    tok_sorted = (order // TOP_K).astype(jnp.int32)
    g_sorted = gate.reshape(-1)[order]
    counts = jnp.bincount(flat_e, length=N_EXPERTS)
    padded = (counts + blk - 1) // blk * blk
    pad_end = jnp.cumsum(padded)
    pad_start = pad_end - padded
    start = jnp.cumsum(counts) - counts
    dest = pad_start[e_sorted] + jnp.arange(A) - start[e_sorted]
    row_tok = jnp.zeros((L,), jnp.int32).at[dest].set(tok_sorted)
    row_w = jnp.zeros((L,), F32).at[dest].set(g_sorted)
    block_e = jnp.minimum(jnp.searchsorted(pad_end, jnp.arange(n_blocks) * blk, side='right'),
                          N_EXPERTS - 1)
    xb = h[row_tok].reshape(n_blocks, blk, D_MODEL)

    def expert_block(args):
        xe, e = args
        return (jax.nn.silu(xe @ w_gate[e]) * (xe @ w_up[e])) @ w_down[e]

    yb = lax.map(expert_block, (xb, block_e)).reshape(L, D_MODEL)
    out = jnp.zeros_like(h).at[row_tok].add(yb * row_w[:, None].astype(h.dtype))
    return out.reshape(shp)


def mix_prompt(x, w_in, conv_w, sinks, w_o):
    B, S, _ = x.shape
    c, bg, u, q, k, v = split_proj(x @ w_in)
    inner = c * u
    full = jnp.pad(inner, ((0, 0), (CONV_W - 1, 0), (0, 0)))
    conv_out = bg * causal_depthwise_conv(full, conv_w)
    pos = jnp.arange(S)
    q = rope(q.reshape(B, S, N_HEADS, HEAD_DIM), pos)
    k = rope(k.reshape(B, S, N_KV_HEADS, HEAD_DIM), pos)
    v = v.reshape(B, S, N_KV_HEADS, HEAD_DIM)
    attn = banded_window_attention(q, k, v, sinks)
    y = jnp.concatenate([conv_out, attn], axis=-1) @ w_o
    return y, inner[:, -(CONV_W - 1):], k[:, -WINDOW:], v[:, -WINDOW:]


def mix_sample(x, conv_buf, k_buf, v_buf, w_in, conv_w, sinks, w_o):
    Bd, T, _ = x.shape
    c, bg, u, q, k, v = split_proj(x @ w_in)
    inner = c * u
    full = jnp.concatenate([conv_buf.astype(inner.dtype), inner], axis=1)
    conv_out = bg * causal_depthwise_conv(full, conv_w)
    pos = PAST_LEN + jnp.arange(T)
    q = rope(q.reshape(Bd, T, N_HEADS, HEAD_DIM), pos)
    k = rope(k.reshape(Bd, T, N_KV_HEADS, HEAD_DIM), pos)
    v = v.reshape(Bd, T, N_KV_HEADS, HEAD_DIM)
    attn, k_win, v_win = window_decode_attention(q, k, v, k_buf, v_buf, sinks)
    y = jnp.concatenate([conv_out, attn], axis=-1) @ w_o
    return y, full[:, -(CONV_W - 1):], k_win, v_win


def setup_inputs(seed: int = 0) -> dict:
    key = jax.random.key(seed)
    ks = jax.random.split(key, 24)
    beta = (8 * DEPTH) ** -0.25
    nrm = lambda k, shape, scale: jax.random.normal(k, shape, F32) * scale
    return {
        'x_prompt': nrm(ks[0], (BATCH, SEQ, D_MODEL), 1.0),
        'x_sample': nrm(ks[1], (DEC_BATCH, DEC_SEQ, D_MODEL), 1.0),
        'state_conv': nrm(ks[2], (DEPTH, DEC_BATCH, CONV_W - 1, CONV_DIM), 1.0),
        'cache_k': nrm(ks[3], (DEPTH, DEC_BATCH, WINDOW, N_KV_HEADS, HEAD_DIM), 1.0),
        'cache_v': nrm(ks[4], (DEPTH, DEC_BATCH, WINDOW, N_KV_HEADS, HEAD_DIM), 1.0),
        'w_in': nrm(ks[5], (DEPTH, D_MODEL, PROJ_DIM), D_MODEL ** -0.5),
        'conv_w': nrm(ks[6], (DEPTH, CONV_W, CONV_DIM), CONV_W ** -0.5),
        'attn_sinks': nrm(ks[7], (DEPTH, N_HEADS), 0.5),
        'w_o': nrm(ks[8], (DEPTH, D_MODEL, D_MODEL), beta * D_MODEL ** -0.5),
        'ln1_g': 1.0 + nrm(ks[9], (DEPTH, D_MODEL), 0.02),
        'ln1_b': nrm(ks[10], (DEPTH, D_MODEL), 0.02),
        'w_router': nrm(ks[11], (D_MODEL, N_EXPERTS), D_MODEL ** -0.5),
        'b_router': nrm(ks[12], (N_EXPERTS,), 0.01),
        'w_gate': nrm(ks[13], (DEPTH, N_EXPERTS, D_MODEL, D_FF), D_MODEL ** -0.5),
        'w_up': nrm(ks[14], (DEPTH, N_EXPERTS, D_MODEL, D_FF), D_MODEL ** -0.5),
        'w_down': nrm(ks[15], (DEPTH, N_EXPERTS, D_FF, D_MODEL), beta * D_FF ** -0.5),
        'ln2_g': 1.0 + nrm(ks[16], (DEPTH, D_MODEL), 0.02),
        'ln2_b': nrm(ks[17], (DEPTH, D_MODEL), 0.02),
    }


def reference(x_prompt, x_sample, state_conv, cache_k, cache_v, w_in, conv_w, attn_sinks, w_o,
              ln1_g, ln1_b, w_router, b_router, w_gate, w_up, w_down, ln2_g, ln2_b):
    alpha = (2 * DEPTH) ** 0.25
    yp, ys = x_prompt, x_sample
    conv_p, k_p, v_p, conv_s, k_s, v_s = [], [], [], [], [], []
    for l in range(DEPTH):
        mp, cp, kp, vp = mix_prompt(yp, w_in[l], conv_w[l], attn_sinks[l], w_o[l])
        ms, cs, kss, vs = mix_sample(ys, state_conv[l], cache_k[l], cache_v[l],
                                     w_in[l], conv_w[l], attn_sinks[l], w_o[l])
        yp = layer_norm(alpha * yp + mp, ln1_g[l], ln1_b[l])
        ys = layer_norm(alpha * ys + ms, ln1_g[l], ln1_b[l])
        yp = layer_norm(alpha * yp + moe_ffn(yp, w_router, b_router, w_gate[l], w_up[l], w_down[l]),
                        ln2_g[l], ln2_b[l])
        ys = layer_norm(alpha * ys + moe_ffn(ys, w_router, b_router, w_gate[l], w_up[l], w_down[l]),
                        ln2_g[l], ln2_b[l])
        conv_p.append(cp); k_p.append(kp); v_p.append(vp)
        conv_s.append(cs); k_s.append(kss); v_s.append(vs)
    return (yp, ys, jnp.stack(conv_p), jnp.stack(k_p), jnp.stack(v_p),
            jnp.stack(conv_s), jnp.stack(k_s), jnp.stack(v_s))
```

```python
import functools
import math

import jax
import jax.numpy as jnp
from jax import lax
from jax.experimental import pallas as pl
from jax.experimental.pallas import tpu as pltpu

F32 = jnp.float32
BF16 = jnp.bfloat16

HEAD_DIM = 64
N_KV_HEADS = 4
ROT_DIM = HEAD_DIM // 4
ROPE_THETA = 500000.0
WINDOW = 128
PAST_LEN = 16384
N_EXPERT_GROUPS = 4
TOP_K = 2
LN_EPS = 1e-5

LANES = 128
VMEM_LIMIT = 56 * 1024 * 1024

EXPERT_TILE = 640
FF_CHUNK = 256
ROW_CHUNK = 1024


def _params(*sem):
    return pltpu.CompilerParams(dimension_semantics=sem, vmem_limit_bytes=VMEM_LIMIT)


def _matmul_kernel(x_ref, w_ref, o_ref):
    o_ref[...] = jnp.dot(x_ref[...].astype(BF16), w_ref[...], preferred_element_type=F32)


def _matmul(x, w, tm, tn, name):
    m, k = x.shape
    n = w.shape[1]
    return pl.pallas_call(
        _matmul_kernel,
        grid=(n // tn, m // tm),
        in_specs=[pl.BlockSpec((tm, k), lambda j, i: (i, 0)),
                  pl.BlockSpec((k, tn), lambda j, i: (0, j))],
        out_specs=pl.BlockSpec((tm, tn), lambda j, i: (i, j)),
        out_shape=jax.ShapeDtypeStruct((m, n), F32),
        compiler_params=_params("arbitrary", "arbitrary"),
        name=name,
    )(x, w)


def _rope_chunk(xc, cs, sa, sb):
    return xc * cs + pltpu.roll(xc, LANES - ROT_DIM // 2, 1) * sa + pltpu.roll(xc, ROT_DIM // 2, 1) * sb


def _conv(inner, gate, w_ref, prev1, prev2):
    return gate * (w_ref[0:1, :] * prev2 + w_ref[1:2, :] * prev1 + w_ref[2:3, :] * inner)


def _prep_prompt_kernel(hc, hb, hu, hq, hkv, cw, cs_ref, sa_ref, sb_ref,
                        conv_o, q_o, k_o, v_o, cst_o, kw_o, vw_o, carry, *, tiles_per_seq):
    tm, cdim = hc.shape

    @pl.when(pl.program_id(0) % tiles_per_seq == 0)
    def _():
        carry[...] = jnp.zeros_like(carry)

    inner = hc[...] * hu[...]
    row = lax.broadcasted_iota(jnp.int32, (tm, cdim), 0)
    c1 = carry[7:8, :]
    c2 = carry[6:7, :]
    prev1 = jnp.where(row == 0, c1, pltpu.roll(inner, 1, 0))
    prev2 = jnp.where(row == 0, c2, jnp.where(row == 1, c1, pltpu.roll(inner, 2, 0)))
    conv_o[...] = _conv(inner, hb[...], cw, prev1, prev2).astype(BF16)
    carry[...] = inner[tm - 8:, :]
    cst_o[...] = inner[tm - 2:, :]

    cs, sa, sb = cs_ref[...], sa_ref[...], sb_ref[...]
    for c in range(hq.shape[1] // LANES):
        sl = slice(c * LANES, (c + 1) * LANES)
        q_o[:, sl] = _rope_chunk(hq[:, sl], cs, sa, sb).astype(BF16)
    kvd = hkv.shape[1] // 2
    for c in range(kvd // LANES):
        sl = slice(c * LANES, (c + 1) * LANES)
        kr = _rope_chunk(hkv[:, sl], cs, sa, sb)
        k_o[:, sl] = kr.astype(BF16)
        kw_o[:, sl] = kr[tm - WINDOW:, :]
    v = hkv[:, kvd:]
    v_o[...] = v.astype(BF16)
    vw_o[...] = v[tm - WINDOW:, :]


def _prep_prompt(h, conv_w, l, tabs, batch, seq, cdim, adim, kvd, tm):
    m = h.shape[0]
    tps = seq // tm
    qblk = 3 * cdim // adim
    kvblk = (3 * cdim + adim) // (2 * kvd)
    row = lambda i: (i, 0)
    return pl.pallas_call(
        functools.partial(_prep_prompt_kernel, tiles_per_seq=tps),
        grid=(m // tm,),
        in_specs=[pl.BlockSpec((tm, cdim), lambda i: (i, 0)),
                  pl.BlockSpec((tm, cdim), lambda i: (i, 1)),
                  pl.BlockSpec((tm, cdim), lambda i: (i, 2)),
                  pl.BlockSpec((tm, adim), lambda i: (i, qblk)),
                  pl.BlockSpec((tm, 2 * kvd), lambda i: (i, kvblk)),
                  pl.BlockSpec((None, 3, cdim), lambda i: (l, 0, 0)),
                  pl.BlockSpec((tm, LANES), lambda i: (i % tps, 0)),
                  pl.BlockSpec((tm, LANES), lambda i: (i % tps, 0)),
                  pl.BlockSpec((tm, LANES), lambda i: (i % tps, 0))],
        out_specs=[pl.BlockSpec((tm, cdim), row),
                   pl.BlockSpec((tm, adim), row),
                   pl.BlockSpec((tm, kvd), row),
                   pl.BlockSpec((tm, kvd), row),
                   pl.BlockSpec((None, 2, cdim), lambda i: (i // tps, 0, 0)),
                   pl.BlockSpec((None, WINDOW, kvd), lambda i: (i // tps, 0, 0)),
                   pl.BlockSpec((None, WINDOW, kvd), lambda i: (i // tps, 0, 0))],
        out_shape=[jax.ShapeDtypeStruct((m, cdim), BF16),
                   jax.ShapeDtypeStruct((m, adim), BF16),
                   jax.ShapeDtypeStruct((m, kvd), BF16),
                   jax.ShapeDtypeStruct((m, kvd), BF16),
                   jax.ShapeDtypeStruct((batch, 2, cdim), F32),
                   jax.ShapeDtypeStruct((batch, WINDOW, kvd), F32),
                   jax.ShapeDtypeStruct((batch, WINDOW, kvd), F32)],
        scratch_shapes=[pltpu.VMEM((8, cdim), F32)],
        compiler_params=_params("arbitrary"),
        name="prep_prompt",
    )(h, h, h, h, h, conv_w, *tabs)


def _prep_sample_kernel(hc, hb, hu, hq, hkv, cw, cs_ref, sa_ref, sb_ref, st1, st2,
                        conv_o, q_o, k_o, v_o, inner_o, *, dec_seq):
    tm, cdim = hc.shape
    inner = hc[...] * hu[...]
    t = lax.broadcasted_iota(jnp.int32, (tm, cdim), 0) % dec_seq
    prev1 = jnp.where(t == 0, st1[...], pltpu.roll(inner, 1, 0))
    prev2 = jnp.where(t < 2, st2[...], pltpu.roll(inner, 2, 0))
    conv_o[...] = _conv(inner, hb[...], cw, prev1, prev2).astype(BF16)
    inner_o[...] = inner
    cs, sa, sb = cs_ref[...], sa_ref[...], sb_ref[...]
    for c in range(hq.shape[1] // LANES):
        sl = slice(c * LANES, (c + 1) * LANES)
        q_o[:, sl] = _rope_chunk(hq[:, sl], cs, sa, sb)
    kvd = hkv.shape[1] // 2
    for c in range(kvd // LANES):
        sl = slice(c * LANES, (c + 1) * LANES)
        k_o[:, sl] = _rope_chunk(hkv[:, sl], cs, sa, sb)
    v_o[...] = hkv[:, kvd:]


def _prep_sample(h, conv_w, l, tabs, st1, st2, dec_seq, cdim, adim, kvd):
    m = h.shape[0]
    qblk = 3 * cdim // adim
    kvblk = (3 * cdim + adim) // (2 * kvd)
    z = lambda i: (0, 0)
    return pl.pallas_call(
        functools.partial(_prep_sample_kernel, dec_seq=dec_seq),
        grid=(1,),
        in_specs=[pl.BlockSpec((m, cdim), lambda i: (0, 0)),
                  pl.BlockSpec((m, cdim), lambda i: (0, 1)),
                  pl.BlockSpec((m, cdim), lambda i: (0, 2)),
                  pl.BlockSpec((m, adim), lambda i: (0, qblk)),
                  pl.BlockSpec((m, 2 * kvd), lambda i: (0, kvblk)),
                  pl.BlockSpec((None, 3, cdim), lambda i: (l, 0, 0)),
                  pl.BlockSpec((m, LANES), z), pl.BlockSpec((m, LANES), z), pl.BlockSpec((m, LANES), z),
                  pl.BlockSpec((m, cdim), z), pl.BlockSpec((m, cdim), z)],
        out_specs=[pl.BlockSpec((m, cdim), z), pl.BlockSpec((m, adim), z),
                   pl.BlockSpec((m, kvd), z), pl.BlockSpec((m, kvd), z), pl.BlockSpec((m, cdim), z)],
        out_shape=[jax.ShapeDtypeStruct((m, cdim), BF16),
                   jax.ShapeDtypeStruct((m, adim), F32),
                   jax.ShapeDtypeStruct((m, kvd), F32),
                   jax.ShapeDtypeStruct((m, kvd), F32),
                   jax.ShapeDtypeStruct((m, cdim), F32)],
        compiler_params=_params("arbitrary"),
        name="prep_sample",
    )(h, h, h, h, h, conv_w, *tabs, st1, st2)


def _sink_softmax(s, mask, sink_col):
    s = jnp.where(mask, s, -jnp.inf)
    m = jnp.maximum(jnp.max(s, axis=-1, keepdims=True), sink_col)
    p = jnp.exp(s - m)
    den = jnp.sum(p, axis=-1, keepdims=True) + jnp.exp(sink_col - m)
    return p * (1.0 / den)


def _attn_prompt_kernel(sink_ref, q_ref, kp_ref, kc_ref, vp_ref, vc_ref, o_ref, *, l, scale):
    n = pl.program_id(1)
    w = q_ref.shape[0]
    chunks = q_ref.shape[1] // LANES
    pairs = kc_ref.shape[1] // LANES
    cpp = chunks // pairs
    rows = 2 * cpp * w
    lo = lax.broadcasted_iota(jnp.int32, (w, LANES), 1) < HEAD_DIM
    qi = lax.broadcasted_iota(jnp.int32, (rows, 2 * w), 0) % w
    sj = lax.broadcasted_iota(jnp.int32, (rows, 2 * w), 1)
    mask = (sj <= qi + w) & (sj > qi) & ((n > 0) | (sj >= w))
    zero = jnp.zeros((w, LANES), BF16)
    for p in range(pairs):
        ksl = slice(p * LANES, (p + 1) * LANES)
        kk = jnp.concatenate([kp_ref[:, ksl], kc_ref[:, ksl]], axis=0)
        vv = jnp.concatenate([vp_ref[:, ksl], vc_ref[:, ksl]], axis=0)
        pieces, sinks = [], []
        for j in range(cpp):
            c = p * cpp + j
            qc = q_ref[:, c * LANES:(c + 1) * LANES]
            pieces += [jnp.where(lo, qc, zero), jnp.where(lo, zero, qc)]
            sinks += [jnp.full((w, 1), sink_ref[l, 2 * c], F32), jnp.full((w, 1), sink_ref[l, 2 * c + 1], F32)]
        qs = jnp.concatenate(pieces, axis=0)
        s = lax.dot_general(qs, kk, (((1,), (1,)), ((), ())), preferred_element_type=F32) * scale
        pr = _sink_softmax(s, mask, jnp.concatenate(sinks, axis=0))
        o = jnp.dot(pr.astype(BF16), vv, preferred_element_type=F32)
        for j in range(cpp):
            c = p * cpp + j
            o_ref[:, c * LANES:(c + 1) * LANES] = jnp.where(
                lo, o[2 * j * w:(2 * j + 1) * w], o[(2 * j + 1) * w:(2 * j + 2) * w]).astype(BF16)


def _attn_prompt(sinks, q, k, v, l, batch, seq):
    m, adim = q.shape
    kvd = k.shape[1]
    nb = seq // WINDOW
    cur = lambda b, n: (b * nb + n, 0)
    prev = lambda b, n: (b * nb + jnp.maximum(n - 1, 0), 0)
    return pl.pallas_call(
        functools.partial(_attn_prompt_kernel, l=l, scale=HEAD_DIM ** -0.5),
        grid=(batch, nb),
        in_specs=[pl.BlockSpec(memory_space=pltpu.SMEM),
                  pl.BlockSpec((WINDOW, adim), cur),
                  pl.BlockSpec((WINDOW, kvd), prev), pl.BlockSpec((WINDOW, kvd), cur),
                  pl.BlockSpec((WINDOW, kvd), prev), pl.BlockSpec((WINDOW, kvd), cur)],
        out_specs=pl.BlockSpec((WINDOW, adim), cur),
        out_shape=jax.ShapeDtypeStruct((m, adim), BF16),
        compiler_params=_params("arbitrary", "arbitrary"),
        name="attn_prompt",
    )(sinks, q, k, k, v, v)


def _attn_sample_kernel(sink_ref, q_ref, kn_ref, vn_ref, ck_ref, cv_ref,
                        o_ref, kw_ref, vw_ref, kk_ref, vv_ref, qs_ref, *, l, scale):
    t = q_ref.shape[0]
    w = ck_ref.shape[0]
    tot = kk_ref.shape[0]
    chunks = q_ref.shape[1] // LANES
    pairs = ck_ref.shape[1] // LANES
    cpp = chunks // pairs
    rows = 2 * cpp * t
    kk_ref[0:w, :] = ck_ref[...]
    vv_ref[0:w, :] = cv_ref[...]
    pad = jnp.zeros((tot - w - t, kk_ref.shape[1]), F32)
    kk_ref[w:w + t, :] = kn_ref[...]
    vv_ref[w:w + t, :] = vn_ref[...]
    kk_ref[w + t:, :] = pad
    vv_ref[w + t:, :] = pad
    kw_ref[...] = pltpu.roll(kk_ref[...], tot - t, 0)[0:w, :]
    vw_ref[...] = pltpu.roll(vv_ref[...], tot - t, 0)[0:w, :]

    lo = lax.broadcasted_iota(jnp.int32, (t, LANES), 1) < HEAD_DIM
    qi = lax.broadcasted_iota(jnp.int32, (rows, tot), 0) % t
    sj = lax.broadcasted_iota(jnp.int32, (rows, tot), 1)
    mask = (sj <= qi + w) & (sj > qi)
    rowid = lax.broadcasted_iota(jnp.int32, (rows, 1), 0) // t
    for p in range(pairs):
        ksl = slice(p * LANES, (p + 1) * LANES)
        sink_col = jnp.zeros((rows, 1), F32)
        for j in range(cpp):
            c = p * cpp + j
            qc = q_ref[:, c * LANES:(c + 1) * LANES]
            qs_ref[(2 * j) * t:(2 * j + 1) * t, :] = jnp.where(lo, qc, 0.0)
            qs_ref[(2 * j + 1) * t:(2 * j + 2) * t, :] = jnp.where(lo, 0.0, qc)
            sink_col = jnp.where(rowid == 2 * j, sink_ref[l, 2 * c], sink_col)
            sink_col = jnp.where(rowid == 2 * j + 1, sink_ref[l, 2 * c + 1], sink_col)
        s = lax.dot_general(qs_ref[...].astype(BF16), kk_ref[:, ksl].astype(BF16),
                            (((1,), (1,)), ((), ())), preferred_element_type=F32) * scale
        pr = _sink_softmax(s, mask, sink_col)
        qs_ref[...] = jnp.dot(pr.astype(BF16), vv_ref[:, ksl].astype(BF16), preferred_element_type=F32)
        for j in range(cpp):
            c = p * cpp + j
            o_ref[:, c * LANES:(c + 1) * LANES] = jnp.where(
                lo, qs_ref[(2 * j) * t:(2 * j + 1) * t, :], qs_ref[(2 * j + 1) * t:(2 * j + 2) * t, :]).astype(BF16)


def _attn_sample(sinks, q, kn, vn, cache_k, cache_v, l):
    nseq, t, adim = q.shape
    w, kvd = cache_k.shape[2], cache_k.shape[3]
    tot = -(-(w + t) // 8) * 8
    rows = 2 * (adim // kvd) * t
    seq3 = lambda b: (b, 0, 0)
    cache = lambda b: (l, b, 0, 0)
    return pl.pallas_call(
        functools.partial(_attn_sample_kernel, l=l, scale=HEAD_DIM ** -0.5),
        grid=(nseq,),
        in_specs=[pl.BlockSpec(memory_space=pltpu.SMEM),
                  pl.BlockSpec((None, t, adim), seq3),
                  pl.BlockSpec((None, t, kvd), seq3), pl.BlockSpec((None, t, kvd), seq3),
                  pl.BlockSpec((None, None, w, kvd), cache), pl.BlockSpec((None, None, w, kvd), cache)],
        out_specs=[pl.BlockSpec((None, t, adim), seq3),
                   pl.BlockSpec((None, w, kvd), seq3), pl.BlockSpec((None, w, kvd), seq3)],
        out_shape=[jax.ShapeDtypeStruct((nseq, t, adim), BF16),
                   jax.ShapeDtypeStruct((nseq, w, kvd), F32),
                   jax.ShapeDtypeStruct((nseq, w, kvd), F32)],
        scratch_shapes=[pltpu.VMEM((tot, kvd), F32), pltpu.VMEM((tot, kvd), F32),
                        pltpu.VMEM((rows, LANES), F32)],
        compiler_params=_params("arbitrary"),
        name="attn_sample",
    )(sinks, q, kn, vn, cache_k, cache_v)


def _layer_norm(z, g, b):
    mu = jnp.mean(z, axis=-1, keepdims=True)
    d = z - mu
    var = jnp.mean(d * d, axis=-1, keepdims=True)
    return d * lax.rsqrt(var + LN_EPS) * g + b


def _top2(sg, lane, n):
    m1 = jnp.max(sg, axis=-1, keepdims=True)
    i1 = jnp.min(jnp.where(sg == m1, lane, n), axis=-1, keepdims=True)
    rest = jnp.where(lane == i1, -jnp.inf, sg)
    m2 = jnp.max(rest, axis=-1, keepdims=True)
    i2 = jnp.min(jnp.where(rest == m2, lane, n), axis=-1, keepdims=True)
    return m1, i1, m2, i2


def _mix_out_kernel(conv_ref, attn_ref, x_ref, wo_ref, g_ref, b_ref, wr_ref, br_ref, cnt_in,
                    x1_ref, route_ref, cnt_ref, carry, *, l, alpha, has_alias):
    tm = x_ref.shape[0]
    cdim = conv_ref.shape[1]
    ne = wr_ref.shape[1]
    epg = ne // N_EXPERT_GROUPS

    @pl.when(pl.program_id(0) == 0)
    def _():
        carry[...] = cnt_in[...]

    y = jnp.dot(conv_ref[...], wo_ref[0:cdim, :], preferred_element_type=F32)
    y = y + jnp.dot(attn_ref[...], wo_ref[cdim:, :], preferred_element_type=F32)
    x1 = _layer_norm(alpha * x_ref[...] + y, g_ref[...], b_ref[...])
    x1_ref[...] = x1

    logits = jnp.dot(x1.astype(BF16), wr_ref[...], preferred_element_type=F32)
    scores = jax.nn.sigmoid(logits)
    sel = scores + br_ref[...]
    lane = lax.broadcasted_iota(jnp.int32, (tm, ne), 1).astype(F32)
    grp = jnp.floor(lane * (1.0 / epg))
    best_s = None
    best_g = None
    for g in range(N_EXPERT_GROUPS):
        m1, _, m2, _ = _top2(jnp.where(grp == g, sel, -jnp.inf), lane, float(ne))
        gs = m1 + m2
        if g == 0:
            best_s, best_g = gs, jnp.zeros_like(gs)
        else:
            upd = gs > best_s
            best_g = jnp.where(upd, float(g), best_g)
            best_s = jnp.where(upd, gs, best_s)
    _, e1, _, e2 = _top2(jnp.where(grp == best_g, sel, -jnp.inf), lane, float(ne))
    hit1 = lane == e1
    hit2 = lane == e2
    w1 = jnp.sum(jnp.where(hit1, scores, 0.0), axis=-1, keepdims=True)
    w2 = jnp.sum(jnp.where(hit2, scores, 0.0), axis=-1, keepdims=True)
    wsum = w1 + w2

    onehot = jnp.where(hit1 | hit2, 1.0, 0.0)
    tri = (lax.broadcasted_iota(jnp.int32, (tm, tm), 0) > lax.broadcasted_iota(jnp.int32, (tm, tm), 1))
    before = jnp.dot(jnp.where(tri, 1.0, 0.0).astype(BF16), onehot.astype(BF16), preferred_element_type=F32)
    before = before + carry[0:1, 0:ne]
    r1 = jnp.sum(jnp.where(hit1, before, 0.0), axis=-1, keepdims=True)
    r2 = jnp.sum(jnp.where(hit2, before, 0.0), axis=-1, keepdims=True)
    carry[0:1, 0:ne] = carry[0:1, 0:ne] + jnp.sum(onehot, axis=0, keepdims=True)
    cnt_ref[...] = carry[...]

    col = lax.broadcasted_iota(jnp.int32, (tm, LANES), 1)
    out = jnp.zeros((tm, LANES), F32)
    for c, val in enumerate((e1, e2, w1 / wsum, w2 / wsum, r1, r2)):
        out = jnp.where(col == c, val, out)
    route_ref[...] = out


def _mix_out(conv, attn, x, wo, ln_g, ln_b, wr, br, cnt_in, l, alpha, tm, total_rows, row_block0, prev=None):
    m, d = x.shape
    cdim = conv.shape[1]
    ne = wr.shape[1]
    rowi = lambda i: (i, 0)
    rowo = lambda i: (i + row_block0, 0)
    const = lambda i: (0, 0)
    lay = lambda i: (l, 0, 0)
    in_specs = [pl.BlockSpec((tm, cdim), rowi), pl.BlockSpec((tm, d - cdim), rowi), pl.BlockSpec((tm, d), rowi),
                pl.BlockSpec((d, d), const, pipeline_mode=pl.Buffered(1)),
                pl.BlockSpec((None, 1, d), lay), pl.BlockSpec((None, 1, d), lay),
                pl.BlockSpec((d, ne), const), pl.BlockSpec((1, ne), const),
                pl.BlockSpec((8, LANES), const)]
    args = [conv, attn, x, wo, ln_g, ln_b, wr, br, cnt_in]
    aliases = {}
    if prev is not None:
        in_specs += [pl.BlockSpec(memory_space=pl.ANY), pl.BlockSpec(memory_space=pl.ANY)]
        args += list(prev)
        aliases = {len(args) - 2: 0, len(args) - 1: 1}

    def body(*refs):
        ins = refs[:9]
        outs = refs[-4:]
        _mix_out_kernel(*ins, *outs, l=l, alpha=alpha, has_alias=prev is not None)

    return pl.pallas_call(
        body,
        grid=(m // tm,),
        in_specs=in_specs,
        out_specs=[pl.BlockSpec((tm, d), rowo), pl.BlockSpec((tm, LANES), rowo), pl.BlockSpec((8, LANES), const)],
        out_shape=[jax.ShapeDtypeStruct((total_rows, d), F32),
                   jax.ShapeDtypeStruct((total_rows, LANES), F32),
                   jax.ShapeDtypeStruct((8, LANES), F32)],
        scratch_shapes=[pltpu.VMEM((8, LANES), F32)],
        input_output_aliases=aliases,
        compiler_params=_params("arbitrary"),
        name="mix_out_sample" if prev is not None else "mix_out_prompt",
    )(*args)


def _row_copy_kernel(n_ref, idx_ref, src_ref, dst_ref, sem, *, chunk):
    base = pl.program_id(0) * chunk
    n = jnp.clip(n_ref[0] - base, 0, chunk)

    def row_dma(src_row, dst_row):
        return pltpu.make_async_copy(src_ref.at[pl.ds(src_row, 1)], dst_ref.at[pl.ds(dst_row, 1)], sem)

    def issue(j, carry):
        row_dma(idx_ref[j], base + j).start()
        return carry

    def drain(j, carry):
        row_dma(0, 0).wait()
        return carry

    lax.fori_loop(0, n, issue, 0)
    lax.fori_loop(0, n, drain, 0)


def _row_copy(src, idx, n_valid, name):
    n_rows = idx.shape[0]
    return pl.pallas_call(
        functools.partial(_row_copy_kernel, chunk=ROW_CHUNK),
        grid_spec=pltpu.PrefetchScalarGridSpec(
            num_scalar_prefetch=1,
            grid=(n_rows // ROW_CHUNK,),
            in_specs=[pl.BlockSpec((ROW_CHUNK,), lambda c, n: (c,), memory_space=pltpu.SMEM),
                      pl.BlockSpec(memory_space=pl.ANY)],
            out_specs=pl.BlockSpec(memory_space=pl.ANY),
            scratch_shapes=[pltpu.SemaphoreType.DMA(())]),
        out_shape=jax.ShapeDtypeStruct((n_rows, src.shape[1]), src.dtype),
        compiler_params=_params("arbitrary"),
        name=name,
    )(n_valid, idx, src)


def _expert_kernel(te_ref, nu_ref, x_ref, wg_ref, wu_ref, wd_ref, o_ref):
    i = pl.program_id(0)
    f = pl.program_id(1)

    @pl.when(i < nu_ref[0])
    def _():
        x = x_ref[...].astype(BF16)
        g = jnp.dot(x, wg_ref[...].astype(BF16), preferred_element_type=F32)
        u = jnp.dot(x, wu_ref[...].astype(BF16), preferred_element_type=F32)
        h = (jax.nn.silu(g) * u).astype(BF16)
        y = jnp.dot(h, wd_ref[...].astype(BF16), preferred_element_type=F32)

        @pl.when(f == 0)
        def _():
            o_ref[...] = y

        @pl.when(f > 0)
        def _():
            o_ref[...] += y


def _expert_ffn(xs, w_gate, w_up, w_down, tile_e, n_used, l):
    rows, d = xs.shape
    dff = w_gate.shape[3]
    nf = dff // FF_CHUNK
    nt = rows // EXPERT_TILE

    def tile(i, nu):
        return jnp.minimum(i, nu[0] - 1)

    def ffc(i, f, nu):
        return jnp.where(i < nu[0], f, nf - 1)

    xmap = lambda i, f, te, nu: (tile(i, nu), 0)
    gmap = lambda i, f, te, nu: (l, te[tile(i, nu)], 0, ffc(i, f, nu))
    dmap = lambda i, f, te, nu: (l, te[tile(i, nu)], ffc(i, f, nu), 0)
    return pl.pallas_call(
        _expert_kernel,
        grid_spec=pltpu.PrefetchScalarGridSpec(
            num_scalar_prefetch=2,
            grid=(nt, nf),
            in_specs=[pl.BlockSpec((EXPERT_TILE, d), xmap),
                      pl.BlockSpec((None, None, d, FF_CHUNK), gmap),
                      pl.BlockSpec((None, None, d, FF_CHUNK), gmap),
                      pl.BlockSpec((None, None, FF_CHUNK, d), dmap)],
            out_specs=pl.BlockSpec((EXPERT_TILE, d), xmap)),
        out_shape=jax.ShapeDtypeStruct((rows, d), F32),
        compiler_params=_params("arbitrary", "arbitrary"),
        name="expert_ffn",
    )(tile_e, n_used, xs, w_gate, w_up, w_down)


def _combine_kernel(x1_ref, ya_ref, yb_ref, r_ref, g_ref, b_ref, o_ref, *, alpha):
    r = r_ref[...]
    moe = ya_ref[...] * r[:, 2:3] + yb_ref[...] * r[:, 3:4]
    o_ref[...] = _layer_norm(alpha * x1_ref[...] + moe, g_ref[...], b_ref[...])


def _combine(x1, y2, route, ln_g, ln_b, l, alpha, tm, row_block0, m):
    d = x1.shape[1]
    off2 = y2.shape[0] // TOP_K // tm
    rowo = lambda i: (i + row_block0, 0)
    lay = lambda i: (l, 0, 0)
    return pl.pallas_call(
        functools.partial(_combine_kernel, alpha=alpha),
        grid=(m // tm,),
        in_specs=[pl.BlockSpec((tm, d), rowo),
                  pl.BlockSpec((tm, d), rowo),
                  pl.BlockSpec((tm, d), lambda i: (i + row_block0 + off2, 0)),
                  pl.BlockSpec((tm, LANES), rowo),
                  pl.BlockSpec((None, 1, d), lay), pl.BlockSpec((None, 1, d), lay)],
        out_specs=pl.BlockSpec((tm, d), lambda i: (i, 0)),
        out_shape=jax.ShapeDtypeStruct((m, d), F32),
        compiler_params=_params("arbitrary"),
        name="combine",
    )(x1, y2, y2, route, ln_g, ln_b)


def _rope_tables(pos):
    half = ROT_DIM // 2
    inv = ROPE_THETA ** (-jnp.arange(half, dtype=F32) * 2.0 / ROT_DIM)
    ang = pos.astype(F32)[:, None] * inv[None, :]
    cos, sin = jnp.cos(ang), jnp.sin(ang)
    n = pos.shape[0]
    rest = HEAD_DIM - ROT_DIM
    cs = jnp.concatenate([cos, cos, jnp.ones((n, rest), F32)], axis=1)
    sa = jnp.concatenate([-sin, jnp.zeros((n, half + rest), F32)], axis=1)
    sb = jnp.concatenate([jnp.zeros((n, half), F32), sin, jnp.zeros((n, rest), F32)], axis=1)
    rep = LANES // HEAD_DIM
    return tuple(jnp.tile(t, (1, rep)) for t in (cs, sa, sb))


def _head_perm(n_heads):
    group = n_heads // N_KV_HEADS
    order = []
    for pair in range(N_KV_HEADS // 2):
        for j in range(group):
            order += [(2 * pair) * group + j, (2 * pair + 1) * group + j]
    return jnp.asarray(order, jnp.int32)


def kernel(x_prompt, x_sample, state_conv, cache_k, cache_v, w_in, conv_w, attn_sinks, w_o,
           ln1_g, ln1_b, w_router, b_router, w_gate, w_up, w_down, ln2_g, ln2_b):
    batch, seq, d = x_prompt.shape
    nseq, dec_seq, _ = x_sample.shape
    depth = w_in.shape[0]
    cdim = conv_w.shape[2]
    adim = d - cdim
    kvd = N_KV_HEADS * HEAD_DIM
    n_heads = adim // HEAD_DIM
    ne = w_router.shape[1]
    mp, ms = batch * seq, nseq * dec_seq
    total = mp + ms
    assign = total * TOP_K
    alpha = (2 * depth) ** 0.25

    perm = _head_perm(n_heads)
    qcols = (3 * cdim + perm[:, None] * HEAD_DIM + jnp.arange(HEAD_DIM)[None, :]).reshape(-1)
    in_cols = jnp.concatenate([jnp.arange(3 * cdim), qcols, jnp.arange(3 * cdim + adim, w_in.shape[2])])
    o_rows = jnp.concatenate([jnp.arange(cdim), qcols - 2 * cdim])
    sinks = attn_sinks[:, perm]
    wr_b = w_router.astype(BF16)
    br = b_router.reshape(1, ne)

    tabs_p = _rope_tables(jnp.arange(seq))
    tabs_s = _rope_tables(jnp.tile(PAST_LEN + jnp.arange(dec_seq), nseq))
    cache_k2 = cache_k.reshape(depth, nseq, WINDOW, kvd)
    cache_v2 = cache_v.reshape(depth, nseq, WINDOW, kvd)
    ln1g, ln1b = ln1_g.reshape(depth, 1, d), ln1_b.reshape(depth, 1, d)
    ln2g, ln2b = ln2_g.reshape(depth, 1, d), ln2_b.reshape(depth, 1, d)

    unit = math.lcm(EXPERT_TILE, ROW_CHUNK)
    slots = -(-((-(-assign // EXPERT_TILE) + ne) * EXPERT_TILE) // unit) * unit
    n_tiles = slots // EXPERT_TILE
    t_pad = -(-total // 512) * 512
    tok_ids = jnp.repeat(jnp.arange(total, dtype=jnp.int32), TOP_K)

    xp = x_prompt.reshape(mp, d)
    xs_ = x_sample.reshape(ms, d)
    outs = [[] for _ in range(6)]
    for l in range(depth):
        w_in_b = w_in[l][:, in_cols].astype(BF16)
        w_o_b = w_o[l][o_rows, :].astype(BF16)

        hp = _matmul(xp, w_in_b, 512, 1536, "proj_prompt")
        conv_p, q_p, k_p, v_p, cst_p, kw_p, vw_p = _prep_prompt(
            hp, conv_w, l, tabs_p, batch, seq, cdim, adim, kvd, 256)
        attn_p = _attn_prompt(sinks, q_p, k_p, v_p, l, batch, seq)

        hs = _matmul(xs_, w_in_b, ms, 1536, "proj_sample")
        st = state_conv[l]
        st1 = jnp.repeat(st[:, 1], dec_seq, axis=0)
        st2 = jnp.stack([st[:, 0], st[:, 1]] + [st[:, 1]] * (dec_seq - 2), axis=1).reshape(ms, cdim)
        conv_s, q_s, kn_s, vn_s, inner_s = _prep_sample(hs, conv_w, l, tabs_s, st1, st2, dec_seq, cdim, adim, kvd)
        attn_s, kw_s, vw_s = _attn_sample(
            sinks, q_s.reshape(nseq, dec_seq, adim), kn_s.reshape(nseq, dec_seq, kvd),
            vn_s.reshape(nseq, dec_seq, kvd), cache_k2, cache_v2, l)

        zeros_cnt = jnp.zeros((8, LANES), F32)
        x1, route, cnt = _mix_out(conv_p, attn_p, xp, w_o_b, ln1g, ln1b, wr_b, br, zeros_cnt,
                                  l, alpha, 256, total, 0)
        x1, route, cnt = _mix_out(conv_s, attn_s.reshape(ms, adim), xs_, w_o_b, ln1g, ln1b, wr_b, br, cnt,
                                  l, alpha, ms, total, mp // ms, prev=(x1, route))

        e = route[:, 0:TOP_K].astype(jnp.int32)
        rank = route[:, 4:4 + TOP_K].astype(jnp.int32)
        counts = cnt[0, :ne].astype(jnp.int32)
        padded = (counts + EXPERT_TILE - 1) // EXPERT_TILE * EXPERT_TILE
        pad_end = jnp.cumsum(padded)
        pad_start = pad_end - padded
        onehot = e[:, :, None] == jnp.arange(ne)[None, None, :]
        pos = jnp.sum(jnp.where(onehot, pad_start[None, None, :], 0), axis=-1) + rank
        n_slots = pad_end[-1:].astype(jnp.int32)
        n_used = n_slots // EXPERT_TILE
        tile_e = jnp.minimum(jnp.sum(pad_end[None, :] <= (jnp.arange(n_tiles) * EXPERT_TILE)[:, None], axis=1),
                             ne - 1).astype(jnp.int32)
        row_tok = jnp.zeros((slots,), jnp.int32).at[pos.reshape(-1)].set(tok_ids)
        pos_flat = jnp.zeros((TOP_K, t_pad), jnp.int32).at[:, :total].set(pos.T).reshape(-1)

        xsorted = _row_copy(x1, row_tok, n_slots, "gather_rows")
        yb = _expert_ffn(xsorted, w_gate, w_up, w_down, tile_e, n_used, l)
        y2 = _row_copy(yb, pos_flat, jnp.full((1,), TOP_K * t_pad, jnp.int32), "unsort_rows")

        xp = _combine(x1, y2, route, ln2g, ln2b, l, alpha, 512, 0, mp)
        xs_ = _combine(x1, y2, route, ln2g, ln2b, l, alpha, ms, mp // ms, ms)

        inner3 = inner_s.reshape(nseq, dec_seq, cdim)
        for lst, val in zip(outs, (cst_p, kw_p.reshape(batch, WINDOW, N_KV_HEADS, HEAD_DIM),
                                   vw_p.reshape(batch, WINDOW, N_KV_HEADS, HEAD_DIM),
                                   inner3[:, dec_seq - 2:],
                                   kw_s.reshape(nseq, WINDOW, N_KV_HEADS, HEAD_DIM),
                                   vw_s.reshape(nseq, WINDOW, N_KV_HEADS, HEAD_DIM))):
            lst.append(val)

    return (xp.reshape(batch, seq, d), xs_.reshape(nseq, dec_seq, d), *[jnp.stack(o) for o in outs])
```

```python
import functools
import math

import jax
import jax.numpy as jnp
from jax import lax
from jax.experimental import pallas as pl
from jax.experimental.pallas import tpu as pltpu

F32 = jnp.float32
BF16 = jnp.bfloat16

HEAD_DIM = 64
N_KV_HEADS = 4
ROT_DIM = HEAD_DIM // 4
ROPE_THETA = 500000.0
WINDOW = 128
PAST_LEN = 16384
N_EXPERT_GROUPS = 4
TOP_K = 2
LN_EPS = 1e-5

LANES = 128
VMEM_LIMIT = 56 * 1024 * 1024

EXPERT_TILE = 640
FF_CHUNK = 256
ROW_CHUNK = 1024


def _params(*sem):
    return pltpu.CompilerParams(dimension_semantics=sem, vmem_limit_bytes=VMEM_LIMIT)


def _matmul_kernel(x_ref, w_ref, o_ref):
    o_ref[...] = jnp.dot(x_ref[...].astype(BF16), w_ref[...], preferred_element_type=F32)


def _matmul(x, w, tm, tn, name):
    m, k = x.shape
    n = w.shape[1]
    return pl.pallas_call(
        _matmul_kernel,
        grid=(n // tn, m // tm),
        in_specs=[pl.BlockSpec((tm, k), lambda j, i: (i, 0)),
                  pl.BlockSpec((k, tn), lambda j, i: (0, j))],
        out_specs=pl.BlockSpec((tm, tn), lambda j, i: (i, j)),
        out_shape=jax.ShapeDtypeStruct((m, n), F32),
        compiler_params=_params("arbitrary", "arbitrary"),
        name=name,
    )(x, w)


def _rope_chunk(xc, cs, sa, sb):
    return xc * cs + pltpu.roll(xc, LANES - ROT_DIM // 2, 1) * sa + pltpu.roll(xc, ROT_DIM // 2, 1) * sb


def _conv(inner, gate, w_ref, prev1, prev2):
    return gate * (w_ref[0:1, :] * prev2 + w_ref[1:2, :] * prev1 + w_ref[2:3, :] * inner)


def _prep_prompt_kernel(hc, hb, hu, hq, hkv, cw, cs_ref, sa_ref, sb_ref,
                        conv_o, q_o, k_o, v_o, cst_o, kw_o, vw_o, carry, *, tiles_per_seq):
    tm, cdim = hc.shape

    @pl.when(pl.program_id(0) % tiles_per_seq == 0)
    def _():
        carry[...] = jnp.zeros_like(carry)

    inner = hc[...] * hu[...]
    row = lax.broadcasted_iota(jnp.int32, (tm, cdim), 0)
    c1 = carry[7:8, :]
    c2 = carry[6:7, :]
    prev1 = jnp.where(row == 0, c1, pltpu.roll(inner, 1, 0))
    prev2 = jnp.where(row == 0, c2, jnp.where(row == 1, c1, pltpu.roll(inner, 2, 0)))
    conv_o[...] = _conv(inner, hb[...], cw, prev1, prev2).astype(BF16)
    carry[...] = inner[tm - 8:, :]
    cst_o[...] = inner[tm - 2:, :]

    cs, sa, sb = cs_ref[...], sa_ref[...], sb_ref[...]
    for c in range(hq.shape[1] // LANES):
        sl = slice(c * LANES, (c + 1) * LANES)
        q_o[:, sl] = _rope_chunk(hq[:, sl], cs, sa, sb).astype(BF16)
    kvd = hkv.shape[1] // 2
    for c in range(kvd // LANES):
        sl = slice(c * LANES, (c + 1) * LANES)
        kr = _rope_chunk(hkv[:, sl], cs, sa, sb)
        k_o[:, sl] = kr.astype(BF16)
        kw_o[:, sl] = kr[tm - WINDOW:, :]
    v = hkv[:, kvd:]
    v_o[...] = v.astype(BF16)
    vw_o[...] = v[tm - WINDOW:, :]


def _prep_prompt(h, conv_w, l, tabs, batch, seq, cdim, adim, kvd, tm):
    m = h.shape[0]
    tps = seq // tm
    qblk = 3 * cdim // adim
    kvblk = (3 * cdim + adim) // (2 * kvd)
    row = lambda i: (i, 0)
    return pl.pallas_call(
        functools.partial(_prep_prompt_kernel, tiles_per_seq=tps),
        grid=(m // tm,),
        in_specs=[pl.BlockSpec((tm, cdim), lambda i: (i, 0)),
                  pl.BlockSpec((tm, cdim), lambda i: (i, 1)),
                  pl.BlockSpec((tm, cdim), lambda i: (i, 2)),
                  pl.BlockSpec((tm, adim), lambda i: (i, qblk)),
                  pl.BlockSpec((tm, 2 * kvd), lambda i: (i, kvblk)),
                  pl.BlockSpec((None, 3, cdim), lambda i: (l, 0, 0)),
                  pl.BlockSpec((tm, LANES), lambda i: (i % tps, 0)),
                  pl.BlockSpec((tm, LANES), lambda i: (i % tps, 0)),
                  pl.BlockSpec((tm, LANES), lambda i: (i % tps, 0))],
        out_specs=[pl.BlockSpec((tm, cdim), row),
                   pl.BlockSpec((tm, adim), row),
                   pl.BlockSpec((tm, kvd), row),
                   pl.BlockSpec((tm, kvd), row),
                   pl.BlockSpec((None, 2, cdim), lambda i: (i // tps, 0, 0)),
                   pl.BlockSpec((None, WINDOW, kvd), lambda i: (i // tps, 0, 0)),
                   pl.BlockSpec((None, WINDOW, kvd), lambda i: (i // tps, 0, 0))],
        out_shape=[jax.ShapeDtypeStruct((m, cdim), BF16),
                   jax.ShapeDtypeStruct((m, adim), BF16),
                   jax.ShapeDtypeStruct((m, kvd), BF16),
                   jax.ShapeDtypeStruct((m, kvd), BF16),
                   jax.ShapeDtypeStruct((batch, 2, cdim), F32),
                   jax.ShapeDtypeStruct((batch, WINDOW, kvd), F32),
                   jax.ShapeDtypeStruct((batch, WINDOW, kvd), F32)],
        scratch_shapes=[pltpu.VMEM((8, cdim), F32)],
        compiler_params=_params("arbitrary"),
        name="prep_prompt",
    )(h, h, h, h, h, conv_w, *tabs)


def _prep_sample_kernel(hc, hb, hu, hq, hkv, cw, cs_ref, sa_ref, sb_ref, st1, st2,
                        conv_o, q_o, k_o, v_o, inner_o, *, dec_seq):
    tm, cdim = hc.shape
    inner = hc[...] * hu[...]
    t = lax.broadcasted_iota(jnp.int32, (tm, cdim), 0) % dec_seq
    prev1 = jnp.where(t == 0, st1[...], pltpu.roll(inner, 1, 0))
    prev2 = jnp.where(t < 2, st2[...], pltpu.roll(inner, 2, 0))
    conv_o[...] = _conv(inner, hb[...], cw, prev1, prev2).astype(BF16)
    inner_o[...] = inner
    cs, sa, sb = cs_ref[...], sa_ref[...], sb_ref[...]
    for c in range(hq.shape[1] // LANES):
        sl = slice(c * LANES, (c + 1) * LANES)
        q_o[:, sl] = _rope_chunk(hq[:, sl], cs, sa, sb)
    kvd = hkv.shape[1] // 2
    for c in range(kvd // LANES):
        sl = slice(c * LANES, (c + 1) * LANES)
        k_o[:, sl] = _rope_chunk(hkv[:, sl], cs, sa, sb)
    v_o[...] = hkv[:, kvd:]


def _prep_sample(h, conv_w, l, tabs, st1, st2, dec_seq, cdim, adim, kvd):
    m = h.shape[0]
    qblk = 3 * cdim // adim
    kvblk = (3 * cdim + adim) // (2 * kvd)
    z = lambda i: (0, 0)
    return pl.pallas_call(
        functools.partial(_prep_sample_kernel, dec_seq=dec_seq),
        grid=(1,),
        in_specs=[pl.BlockSpec((m, cdim), lambda i: (0, 0)),
                  pl.BlockSpec((m, cdim), lambda i: (0, 1)),
                  pl.BlockSpec((m, cdim), lambda i: (0, 2)),
                  pl.BlockSpec((m, adim), lambda i: (0, qblk)),
                  pl.BlockSpec((m, 2 * kvd), lambda i: (0, kvblk)),
                  pl.BlockSpec((None, 3, cdim), lambda i: (l, 0, 0)),
                  pl.BlockSpec((m, LANES), z), pl.BlockSpec((m, LANES), z), pl.BlockSpec((m, LANES), z),
                  pl.BlockSpec((m, cdim), z), pl.BlockSpec((m, cdim), z)],
        out_specs=[pl.BlockSpec((m, cdim), z), pl.BlockSpec((m, adim), z),
                   pl.BlockSpec((m, kvd), z), pl.BlockSpec((m, kvd), z), pl.BlockSpec((m, cdim), z)],
        out_shape=[jax.ShapeDtypeStruct((m, cdim), BF16),
                   jax.ShapeDtypeStruct((m, adim), F32),
                   jax.ShapeDtypeStruct((m, kvd), F32),
                   jax.ShapeDtypeStruct((m, kvd), F32),
                   jax.ShapeDtypeStruct((m, cdim), F32)],
        compiler_params=_params("arbitrary"),
        name="prep_sample",
    )(h, h, h, h, h, conv_w, *tabs, st1, st2)


def _sink_softmax(s, mask, sink_col):
    s = jnp.where(mask, s, -jnp.inf)
    m = jnp.maximum(jnp.max(s, axis=-1, keepdims=True), sink_col)
    p = jnp.exp(s - m)
    den = jnp.sum(p, axis=-1, keepdims=True) + jnp.exp(sink_col - m)
    return p * (1.0 / den)


def _attn_prompt_kernel(sink_ref, q_ref, kp_ref, kc_ref, vp_ref, vc_ref, o_ref, *, l, scale):
    n = pl.program_id(1)
    w = q_ref.shape[0]
    chunks = q_ref.shape[1] // LANES
    pairs = kc_ref.shape[1] // LANES
    cpp = chunks // pairs
    rows = 2 * cpp * w
    lo = lax.broadcasted_iota(jnp.int32, (w, LANES), 1) < HEAD_DIM
    qi = lax.broadcasted_iota(jnp.int32, (rows, 2 * w), 0) % w
    sj = lax.broadcasted_iota(jnp.int32, (rows, 2 * w), 1)
    mask = (sj <= qi + w) & (sj > qi) & ((n > 0) | (sj >= w))
    zero = jnp.zeros((w, LANES), BF16)
    for p in range(pairs):
        ksl = slice(p * LANES, (p + 1) * LANES)
        kk = jnp.concatenate([kp_ref[:, ksl], kc_ref[:, ksl]], axis=0)
        vv = jnp.concatenate([vp_ref[:, ksl], vc_ref[:, ksl]], axis=0)
        pieces, sinks = [], []
        for j in range(cpp):
            c = p * cpp + j
            qc = q_ref[:, c * LANES:(c + 1) * LANES]
            pieces += [jnp.where(lo, qc, zero), jnp.where(lo, zero, qc)]
            sinks += [jnp.full((w, 1), sink_ref[l, 2 * c], F32), jnp.full((w, 1), sink_ref[l, 2 * c + 1], F32)]
        qs = jnp.concatenate(pieces, axis=0)
        s = lax.dot_general(qs, kk, (((1,), (1,)), ((), ())), preferred_element_type=F32) * scale
        pr = _sink_softmax(s, mask, jnp.concatenate(sinks, axis=0))
        o = jnp.dot(pr.astype(BF16), vv, preferred_element_type=F32)
        for j in range(cpp):
            c = p * cpp + j
            o_ref[:, c * LANES:(c + 1) * LANES] = jnp.where(
                lo, o[2 * j * w:(2 * j + 1) * w], o[(2 * j + 1) * w:(2 * j + 2) * w]).astype(BF16)


def _attn_prompt(sinks, q, k, v, l, batch, seq):
    m, adim = q.shape
    kvd = k.shape[1]
    nb = seq // WINDOW
    cur = lambda b, n: (b * nb + n, 0)
    prev = lambda b, n: (b * nb + jnp.maximum(n - 1, 0), 0)
    return pl.pallas_call(
        functools.partial(_attn_prompt_kernel, l=l, scale=HEAD_DIM ** -0.5),
        grid=(batch, nb),
        in_specs=[pl.BlockSpec(memory_space=pltpu.SMEM),
                  pl.BlockSpec((WINDOW, adim), cur),
                  pl.BlockSpec((WINDOW, kvd), prev), pl.BlockSpec((WINDOW, kvd), cur),
                  pl.BlockSpec((WINDOW, kvd), prev), pl.BlockSpec((WINDOW, kvd), cur)],
        out_specs=pl.BlockSpec((WINDOW, adim), cur),
        out_shape=jax.ShapeDtypeStruct((m, adim), BF16),
        compiler_params=_params("arbitrary", "arbitrary"),
        name="attn_prompt",
    )(sinks, q, k, k, v, v)


def _attn_sample_kernel(sink_ref, q_ref, kn_ref, vn_ref, ck_ref, cv_ref,
                        o_ref, kw_ref, vw_ref, kk_ref, vv_ref, qs_ref, *, l, scale):
    t = q_ref.shape[0]
    w = ck_ref.shape[0]
    tot = kk_ref.shape[0]
    chunks = q_ref.shape[1] // LANES
    pairs = ck_ref.shape[1] // LANES
    cpp = chunks // pairs
    rows = 2 * cpp * t
    kk_ref[0:w, :] = ck_ref[...]
    vv_ref[0:w, :] = cv_ref[...]
    pad = jnp.zeros((tot - w - t, kk_ref.shape[1]), F32)
    kk_ref[w:w + t, :] = kn_ref[...]
    vv_ref[w:w + t, :] = vn_ref[...]
    kk_ref[w + t:, :] = pad
    vv_ref[w + t:, :] = pad
    kw_ref[...] = pltpu.roll(kk_ref[...], tot - t, 0)[0:w, :]
    vw_ref[...] = pltpu.roll(vv_ref[...], tot - t, 0)[0:w, :]

    lo = lax.broadcasted_iota(jnp.int32, (t, LANES), 1) < HEAD_DIM
    qi = lax.broadcasted_iota(jnp.int32, (rows, tot), 0) % t
    sj = lax.broadcasted_iota(jnp.int32, (rows, tot), 1)
    mask = (sj <= qi + w) & (sj > qi)
    rowid = lax.broadcasted_iota(jnp.int32, (rows, 1), 0) // t
    for p in range(pairs):
        ksl = slice(p * LANES, (p + 1) * LANES)
        sink_col = jnp.zeros((rows, 1), F32)
        for j in range(cpp):
            c = p * cpp + j
            qc = q_ref[:, c * LANES:(c + 1) * LANES]
            qs_ref[(2 * j) * t:(2 * j + 1) * t, :] = jnp.where(lo, qc, 0.0)
            qs_ref[(2 * j + 1) * t:(2 * j + 2) * t, :] = jnp.where(lo, 0.0, qc)
            sink_col = jnp.where(rowid == 2 * j, sink_ref[l, 2 * c], sink_col)
            sink_col = jnp.where(rowid == 2 * j + 1, sink_ref[l, 2 * c + 1], sink_col)
        s = lax.dot_general(qs_ref[...].astype(BF16), kk_ref[:, ksl].astype(BF16),
                            (((1,), (1,)), ((), ())), preferred_element_type=F32) * scale
        pr = _sink_softmax(s, mask, sink_col)
        qs_ref[...] = jnp.dot(pr.astype(BF16), vv_ref[:, ksl].astype(BF16), preferred_element_type=F32)
        for j in range(cpp):
            c = p * cpp + j
            o_ref[:, c * LANES:(c + 1) * LANES] = jnp.where(
                lo, qs_ref[(2 * j) * t:(2 * j + 1) * t, :], qs_ref[(2 * j + 1) * t:(2 * j + 2) * t, :]).astype(BF16)


def _attn_sample(sinks, q, kn, vn, cache_k, cache_v, l):
    nseq, t, adim = q.shape
    w, kvd = cache_k.shape[2], cache_k.shape[3]
    tot = -(-(w + t) // 8) * 8
    rows = 2 * (adim // kvd) * t
    seq3 = lambda b: (b, 0, 0)
    cache = lambda b: (l, b, 0, 0)
    return pl.pallas_call(
        functools.partial(_attn_sample_kernel, l=l, scale=HEAD_DIM ** -0.5),
        grid=(nseq,),
        in_specs=[pl.BlockSpec(memory_space=pltpu.SMEM),
                  pl.BlockSpec((None, t, adim), seq3),
                  pl.BlockSpec((None, t, kvd), seq3), pl.BlockSpec((None, t, kvd), seq3),
                  pl.BlockSpec((None, None, w, kvd), cache), pl.BlockSpec((None, None, w, kvd), cache)],
        out_specs=[pl.BlockSpec((None, t, adim), seq3),
                   pl.BlockSpec((None, w, kvd), seq3), pl.BlockSpec((None, w, kvd), seq3)],
        out_shape=[jax.ShapeDtypeStruct((nseq, t, adim), BF16),
                   jax.ShapeDtypeStruct((nseq, w, kvd), F32),
                   jax.ShapeDtypeStruct((nseq, w, kvd), F32)],
        scratch_shapes=[pltpu.VMEM((tot, kvd), F32), pltpu.VMEM((tot, kvd), F32),
                        pltpu.VMEM((rows, LANES), F32)],
        compiler_params=_params("arbitrary"),
        name="attn_sample",
    )(sinks, q, kn, vn, cache_k, cache_v)


def _layer_norm(z, g, b):
    mu = jnp.mean(z, axis=-1, keepdims=True)
    d = z - mu
    var = jnp.mean(d * d, axis=-1, keepdims=True)
    return d * lax.rsqrt(var + LN_EPS) * g + b


def _top2(sg, lane, n):
    m1 = jnp.max(sg, axis=-1, keepdims=True)
    i1 = jnp.min(jnp.where(sg == m1, lane, n), axis=-1, keepdims=True)
    rest = jnp.where(lane == i1, -jnp.inf, sg)
    m2 = jnp.max(rest, axis=-1, keepdims=True)
    i2 = jnp.min(jnp.where(rest == m2, lane, n), axis=-1, keepdims=True)
    return m1, i1, m2, i2


def _mix_out_kernel(conv_ref, attn_ref, x_ref, wo_ref, g_ref, b_ref, wr_ref, br_ref, cnt_in,
                    x1_ref, route_ref, cnt_ref, carry, *, l, alpha, has_alias):
    tm = x_ref.shape[0]
    cdim = conv_ref.shape[1]
    ne = wr_ref.shape[1]
    epg = ne // N_EXPERT_GROUPS

    @pl.when(pl.program_id(0) == 0)
    def _():
        carry[...] = cnt_in[...]

    y = jnp.dot(conv_ref[...], wo_ref[0:cdim, :], preferred_element_type=F32)
    y = y + jnp.dot(attn_ref[...], wo_ref[cdim:, :], preferred_element_type=F32)
    x1 = _layer_norm(alpha * x_ref[...] + y, g_ref[...], b_ref[...])
    x1_ref[...] = x1

    logits = jnp.dot(x1.astype(BF16), wr_ref[...], preferred_element_type=F32)
    scores = jax.nn.sigmoid(logits)
    sel = scores + br_ref[...]
    lane = lax.broadcasted_iota(jnp.int32, (tm, ne), 1).astype(F32)
    grp = jnp.floor(lane * (1.0 / epg))
    best_s = None
    best_g = None
    for g in range(N_EXPERT_GROUPS):
        m1, _, m2, _ = _top2(jnp.where(grp == g, sel, -jnp.inf), lane, float(ne))
        gs = m1 + m2
        if g == 0:
            best_s, best_g = gs, jnp.zeros_like(gs)
        else:
            upd = gs > best_s
            best_g = jnp.where(upd, float(g), best_g)
            best_s = jnp.where(upd, gs, best_s)
    _, e1, _, e2 = _top2(jnp.where(grp == best_g, sel, -jnp.inf), lane, float(ne))
    hit1 = lane == e1
    hit2 = lane == e2
    w1 = jnp.sum(jnp.where(hit1, scores, 0.0), axis=-1, keepdims=True)
    w2 = jnp.sum(jnp.where(hit2, scores, 0.0), axis=-1, keepdims=True)
    wsum = w1 + w2

    onehot = jnp.where(hit1 | hit2, 1.0, 0.0)
    tri = (lax.broadcasted_iota(jnp.int32, (tm, tm), 0) > lax.broadcasted_iota(jnp.int32, (tm, tm), 1))
    before = jnp.dot(jnp.where(tri, 1.0, 0.0).astype(BF16), onehot.astype(BF16), preferred_element_type=F32)
    before = before + carry[0:1, 0:ne]
    r1 = jnp.sum(jnp.where(hit1, before, 0.0), axis=-1, keepdims=True)
    r2 = jnp.sum(jnp.where(hit2, before, 0.0), axis=-1, keepdims=True)
    carry[0:1, 0:ne] = carry[0:1, 0:ne] + jnp.sum(onehot, axis=0, keepdims=True)
    cnt_ref[...] = carry[...]

    col = lax.broadcasted_iota(jnp.int32, (tm, LANES), 1)
    out = jnp.zeros((tm, LANES), F32)
    for c, val in enumerate((e1, e2, w1 / wsum, w2 / wsum, r1, r2)):
        out = jnp.where(col == c, val, out)
    route_ref[...] = out


def _mix_out(conv, attn, x, wo, ln_g, ln_b, wr, br, cnt_in, l, alpha, tm, total_rows, row_block0, prev=None):
    m, d = x.shape
    cdim = conv.shape[1]
    ne = wr.shape[1]
    rowi = lambda i: (i, 0)
    rowo = lambda i: (i + row_block0, 0)
    const = lambda i: (0, 0)
    lay = lambda i: (l, 0, 0)
    in_specs = [pl.BlockSpec((tm, cdim), rowi), pl.BlockSpec((tm, d - cdim), rowi), pl.BlockSpec((tm, d), rowi),
                pl.BlockSpec((d, d), const, pipeline_mode=pl.Buffered(1)),
                pl.BlockSpec((None, 1, d), lay), pl.BlockSpec((None, 1, d), lay),
                pl.BlockSpec((d, ne), const), pl.BlockSpec((1, ne), const),
                pl.BlockSpec((8, LANES), const)]
    args = [conv, attn, x, wo, ln_g, ln_b, wr, br, cnt_in]
    aliases = {}
    if prev is not None:
        in_specs += [pl.BlockSpec(memory_space=pl.ANY), pl.BlockSpec(memory_space=pl.ANY)]
        args += list(prev)
        aliases = {len(args) - 2: 0, len(args) - 1: 1}

    def body(*refs):
        ins = refs[:9]
        outs = refs[-4:]
        _mix_out_kernel(*ins, *outs, l=l, alpha=alpha, has_alias=prev is not None)

    return pl.pallas_call(
        body,
        grid=(m // tm,),
        in_specs=in_specs,
        out_specs=[pl.BlockSpec((tm, d), rowo), pl.BlockSpec((tm, LANES), rowo), pl.BlockSpec((8, LANES), const)],
        out_shape=[jax.ShapeDtypeStruct((total_rows, d), F32),
                   jax.ShapeDtypeStruct((total_rows, LANES), F32),
                   jax.ShapeDtypeStruct((8, LANES), F32)],
        scratch_shapes=[pltpu.VMEM((8, LANES), F32)],
        input_output_aliases=aliases,
        compiler_params=_params("arbitrary"),
        name="mix_out_sample" if prev is not None else "mix_out_prompt",
    )(*args)


def _row_copy_kernel(n_ref, idx_ref, src_ref, dst_ref, sem, *, chunk):
    base = pl.program_id(0) * chunk
    n = jnp.clip(n_ref[0] - base, 0, chunk)

    def row_dma(src_row, dst_row):
        return pltpu.make_async_copy(src_ref.at[pl.ds(src_row, 1)], dst_ref.at[pl.ds(dst_row, 1)], sem)

    def issue(j, carry):
        row_dma(idx_ref[j], j).start()
        return carry

    def drain(j, carry):
        row_dma(0, 0).wait()
        return carry

    lax.fori_loop(0, n, issue, 0)
    lax.fori_loop(0, n, drain, 0)


def _row_copy(src, idx, n_valid, name):
    n_rows = idx.shape[0]
    return pl.pallas_call(
        functools.partial(_row_copy_kernel, chunk=ROW_CHUNK),
        grid_spec=pltpu.PrefetchScalarGridSpec(
            num_scalar_prefetch=1,
            grid=(n_rows // ROW_CHUNK,),
            in_specs=[pl.BlockSpec((ROW_CHUNK,), lambda c, n: (c,), memory_space=pltpu.SMEM),
                      pl.BlockSpec(memory_space=pl.ANY)],
            out_specs=pl.BlockSpec(
                (ROW_CHUNK, src.shape[1]), lambda c, n: (jnp.minimum(c, (n[0] - 1) // ROW_CHUNK), 0)),
            scratch_shapes=[pltpu.SemaphoreType.DMA(())]),
        out_shape=jax.ShapeDtypeStruct((n_rows, src.shape[1]), src.dtype),
        compiler_params=_params("arbitrary"),
        name=name,
    )(n_valid, idx, src)


def _expert_kernel(te_ref, nu_ref, x_ref, wg_ref, wu_ref, wd_ref, o_ref):
    i = pl.program_id(0)
    f = pl.program_id(1)

    @pl.when(i < nu_ref[0])
    def _():
        x = x_ref[...].astype(BF16)
        g = jnp.dot(x, wg_ref[...].astype(BF16), preferred_element_type=F32)
        u = jnp.dot(x, wu_ref[...].astype(BF16), preferred_element_type=F32)
        h = (jax.nn.silu(g) * u).astype(BF16)
        y = jnp.dot(h, wd_ref[...].astype(BF16), preferred_element_type=F32)

        @pl.when(f == 0)
        def _():
            o_ref[...] = y

        @pl.when(f > 0)
        def _():
            o_ref[...] += y


def _expert_ffn(xs, w_gate, w_up, w_down, tile_e, n_used, l):
    rows, d = xs.shape
    dff = w_gate.shape[3]
    nf = dff // FF_CHUNK
    nt = rows // EXPERT_TILE

    def tile(i, nu):
        return jnp.minimum(i, nu[0] - 1)

    def ffc(i, f, nu):
        return jnp.where(i < nu[0], f, nf - 1)

    xmap = lambda i, f, te, nu: (tile(i, nu), 0)
    gmap = lambda i, f, te, nu: (l, te[tile(i, nu)], 0, ffc(i, f, nu))
    dmap = lambda i, f, te, nu: (l, te[tile(i, nu)], ffc(i, f, nu), 0)
    return pl.pallas_call(
        _expert_kernel,
        grid_spec=pltpu.PrefetchScalarGridSpec(
            num_scalar_prefetch=2,
            grid=(nt, nf),
            in_specs=[pl.BlockSpec((EXPERT_TILE, d), xmap),
                      pl.BlockSpec((None, None, d, FF_CHUNK), gmap),
                      pl.BlockSpec((None, None, d, FF_CHUNK), gmap),
                      pl.BlockSpec((None, None, FF_CHUNK, d), dmap)],
            out_specs=pl.BlockSpec((EXPERT_TILE, d), xmap)),
        out_shape=jax.ShapeDtypeStruct((rows, d), F32),
        compiler_params=_params("arbitrary", "arbitrary"),
        name="expert_ffn",
    )(tile_e, n_used, xs, w_gate, w_up, w_down)


def _combine_kernel(x1_ref, ya_ref, yb_ref, r_ref, g_ref, b_ref, o_ref, *, alpha):
    r = r_ref[...]
    moe = ya_ref[...] * r[:, 2:3] + yb_ref[...] * r[:, 3:4]
    o_ref[...] = _layer_norm(alpha * x1_ref[...] + moe, g_ref[...], b_ref[...])


def _combine(x1, y2, route, ln_g, ln_b, l, alpha, tm, row_block0, m):
    d = x1.shape[1]
    off2 = y2.shape[0] // TOP_K // tm
    rowo = lambda i: (i + row_block0, 0)
    lay = lambda i: (l, 0, 0)
    return pl.pallas_call(
        functools.partial(_combine_kernel, alpha=alpha),
        grid=(m // tm,),
        in_specs=[pl.BlockSpec((tm, d), rowo),
                  pl.BlockSpec((tm, d), rowo),
                  pl.BlockSpec((tm, d), lambda i: (i + row_block0 + off2, 0)),
                  pl.BlockSpec((tm, LANES), rowo),
                  pl.BlockSpec((None, 1, d), lay), pl.BlockSpec((None, 1, d), lay)],
        out_specs=pl.BlockSpec((tm, d), lambda i: (i, 0)),
        out_shape=jax.ShapeDtypeStruct((m, d), F32),
        compiler_params=_params("arbitrary"),
        name="combine",
    )(x1, y2, y2, route, ln_g, ln_b)


def _rope_tables(pos):
    half = ROT_DIM // 2
    inv = ROPE_THETA ** (-jnp.arange(half, dtype=F32) * 2.0 / ROT_DIM)
    ang = pos.astype(F32)[:, None] * inv[None, :]
    cos, sin = jnp.cos(ang), jnp.sin(ang)
    n = pos.shape[0]
    rest = HEAD_DIM - ROT_DIM
    cs = jnp.concatenate([cos, cos, jnp.ones((n, rest), F32)], axis=1)
    sa = jnp.concatenate([-sin, jnp.zeros((n, half + rest), F32)], axis=1)
    sb = jnp.concatenate([jnp.zeros((n, half), F32), sin, jnp.zeros((n, rest), F32)], axis=1)
    rep = LANES // HEAD_DIM
    return tuple(jnp.tile(t, (1, rep)) for t in (cs, sa, sb))


def _head_perm(n_heads):
    group = n_heads // N_KV_HEADS
    order = []
    for pair in range(N_KV_HEADS // 2):
        for j in range(group):
            order += [(2 * pair) * group + j, (2 * pair + 1) * group + j]
    return jnp.asarray(order, jnp.int32)


def kernel(x_prompt, x_sample, state_conv, cache_k, cache_v, w_in, conv_w, attn_sinks, w_o,
           ln1_g, ln1_b, w_router, b_router, w_gate, w_up, w_down, ln2_g, ln2_b):
    batch, seq, d = x_prompt.shape
    nseq, dec_seq, _ = x_sample.shape
    depth = w_in.shape[0]
    cdim = conv_w.shape[2]
    adim = d - cdim
    kvd = N_KV_HEADS * HEAD_DIM
    n_heads = adim // HEAD_DIM
    ne = w_router.shape[1]
    mp, ms = batch * seq, nseq * dec_seq
    total = mp + ms
    assign = total * TOP_K
    alpha = (2 * depth) ** 0.25

    perm = _head_perm(n_heads)
    qcols = (3 * cdim + perm[:, None] * HEAD_DIM + jnp.arange(HEAD_DIM)[None, :]).reshape(-1)
    in_cols = jnp.concatenate([jnp.arange(3 * cdim), qcols, jnp.arange(3 * cdim + adim, w_in.shape[2])])
    o_rows = jnp.concatenate([jnp.arange(cdim), qcols - 2 * cdim])
    sinks = attn_sinks[:, perm]
    wr_b = w_router.astype(BF16)
    br = b_router.reshape(1, ne)

    tabs_p = _rope_tables(jnp.arange(seq))
    tabs_s = _rope_tables(jnp.tile(PAST_LEN + jnp.arange(dec_seq), nseq))
    cache_k2 = cache_k.reshape(depth, nseq, WINDOW, kvd)
    cache_v2 = cache_v.reshape(depth, nseq, WINDOW, kvd)
    ln1g, ln1b = ln1_g.reshape(depth, 1, d), ln1_b.reshape(depth, 1, d)
    ln2g, ln2b = ln2_g.reshape(depth, 1, d), ln2_b.reshape(depth, 1, d)

    unit = math.lcm(EXPERT_TILE, ROW_CHUNK)
    slots = -(-((-(-assign // EXPERT_TILE) + ne) * EXPERT_TILE) // unit) * unit
    n_tiles = slots // EXPERT_TILE
    t_pad = -(-total // 512) * 512
    tok_ids = jnp.repeat(jnp.arange(total, dtype=jnp.int32), TOP_K)

    xp = x_prompt.reshape(mp, d)
    xs_ = x_sample.reshape(ms, d)
    outs = [[] for _ in range(6)]
    for l in range(depth):
        w_in_b = w_in[l][:, in_cols].astype(BF16)
        w_o_b = w_o[l][o_rows, :].astype(BF16)

        hp = _matmul(xp, w_in_b, 512, 1536, "proj_prompt")
        conv_p, q_p, k_p, v_p, cst_p, kw_p, vw_p = _prep_prompt(
            hp, conv_w, l, tabs_p, batch, seq, cdim, adim, kvd, 256)
        attn_p = _attn_prompt(sinks, q_p, k_p, v_p, l, batch, seq)

        hs = _matmul(xs_, w_in_b, ms, 1536, "proj_sample")
        st = state_conv[l]
        st1 = jnp.repeat(st[:, 1], dec_seq, axis=0)
        st2 = jnp.stack([st[:, 0], st[:, 1]] + [st[:, 1]] * (dec_seq - 2), axis=1).reshape(ms, cdim)
        conv_s, q_s, kn_s, vn_s, inner_s = _prep_sample(hs, conv_w, l, tabs_s, st1, st2, dec_seq, cdim, adim, kvd)
        attn_s, kw_s, vw_s = _attn_sample(
            sinks, q_s.reshape(nseq, dec_seq, adim), kn_s.reshape(nseq, dec_seq, kvd),
            vn_s.reshape(nseq, dec_seq, kvd), cache_k2, cache_v2, l)

        zeros_cnt = jnp.zeros((8, LANES), F32)
        x1, route, cnt = _mix_out(conv_p, attn_p, xp, w_o_b, ln1g, ln1b, wr_b, br, zeros_cnt,
                                  l, alpha, 256, total, 0)
        x1, route, cnt = _mix_out(conv_s, attn_s.reshape(ms, adim), xs_, w_o_b, ln1g, ln1b, wr_b, br, cnt,
                                  l, alpha, ms, total, mp // ms, prev=(x1, route))

        e = route[:, 0:TOP_K].astype(jnp.int32)
        rank = route[:, 4:4 + TOP_K].astype(jnp.int32)
        counts = cnt[0, :ne].astype(jnp.int32)
        padded = (counts + EXPERT_TILE - 1) // EXPERT_TILE * EXPERT_TILE
        pad_end = jnp.cumsum(padded)
        pad_start = pad_end - padded
        onehot = e[:, :, None] == jnp.arange(ne)[None, None, :]
        pos = jnp.sum(jnp.where(onehot, pad_start[None, None, :], 0), axis=-1) + rank
        n_slots = pad_end[-1:].astype(jnp.int32)
        n_used = n_slots // EXPERT_TILE
        tile_e = jnp.minimum(jnp.sum(pad_end[None, :] <= (jnp.arange(n_tiles) * EXPERT_TILE)[:, None], axis=1),
                             ne - 1).astype(jnp.int32)
        row_tok = jnp.zeros((slots,), jnp.int32).at[pos.reshape(-1)].set(tok_ids)
        pos_flat = jnp.zeros((TOP_K, t_pad), jnp.int32).at[:, :total].set(pos.T).reshape(-1)

        xsorted = _row_copy(x1, row_tok, n_slots, "gather_rows")
        yb = _expert_ffn(xsorted, w_gate, w_up, w_down, tile_e, n_used, l)
        y2 = _row_copy(yb, pos_flat, jnp.full((1,), TOP_K * t_pad, jnp.int32), "unsort_rows")

        xp = _combine(x1, y2, route, ln2g, ln2b, l, alpha, 512, 0, mp)
        xs_ = _combine(x1, y2, route, ln2g, ln2b, l, alpha, ms, mp // ms, ms)

        inner3 = inner_s.reshape(nseq, dec_seq, cdim)
        for lst, val in zip(outs, (cst_p, kw_p.reshape(batch, WINDOW, N_KV_HEADS, HEAD_DIM),
                                   vw_p.reshape(batch, WINDOW, N_KV_HEADS, HEAD_DIM),
                                   inner3[:, dec_seq - 2:],
                                   kw_s.reshape(nseq, WINDOW, N_KV_HEADS, HEAD_DIM),
                                   vw_s.reshape(nseq, WINDOW, N_KV_HEADS, HEAD_DIM))):
            lst.append(val)

    return (xp.reshape(batch, seq, d), xs_.reshape(nseq, dec_seq, d), *[jnp.stack(o) for o in outs])
```

```python
import functools

import jax
import jax.numpy as jnp
from jax import lax
from jax.experimental import pallas as pl
from jax.experimental.pallas import tpu as pltpu

F32 = jnp.float32
BF16 = jnp.bfloat16

HEAD_DIM = 64
N_KV_HEADS = 4
ROT_DIM = HEAD_DIM // 4
ROPE_THETA = 500000.0
WINDOW = 128
PAST_LEN = 16384
N_EXPERT_GROUPS = 4
TOP_K = 2
LN_EPS = 1e-5

LANES = 128
VMEM_LIMIT = 56 * 1024 * 1024

EXPERT_TILE = 640
FF_CHUNK = 256
SLOT_BLOCK = 1024
ISSUE_UNROLL = 8


def _params(*sem):
    return pltpu.CompilerParams(dimension_semantics=sem, vmem_limit_bytes=VMEM_LIMIT)


def _matmul_kernel(x_ref, w_ref, o_ref):
    o_ref[...] = jnp.dot(x_ref[...].astype(BF16), w_ref[...], preferred_element_type=F32)


def _matmul(x, w, tm, tn, name):
    m, k = x.shape
    n = w.shape[1]
    return pl.pallas_call(
        _matmul_kernel,
        grid=(n // tn, m // tm),
        in_specs=[pl.BlockSpec((tm, k), lambda j, i: (i, 0)),
                  pl.BlockSpec((k, tn), lambda j, i: (0, j))],
        out_specs=pl.BlockSpec((tm, tn), lambda j, i: (i, j)),
        out_shape=jax.ShapeDtypeStruct((m, n), F32),
        compiler_params=_params("arbitrary", "arbitrary"),
        name=name,
    )(x, w)


def _rope_chunk(xc, cs, sa, sb):
    return xc * cs + pltpu.roll(xc, LANES - ROT_DIM // 2, 1) * sa + pltpu.roll(xc, ROT_DIM // 2, 1) * sb


def _conv(inner, gate, w_ref, prev1, prev2):
    return gate * (w_ref[0:1, :] * prev2 + w_ref[1:2, :] * prev1 + w_ref[2:3, :] * inner)


def _prep_prompt_kernel(hc, hb, hu, hq, hkv, cw, cs_ref, sa_ref, sb_ref,
                        conv_o, q_o, k_o, v_o, cst_o, kw_o, vw_o, carry, *, tiles_per_seq):
    tm, cdim = hc.shape

    @pl.when(pl.program_id(0) % tiles_per_seq == 0)
    def _():
        carry[...] = jnp.zeros_like(carry)

    inner = hc[...] * hu[...]
    row = lax.broadcasted_iota(jnp.int32, (tm, cdim), 0)
    c1 = carry[7:8, :]
    c2 = carry[6:7, :]
    prev1 = jnp.where(row == 0, c1, pltpu.roll(inner, 1, 0))
    prev2 = jnp.where(row == 0, c2, jnp.where(row == 1, c1, pltpu.roll(inner, 2, 0)))
    conv_o[...] = _conv(inner, hb[...], cw, prev1, prev2).astype(BF16)
    carry[...] = inner[tm - 8:, :]
    cst_o[...] = inner[tm - 2:, :]

    cs, sa, sb = cs_ref[...], sa_ref[...], sb_ref[...]
    for c in range(hq.shape[1] // LANES):
        sl = slice(c * LANES, (c + 1) * LANES)
        q_o[:, sl] = _rope_chunk(hq[:, sl], cs, sa, sb).astype(BF16)
    kvd = hkv.shape[1] // 2
    for c in range(kvd // LANES):
        sl = slice(c * LANES, (c + 1) * LANES)
        kr = _rope_chunk(hkv[:, sl], cs, sa, sb)
        k_o[:, sl] = kr.astype(BF16)
        kw_o[:, sl] = kr[tm - WINDOW:, :]
    v = hkv[:, kvd:]
    v_o[...] = v.astype(BF16)
    vw_o[...] = v[tm - WINDOW:, :]


def _prep_prompt(h, conv_w, l, tabs, batch, seq, cdim, adim, kvd, tm):
    m = h.shape[0]
    tps = seq // tm
    qblk = 3 * cdim // adim
    kvblk = (3 * cdim + adim) // (2 * kvd)
    row = lambda i: (i, 0)
    return pl.pallas_call(
        functools.partial(_prep_prompt_kernel, tiles_per_seq=tps),
        grid=(m // tm,),
        in_specs=[pl.BlockSpec((tm, cdim), lambda i: (i, 0)),
                  pl.BlockSpec((tm, cdim), lambda i: (i, 1)),
                  pl.BlockSpec((tm, cdim), lambda i: (i, 2)),
                  pl.BlockSpec((tm, adim), lambda i: (i, qblk)),
                  pl.BlockSpec((tm, 2 * kvd), lambda i: (i, kvblk)),
                  pl.BlockSpec((None, 3, cdim), lambda i: (l, 0, 0)),
                  pl.BlockSpec((tm, LANES), lambda i: (i % tps, 0)),
                  pl.BlockSpec((tm, LANES), lambda i: (i % tps, 0)),
                  pl.BlockSpec((tm, LANES), lambda i: (i % tps, 0))],
        out_specs=[pl.BlockSpec((tm, cdim), row),
                   pl.BlockSpec((tm, adim), row),
                   pl.BlockSpec((tm, kvd), row),
                   pl.BlockSpec((tm, kvd), row),
                   pl.BlockSpec((None, 2, cdim), lambda i: (i // tps, 0, 0)),
                   pl.BlockSpec((None, WINDOW, kvd), lambda i: (i // tps, 0, 0)),
                   pl.BlockSpec((None, WINDOW, kvd), lambda i: (i // tps, 0, 0))],
        out_shape=[jax.ShapeDtypeStruct((m, cdim), BF16),
                   jax.ShapeDtypeStruct((m, adim), BF16),
                   jax.ShapeDtypeStruct((m, kvd), BF16),
                   jax.ShapeDtypeStruct((m, kvd), BF16),
                   jax.ShapeDtypeStruct((batch, 2, cdim), F32),
                   jax.ShapeDtypeStruct((batch, WINDOW, kvd), F32),
                   jax.ShapeDtypeStruct((batch, WINDOW, kvd), F32)],
        scratch_shapes=[pltpu.VMEM((8, cdim), F32)],
        compiler_params=_params("arbitrary"),
        name="prep_prompt",
    )(h, h, h, h, h, conv_w, *tabs)


def _prep_sample_kernel(hc, hb, hu, hq, hkv, cw, cs_ref, sa_ref, sb_ref, st1, st2,
                        conv_o, q_o, k_o, v_o, inner_o, *, dec_seq):
    tm, cdim = hc.shape
    inner = hc[...] * hu[...]
    t = lax.broadcasted_iota(jnp.int32, (tm, cdim), 0) % dec_seq
    prev1 = jnp.where(t == 0, st1[...], pltpu.roll(inner, 1, 0))
    prev2 = jnp.where(t < 2, st2[...], pltpu.roll(inner, 2, 0))
    conv_o[...] = _conv(inner, hb[...], cw, prev1, prev2).astype(BF16)
    inner_o[...] = inner
    cs, sa, sb = cs_ref[...], sa_ref[...], sb_ref[...]
    for c in range(hq.shape[1] // LANES):
        sl = slice(c * LANES, (c + 1) * LANES)
        q_o[:, sl] = _rope_chunk(hq[:, sl], cs, sa, sb)
    kvd = hkv.shape[1] // 2
    for c in range(kvd // LANES):
        sl = slice(c * LANES, (c + 1) * LANES)
        k_o[:, sl] = _rope_chunk(hkv[:, sl], cs, sa, sb)
    v_o[...] = hkv[:, kvd:]


def _prep_sample(h, conv_w, l, tabs, st1, st2, dec_seq, cdim, adim, kvd):
    m = h.shape[0]
    qblk = 3 * cdim // adim
    kvblk = (3 * cdim + adim) // (2 * kvd)
    z = lambda i: (0, 0)
    return pl.pallas_call(
        functools.partial(_prep_sample_kernel, dec_seq=dec_seq),
        grid=(1,),
        in_specs=[pl.BlockSpec((m, cdim), lambda i: (0, 0)),
                  pl.BlockSpec((m, cdim), lambda i: (0, 1)),
                  pl.BlockSpec((m, cdim), lambda i: (0, 2)),
                  pl.BlockSpec((m, adim), lambda i: (0, qblk)),
                  pl.BlockSpec((m, 2 * kvd), lambda i: (0, kvblk)),
                  pl.BlockSpec((None, 3, cdim), lambda i: (l, 0, 0)),
                  pl.BlockSpec((m, LANES), z), pl.BlockSpec((m, LANES), z), pl.BlockSpec((m, LANES), z),
                  pl.BlockSpec((m, cdim), z), pl.BlockSpec((m, cdim), z)],
        out_specs=[pl.BlockSpec((m, cdim), z), pl.BlockSpec((m, adim), z),
                   pl.BlockSpec((m, kvd), z), pl.BlockSpec((m, kvd), z), pl.BlockSpec((m, cdim), z)],
        out_shape=[jax.ShapeDtypeStruct((m, cdim), BF16),
                   jax.ShapeDtypeStruct((m, adim), F32),
                   jax.ShapeDtypeStruct((m, kvd), F32),
                   jax.ShapeDtypeStruct((m, kvd), F32),
                   jax.ShapeDtypeStruct((m, cdim), F32)],
        compiler_params=_params("arbitrary"),
        name="prep_sample",
    )(h, h, h, h, h, conv_w, *tabs, st1, st2)


def _sink_softmax(s, mask, sink_col):
    s = jnp.where(mask, s, -jnp.inf)
    m = jnp.maximum(jnp.max(s, axis=-1, keepdims=True), sink_col)
    p = jnp.exp(s - m)
    den = jnp.sum(p, axis=-1, keepdims=True) + jnp.exp(sink_col - m)
    return p * (1.0 / den)


def _attn_prompt_kernel(sink_ref, q_ref, kp_ref, kc_ref, vp_ref, vc_ref, o_ref, *, l, scale):
    n = pl.program_id(1)
    w = q_ref.shape[0]
    chunks = q_ref.shape[1] // LANES
    pairs = kc_ref.shape[1] // LANES
    cpp = chunks // pairs
    rows = 2 * cpp * w
    lo = lax.broadcasted_iota(jnp.int32, (w, LANES), 1) < HEAD_DIM
    qi = lax.broadcasted_iota(jnp.int32, (rows, 2 * w), 0) % w
    sj = lax.broadcasted_iota(jnp.int32, (rows, 2 * w), 1)
    mask = (sj <= qi + w) & (sj > qi) & ((n > 0) | (sj >= w))
    zero = jnp.zeros((w, LANES), BF16)
    for p in range(pairs):
        ksl = slice(p * LANES, (p + 1) * LANES)
        kk = jnp.concatenate([kp_ref[:, ksl], kc_ref[:, ksl]], axis=0)
        vv = jnp.concatenate([vp_ref[:, ksl], vc_ref[:, ksl]], axis=0)
        pieces, sinks = [], []
        for j in range(cpp):
            c = p * cpp + j
            qc = q_ref[:, c * LANES:(c + 1) * LANES]
            pieces += [jnp.where(lo, qc, zero), jnp.where(lo, zero, qc)]
            sinks += [jnp.full((w, 1), sink_ref[l, 2 * c], F32), jnp.full((w, 1), sink_ref[l, 2 * c + 1], F32)]
        qs = jnp.concatenate(pieces, axis=0)
        s = lax.dot_general(qs, kk, (((1,), (1,)), ((), ())), preferred_element_type=F32) * scale
        pr = _sink_softmax(s, mask, jnp.concatenate(sinks, axis=0))
        o = jnp.dot(pr.astype(BF16), vv, preferred_element_type=F32)
        for j in range(cpp):
            c = p * cpp + j
            o_ref[:, c * LANES:(c + 1) * LANES] = jnp.where(
                lo, o[2 * j * w:(2 * j + 1) * w], o[(2 * j + 1) * w:(2 * j + 2) * w]).astype(BF16)


def _attn_prompt(sinks, q, k, v, l, batch, seq):
    m, adim = q.shape
    kvd = k.shape[1]
    nb = seq // WINDOW
    cur = lambda b, n: (b * nb + n, 0)
    prev = lambda b, n: (b * nb + jnp.maximum(n - 1, 0), 0)
    return pl.pallas_call(
        functools.partial(_attn_prompt_kernel, l=l, scale=HEAD_DIM ** -0.5),
        grid=(batch, nb),
        in_specs=[pl.BlockSpec(memory_space=pltpu.SMEM),
                  pl.BlockSpec((WINDOW, adim), cur),
                  pl.BlockSpec((WINDOW, kvd), prev), pl.BlockSpec((WINDOW, kvd), cur),
                  pl.BlockSpec((WINDOW, kvd), prev), pl.BlockSpec((WINDOW, kvd), cur)],
        out_specs=pl.BlockSpec((WINDOW, adim), cur),
        out_shape=jax.ShapeDtypeStruct((m, adim), BF16),
        compiler_params=_params("arbitrary", "arbitrary"),
        name="attn_prompt",
    )(sinks, q, k, k, v, v)


def _attn_sample_kernel(sink_ref, q_ref, kn_ref, vn_ref, ck_ref, cv_ref,
                        o_ref, kw_ref, vw_ref, kk_ref, vv_ref, qs_ref, *, l, scale):
    t = q_ref.shape[0]
    w = ck_ref.shape[0]
    tot = kk_ref.shape[0]
    chunks = q_ref.shape[1] // LANES
    pairs = ck_ref.shape[1] // LANES
    cpp = chunks // pairs
    rows = 2 * cpp * t
    kk_ref[0:w, :] = ck_ref[...]
    vv_ref[0:w, :] = cv_ref[...]
    pad = jnp.zeros((tot - w - t, kk_ref.shape[1]), F32)
    kk_ref[w:w + t, :] = kn_ref[...]
    vv_ref[w:w + t, :] = vn_ref[...]
    kk_ref[w + t:, :] = pad
    vv_ref[w + t:, :] = pad
    kw_ref[...] = pltpu.roll(kk_ref[...], tot - t, 0)[0:w, :]
    vw_ref[...] = pltpu.roll(vv_ref[...], tot - t, 0)[0:w, :]

    lo = lax.broadcasted_iota(jnp.int32, (t, LANES), 1) < HEAD_DIM
    qi = lax.broadcasted_iota(jnp.int32, (rows, tot), 0) % t
    sj = lax.broadcasted_iota(jnp.int32, (rows, tot), 1)
    mask = (sj <= qi + w) & (sj > qi)
    rowid = lax.broadcasted_iota(jnp.int32, (rows, 1), 0) // t
    for p in range(pairs):
        ksl = slice(p * LANES, (p + 1) * LANES)
        sink_col = jnp.zeros((rows, 1), F32)
        for j in range(cpp):
            c = p * cpp + j
            qc = q_ref[:, c * LANES:(c + 1) * LANES]
            qs_ref[(2 * j) * t:(2 * j + 1) * t, :] = jnp.where(lo, qc, 0.0)
            qs_ref[(2 * j + 1) * t:(2 * j + 2) * t, :] = jnp.where(lo, 0.0, qc)
            sink_col = jnp.where(rowid == 2 * j, sink_ref[l, 2 * c], sink_col)
            sink_col = jnp.where(rowid == 2 * j + 1, sink_ref[l, 2 * c + 1], sink_col)
        s = lax.dot_general(qs_ref[...].astype(BF16), kk_ref[:, ksl].astype(BF16),
                            (((1,), (1,)), ((), ())), preferred_element_type=F32) * scale
        pr = _sink_softmax(s, mask, sink_col)
        qs_ref[...] = jnp.dot(pr.astype(BF16), vv_ref[:, ksl].astype(BF16), preferred_element_type=F32)
        for j in range(cpp):
            c = p * cpp + j
            o_ref[:, c * LANES:(c + 1) * LANES] = jnp.where(
                lo, qs_ref[(2 * j) * t:(2 * j + 1) * t, :], qs_ref[(2 * j + 1) * t:(2 * j + 2) * t, :]).astype(BF16)


def _attn_sample(sinks, q, kn, vn, cache_k, cache_v, l):
    nseq, t, adim = q.shape
    w, kvd = cache_k.shape[2], cache_k.shape[3]
    tot = -(-(w + t) // 8) * 8
    rows = 2 * (adim // kvd) * t
    seq3 = lambda b: (b, 0, 0)
    cache = lambda b: (l, b, 0, 0)
    return pl.pallas_call(
        functools.partial(_attn_sample_kernel, l=l, scale=HEAD_DIM ** -0.5),
        grid=(nseq,),
        in_specs=[pl.BlockSpec(memory_space=pltpu.SMEM),
                  pl.BlockSpec((None, t, adim), seq3),
                  pl.BlockSpec((None, t, kvd), seq3), pl.BlockSpec((None, t, kvd), seq3),
                  pl.BlockSpec((None, None, w, kvd), cache), pl.BlockSpec((None, None, w, kvd), cache)],
        out_specs=[pl.BlockSpec((None, t, adim), seq3),
                   pl.BlockSpec((None, w, kvd), seq3), pl.BlockSpec((None, w, kvd), seq3)],
        out_shape=[jax.ShapeDtypeStruct((nseq, t, adim), BF16),
                   jax.ShapeDtypeStruct((nseq, w, kvd), F32),
                   jax.ShapeDtypeStruct((nseq, w, kvd), F32)],
        scratch_shapes=[pltpu.VMEM((tot, kvd), F32), pltpu.VMEM((tot, kvd), F32),
                        pltpu.VMEM((rows, LANES), F32)],
        compiler_params=_params("arbitrary"),
        name="attn_sample",
    )(sinks, q, kn, vn, cache_k, cache_v)


def _layer_norm(z, g, b):
    mu = jnp.mean(z, axis=-1, keepdims=True)
    d = z - mu
    var = jnp.mean(d * d, axis=-1, keepdims=True)
    return d * lax.rsqrt(var + LN_EPS) * g + b


def _top2(sg, lane, n):
    m1 = jnp.max(sg, axis=-1, keepdims=True)
    i1 = jnp.min(jnp.where(sg == m1, lane, n), axis=-1, keepdims=True)
    rest = jnp.where(lane == i1, -jnp.inf, sg)
    m2 = jnp.max(rest, axis=-1, keepdims=True)
    i2 = jnp.min(jnp.where(rest == m2, lane, n), axis=-1, keepdims=True)
    return m1, i1, m2, i2


def _mix_out_kernel(conv_ref, attn_ref, x_ref, wo_ref, g_ref, b_ref, wr_ref, br_ref, cnt_in,
                    x1_ref, route_ref, cnt_ref, carry, *, l, alpha, has_alias):
    tm = x_ref.shape[0]
    cdim = conv_ref.shape[1]
    ne = wr_ref.shape[1]
    epg = ne // N_EXPERT_GROUPS

    @pl.when(pl.program_id(0) == 0)
    def _():
        carry[...] = cnt_in[...]

    y = jnp.dot(conv_ref[...], wo_ref[0:cdim, :], preferred_element_type=F32)
    y = y + jnp.dot(attn_ref[...], wo_ref[cdim:, :], preferred_element_type=F32)
    x1 = _layer_norm(alpha * x_ref[...] + y, g_ref[...], b_ref[...])
    x1_ref[...] = x1

    logits = jnp.dot(x1.astype(BF16), wr_ref[...], preferred_element_type=F32)
    scores = jax.nn.sigmoid(logits)
    sel = scores + br_ref[...]
    lane = lax.broadcasted_iota(jnp.int32, (tm, ne), 1).astype(F32)
    grp = jnp.floor(lane * (1.0 / epg))
    best_s = None
    best_g = None
    for g in range(N_EXPERT_GROUPS):
        m1, _, m2, _ = _top2(jnp.where(grp == g, sel, -jnp.inf), lane, float(ne))
        gs = m1 + m2
        if g == 0:
            best_s, best_g = gs, jnp.zeros_like(gs)
        else:
            upd = gs > best_s
            best_g = jnp.where(upd, float(g), best_g)
            best_s = jnp.where(upd, gs, best_s)
    _, e1, _, e2 = _top2(jnp.where(grp == best_g, sel, -jnp.inf), lane, float(ne))
    hit1 = lane == e1
    hit2 = lane == e2
    w1 = jnp.sum(jnp.where(hit1, scores, 0.0), axis=-1, keepdims=True)
    w2 = jnp.sum(jnp.where(hit2, scores, 0.0), axis=-1, keepdims=True)
    wsum = w1 + w2

    onehot = jnp.where(hit1 | hit2, 1.0, 0.0)
    tri = (lax.broadcasted_iota(jnp.int32, (tm, tm), 0) > lax.broadcasted_iota(jnp.int32, (tm, tm), 1))
    before = jnp.dot(jnp.where(tri, 1.0, 0.0).astype(BF16), onehot.astype(BF16), preferred_element_type=F32)
    before = before + carry[0:1, 0:ne]
    r1 = jnp.sum(jnp.where(hit1, before, 0.0), axis=-1, keepdims=True)
    r2 = jnp.sum(jnp.where(hit2, before, 0.0), axis=-1, keepdims=True)
    carry[0:1, 0:ne] = carry[0:1, 0:ne] + jnp.sum(onehot, axis=0, keepdims=True)
    cnt_ref[...] = carry[...]

    col = lax.broadcasted_iota(jnp.int32, (tm, LANES), 1)
    out = jnp.zeros((tm, LANES), F32)
    for c, val in enumerate((e1, e2, w1 / wsum, w2 / wsum, r1, r2)):
        out = jnp.where(col == c, val, out)
    route_ref[...] = out


def _mix_out(conv, attn, x, wo, ln_g, ln_b, wr, br, cnt_in, l, alpha, tm, total_rows, row_block0, prev=None):
    m, d = x.shape
    cdim = conv.shape[1]
    ne = wr.shape[1]
    rowi = lambda i: (i, 0)
    rowo = lambda i: (i + row_block0, 0)
    const = lambda i: (0, 0)
    lay = lambda i: (l, 0, 0)
    in_specs = [pl.BlockSpec((tm, cdim), rowi), pl.BlockSpec((tm, d - cdim), rowi), pl.BlockSpec((tm, d), rowi),
                pl.BlockSpec((d, d), const, pipeline_mode=pl.Buffered(1)),
                pl.BlockSpec((None, 1, d), lay), pl.BlockSpec((None, 1, d), lay),
                pl.BlockSpec((d, ne), const), pl.BlockSpec((1, ne), const),
                pl.BlockSpec((8, LANES), const)]
    args = [conv, attn, x, wo, ln_g, ln_b, wr, br, cnt_in]
    aliases = {}
    if prev is not None:
        in_specs += [pl.BlockSpec(memory_space=pl.ANY), pl.BlockSpec(memory_space=pl.ANY)]
        args += list(prev)
        aliases = {len(args) - 2: 0, len(args) - 1: 1}

    def body(*refs):
        ins = refs[:9]
        outs = refs[-4:]
        _mix_out_kernel(*ins, *outs, l=l, alpha=alpha, has_alias=prev is not None)

    return pl.pallas_call(
        body,
        grid=(m // tm,),
        in_specs=in_specs,
        out_specs=[pl.BlockSpec((tm, d), rowo), pl.BlockSpec((tm, LANES), rowo), pl.BlockSpec((8, LANES), const)],
        out_shape=[jax.ShapeDtypeStruct((total_rows, d), F32),
                   jax.ShapeDtypeStruct((total_rows, LANES), F32),
                   jax.ShapeDtypeStruct((8, LANES), F32)],
        scratch_shapes=[pltpu.VMEM((8, LANES), F32)],
        input_output_aliases=aliases,
        compiler_params=_params("arbitrary"),
        name="mix_out_sample" if prev is not None else "mix_out_prompt",
    )(*args)


def _scatter_rows_kernel(slot_ref, x_ref, o_ref, sem, *, n_tok):
    ct = x_ref.shape[0]
    c = pl.program_id(0)
    last = pl.num_programs(0) - 1
    tail = n_tok - (pl.cdiv(n_tok, ct) - 1) * ct

    def run(count):
        def issue(jb, carry):
            j0 = pl.multiple_of(jb * ISSUE_UNROLL, ISSUE_UNROLL)
            for u in range(ISSUE_UNROLL):
                for k in range(TOP_K):
                    pltpu.make_async_copy(x_ref.at[pl.ds(j0 + u, 1)],
                                          o_ref.at[pl.ds(slot_ref[TOP_K * (j0 + u) + k], 1)], sem).start()
            return carry

        lax.fori_loop(0, count // ISSUE_UNROLL, issue, 0)
        for _ in range(TOP_K):
            pltpu.make_async_copy(x_ref.at[pl.ds(0, count)], x_ref.at[pl.ds(0, count)], sem).wait()

    if tail == ct:
        run(ct)
    else:
        pl.when(c < last)(lambda: run(ct))
        pl.when(c == last)(lambda: run(tail))


def _scatter_rows(x1, slot_flat, n_slots):
    total, d = x1.shape
    ct = SLOT_BLOCK // TOP_K
    return pl.pallas_call(
        functools.partial(_scatter_rows_kernel, n_tok=total),
        grid=(pl.cdiv(total, ct),),
        in_specs=[pl.BlockSpec((SLOT_BLOCK,), lambda c: (c,), memory_space=pltpu.SMEM),
                  pl.BlockSpec((ct, d), lambda c: (c, 0))],
        out_specs=pl.BlockSpec(memory_space=pl.ANY),
        out_shape=jax.ShapeDtypeStruct((n_slots, d), x1.dtype),
        scratch_shapes=[pltpu.SemaphoreType.DMA(())],
        compiler_params=_params("arbitrary"),
        name="scatter_rows",
    )(slot_flat, x1)


def _expert_kernel(te_ref, nv_ref, nu_ref, x_ref, wg_ref, wu_ref, wd_ref, o_ref, xb_ref):
    i = pl.program_id(0)
    f = pl.program_id(1)

    @pl.when(i < nu_ref[0])
    def _():
        @pl.when(f == 0)
        def _():
            rid = lax.broadcasted_iota(jnp.int32, (x_ref.shape[0], 1), 0)
            xb_ref[...] = jnp.where(rid < nv_ref[i], x_ref[...], 0.0).astype(BF16)

        x = xb_ref[...]
        g = jnp.dot(x, wg_ref[...].astype(BF16), preferred_element_type=F32)
        u = jnp.dot(x, wu_ref[...].astype(BF16), preferred_element_type=F32)
        h = (jax.nn.silu(g) * u).astype(BF16)
        y = jnp.dot(h, wd_ref[...].astype(BF16), preferred_element_type=F32)

        @pl.when(f == 0)
        def _():
            o_ref[...] = y

        @pl.when(f > 0)
        def _():
            o_ref[...] += y


def _expert_ffn(xs, w_gate, w_up, w_down, tile_e, tile_nv, n_used, l):
    rows, d = xs.shape
    dff = w_gate.shape[3]
    nf = dff // FF_CHUNK
    nt = rows // EXPERT_TILE

    def item(i, nu):
        return jnp.minimum(i, jnp.maximum(nu[0] - 1, 0))

    def ffc(i, f, nu):
        return jnp.where(i < nu[0], f, nf - 1)

    xmap = lambda i, f, te, nv, nu: (item(i, nu), 0)
    gmap = lambda i, f, te, nv, nu: (l, te[item(i, nu)], 0, ffc(i, f, nu))
    dmap = lambda i, f, te, nv, nu: (l, te[item(i, nu)], ffc(i, f, nu), 0)
    return pl.pallas_call(
        _expert_kernel,
        grid_spec=pltpu.PrefetchScalarGridSpec(
            num_scalar_prefetch=3,
            grid=(nt, nf),
            in_specs=[pl.BlockSpec((EXPERT_TILE, d), xmap),
                      pl.BlockSpec((None, None, d, FF_CHUNK), gmap),
                      pl.BlockSpec((None, None, d, FF_CHUNK), gmap),
                      pl.BlockSpec((None, None, FF_CHUNK, d), dmap)],
            out_specs=pl.BlockSpec((EXPERT_TILE, d), xmap),
            scratch_shapes=[pltpu.VMEM((EXPERT_TILE, d), BF16)]),
        out_shape=jax.ShapeDtypeStruct((rows, d), F32),
        compiler_params=_params("arbitrary", "arbitrary"),
        name="expert_ffn",
    )(tile_e, tile_nv, n_used, xs, w_gate, w_up, w_down)


def _combine_kernel(slot_cur, slot_nxt, x1_ref, r_ref, g_ref, b_ref, y_hbm, o_ref, ybuf, sems, *, alpha):
    i = pl.program_id(0)
    ct = x1_ref.shape[0]

    def fetch(slot_ref, buf):
        def issue(jb, carry):
            j0 = pl.multiple_of(jb * ISSUE_UNROLL, ISSUE_UNROLL)
            for u in range(ISSUE_UNROLL):
                for k in range(TOP_K):
                    pltpu.make_async_copy(y_hbm.at[pl.ds(slot_ref[TOP_K * (j0 + u) + k], 1)],
                                          ybuf.at[buf, k, pl.ds(j0 + u, 1)], sems.at[buf]).start()
            return carry

        lax.fori_loop(0, ct // ISSUE_UNROLL, issue, 0)

    @pl.when(i == 0)
    def _():
        fetch(slot_cur, 0)

    @pl.when(i + 1 < pl.num_programs(0))
    def _():
        fetch(slot_nxt, (i + 1) % 2)

    buf = i % 2
    for k in range(TOP_K):
        pltpu.make_async_copy(ybuf.at[buf, k], ybuf.at[buf, k], sems.at[buf]).wait()
    r = r_ref[...]
    moe = ybuf[buf, 0] * r[:, 2:3] + ybuf[buf, 1] * r[:, 3:4]
    o_ref[...] = _layer_norm(alpha * x1_ref[...] + moe, g_ref[...], b_ref[...])


def _combine(x1, yb, slot_flat, route, ln_g, ln_b, l, alpha, tm, row0, m):
    d = x1.shape[1]
    nsteps = m // tm
    rb0 = row0 // tm
    sb0 = row0 * TOP_K // SLOT_BLOCK
    sper = pl.cdiv(tm * TOP_K, SLOT_BLOCK)
    rowo = lambda i: (i + rb0, 0)
    lay = lambda i: (l, 0, 0)
    return pl.pallas_call(
        functools.partial(_combine_kernel, alpha=alpha),
        grid=(nsteps,),
        in_specs=[pl.BlockSpec((SLOT_BLOCK,), lambda i: (sb0 + i * sper,), memory_space=pltpu.SMEM),
                  pl.BlockSpec((SLOT_BLOCK,), lambda i: (sb0 + jnp.minimum(i + 1, nsteps - 1) * sper,),
                               memory_space=pltpu.SMEM),
                  pl.BlockSpec((tm, d), rowo),
                  pl.BlockSpec((tm, LANES), rowo),
                  pl.BlockSpec((None, 1, d), lay), pl.BlockSpec((None, 1, d), lay),
                  pl.BlockSpec(memory_space=pl.ANY)],
        out_specs=pl.BlockSpec((tm, d), lambda i: (i, 0)),
        out_shape=jax.ShapeDtypeStruct((m, d), F32),
        scratch_shapes=[pltpu.VMEM((2, TOP_K, tm, d), F32), pltpu.SemaphoreType.DMA((2,))],
        compiler_params=_params("arbitrary"),
        name="combine",
    )(slot_flat, slot_flat, x1, route, ln_g, ln_b, yb)


def _rope_tables(pos):
    half = ROT_DIM // 2
    inv = ROPE_THETA ** (-jnp.arange(half, dtype=F32) * 2.0 / ROT_DIM)
    ang = pos.astype(F32)[:, None] * inv[None, :]
    cos, sin = jnp.cos(ang), jnp.sin(ang)
    n = pos.shape[0]
    rest = HEAD_DIM - ROT_DIM
    cs = jnp.concatenate([cos, cos, jnp.ones((n, rest), F32)], axis=1)
    sa = jnp.concatenate([-sin, jnp.zeros((n, half + rest), F32)], axis=1)
    sb = jnp.concatenate([jnp.zeros((n, half), F32), sin, jnp.zeros((n, rest), F32)], axis=1)
    rep = LANES // HEAD_DIM
    return tuple(jnp.tile(t, (1, rep)) for t in (cs, sa, sb))


def _pair_heads(t, axis, n_heads):
    group = n_heads // N_KV_HEADS
    inner = t.shape[axis] // n_heads
    shp = t.shape[:axis] + (N_KV_HEADS // 2, 2, group, inner) + t.shape[axis + 1:]
    t = jnp.swapaxes(t.reshape(shp), axis + 1, axis + 2)
    return t.reshape(t.shape[:axis] + (n_heads * inner,) + t.shape[axis + 4:])


def kernel(x_prompt, x_sample, state_conv, cache_k, cache_v, w_in, conv_w, attn_sinks, w_o,
           ln1_g, ln1_b, w_router, b_router, w_gate, w_up, w_down, ln2_g, ln2_b):
    batch, seq, d = x_prompt.shape
    nseq, dec_seq, _ = x_sample.shape
    depth = w_in.shape[0]
    cdim = conv_w.shape[2]
    adim = d - cdim
    kvd = N_KV_HEADS * HEAD_DIM
    n_heads = adim // HEAD_DIM
    ne = w_router.shape[1]
    mp, ms = batch * seq, nseq * dec_seq
    total = mp + ms
    assign = total * TOP_K
    alpha = (2 * depth) ** 0.25

    q0, q1 = 3 * cdim, 3 * cdim + adim
    sinks = _pair_heads(attn_sinks, 1, n_heads)
    wr_b = w_router.astype(BF16)
    br = b_router.reshape(1, ne)

    tabs_p = _rope_tables(jnp.arange(seq))
    tabs_s = _rope_tables(jnp.tile(PAST_LEN + jnp.arange(dec_seq), nseq))
    cache_k2 = cache_k.reshape(depth, nseq, WINDOW, kvd)
    cache_v2 = cache_v.reshape(depth, nseq, WINDOW, kvd)
    ln1g, ln1b = ln1_g.reshape(depth, 1, d), ln1_b.reshape(depth, 1, d)
    ln2g, ln2b = ln2_g.reshape(depth, 1, d), ln2_b.reshape(depth, 1, d)

    n_items = pl.cdiv(assign, EXPERT_TILE) + ne
    slot_len = pl.cdiv(assign, SLOT_BLOCK) * SLOT_BLOCK
    items = jnp.arange(n_items, dtype=jnp.int32)

    xp = x_prompt.reshape(mp, d)
    xs_ = x_sample.reshape(ms, d)
    outs = [[] for _ in range(6)]
    for l in range(depth):
        w_in_b = jnp.concatenate([w_in[l, :, :q0].astype(BF16),
                                  _pair_heads(w_in[l, :, q0:q1].astype(BF16), 1, n_heads),
                                  w_in[l, :, q1:].astype(BF16)], axis=1)
        w_o_b = jnp.concatenate([w_o[l, :cdim].astype(BF16),
                                 _pair_heads(w_o[l, cdim:].astype(BF16), 0, n_heads)], axis=0)

        hp = _matmul(xp, w_in_b, 512, 1536, "proj_prompt")
        conv_p, q_p, k_p, v_p, cst_p, kw_p, vw_p = _prep_prompt(
            hp, conv_w, l, tabs_p, batch, seq, cdim, adim, kvd, 256)
        attn_p = _attn_prompt(sinks, q_p, k_p, v_p, l, batch, seq)

        hs = _matmul(xs_, w_in_b, ms, 1536, "proj_sample")
        st = state_conv[l]
        st1 = jnp.repeat(st[:, 1], dec_seq, axis=0)
        st2 = jnp.stack([st[:, 0], st[:, 1]] + [st[:, 1]] * (dec_seq - 2), axis=1).reshape(ms, cdim)
        conv_s, q_s, kn_s, vn_s, inner_s = _prep_sample(hs, conv_w, l, tabs_s, st1, st2, dec_seq, cdim, adim, kvd)
        attn_s, kw_s, vw_s = _attn_sample(
            sinks, q_s.reshape(nseq, dec_seq, adim), kn_s.reshape(nseq, dec_seq, kvd),
            vn_s.reshape(nseq, dec_seq, kvd), cache_k2, cache_v2, l)

        zeros_cnt = jnp.zeros((8, LANES), F32)
        x1, route, cnt = _mix_out(conv_p, attn_p, xp, w_o_b, ln1g, ln1b, wr_b, br, zeros_cnt,
                                  l, alpha, 256, total, 0)
        x1, route, cnt = _mix_out(conv_s, attn_s.reshape(ms, adim), xs_, w_o_b, ln1g, ln1b, wr_b, br, cnt,
                                  l, alpha, ms, total, mp // ms, prev=(x1, route))

        e = route[:, 0:TOP_K].astype(jnp.int32)
        rank = route[:, 4:4 + TOP_K].astype(jnp.int32)
        counts = cnt[0, :ne].astype(jnp.int32)
        tiles_e = (counts + EXPERT_TILE - 1) // EXPERT_TILE
        tile_end = jnp.cumsum(tiles_e)
        tile_start = tile_end - tiles_e
        onehot = e[:, :, None] == jnp.arange(ne)[None, None, :]
        pos = jnp.sum(jnp.where(onehot, (tile_start * EXPERT_TILE)[None, None, :], 0), axis=-1) + rank
        slot_flat = jnp.pad(pos.reshape(-1), (0, slot_len - assign))
        n_used = tile_end[-1:].astype(jnp.int32)
        item_e = jnp.minimum(jnp.sum(tile_end[None, :] <= items[:, None], axis=1), ne - 1).astype(jnp.int32)
        item_nv = jnp.clip(counts[item_e] - (items - tile_start[item_e]) * EXPERT_TILE, 0, EXPERT_TILE)

        xsorted = _scatter_rows(x1, slot_flat, n_items * EXPERT_TILE)
        yb = _expert_ffn(xsorted, w_gate, w_up, w_down, item_e, item_nv.astype(jnp.int32), n_used, l)

        xp = _combine(x1, yb, slot_flat, route, ln2g, ln2b, l, alpha, 512, 0, mp)
        xs_ = _combine(x1, yb, slot_flat, route, ln2g, ln2b, l, alpha, ms, mp, ms)

        inner3 = inner_s.reshape(nseq, dec_seq, cdim)
        for lst, val in zip(outs, (cst_p, kw_p.reshape(batch, WINDOW, N_KV_HEADS, HEAD_DIM),
                                   vw_p.reshape(batch, WINDOW, N_KV_HEADS, HEAD_DIM),
                                   inner3[:, dec_seq - 2:],
                                   kw_s.reshape(nseq, WINDOW, N_KV_HEADS, HEAD_DIM),
                                   vw_s.reshape(nseq, WINDOW, N_KV_HEADS, HEAD_DIM))):
            lst.append(val)

    return (xp.reshape(batch, seq, d), xs_.reshape(nseq, dec_seq, d), *[jnp.stack(o) for o in outs])
```

```python
import functools
import math

import jax
import jax.numpy as jnp
from jax import lax
from jax.experimental import pallas as pl
from jax.experimental.pallas import tpu as pltpu

F32 = jnp.float32
BF16 = jnp.bfloat16

HEAD_DIM = 64
N_KV_HEADS = 4
ROT_DIM = HEAD_DIM // 4
ROPE_THETA = 500000.0
WINDOW = 128
PAST_LEN = 16384
N_EXPERT_GROUPS = 4
TOP_K = 2
LN_EPS = 1e-5
QK_SCALE = HEAD_DIM ** -0.5
assert math.frexp(QK_SCALE)[0] == 0.5

LANES = 128
VMEM_LIMIT = 56 * 1024 * 1024

EXPERT_TILE = 640
FF_CHUNK = 512
OUT_CHUNK = 1024
SLOT_BLOCK = 1024
ISSUE_UNROLL = 8
MIX_SUB_ROWS = 256


def _params(*sem):
    return pltpu.CompilerParams(dimension_semantics=sem, vmem_limit_bytes=VMEM_LIMIT)


def _matmul_kernel(x_ref, w_ref, o_ref):
    o_ref[...] = jnp.dot(x_ref[...].astype(BF16), w_ref[...], preferred_element_type=F32)


def _matmul(x, w, tm, tn, name):
    m, k = x.shape
    n = w.shape[1]
    return pl.pallas_call(
        _matmul_kernel,
        grid=(n // tn, m // tm),
        in_specs=[pl.BlockSpec((tm, k), lambda j, i: (i, 0)),
                  pl.BlockSpec((k, tn), lambda j, i: (0, j))],
        out_specs=pl.BlockSpec((tm, tn), lambda j, i: (i, j)),
        out_shape=jax.ShapeDtypeStruct((m, n), F32),
        compiler_params=_params("arbitrary", "arbitrary"),
        name=name,
    )(x, w)


def _rope_chunk(xc, cs, sa, sb):
    return xc * cs + pltpu.roll(xc, LANES - ROT_DIM // 2, 1) * sa + pltpu.roll(xc, ROT_DIM // 2, 1) * sb


def _conv(inner, gate, w_ref, prev1, prev2):
    return gate * (w_ref[0:1, :] * prev2 + w_ref[1:2, :] * prev1 + w_ref[2:3, :] * inner)


def _prep_prompt_kernel(hc, hb, hu, hq, hkv, cw, cs_ref, sa_ref, sb_ref,
                        conv_o, q_o, k_o, v_o, cst_o, kw_o, vw_o, carry, *, tiles_per_seq):
    tm, cdim = hc.shape

    @pl.when(pl.program_id(0) % tiles_per_seq == 0)
    def _():
        carry[...] = jnp.zeros_like(carry)

    inner = hc[...] * hu[...]
    row = lax.broadcasted_iota(jnp.int32, (tm, cdim), 0)
    c1 = carry[7:8, :]
    c2 = carry[6:7, :]
    prev1 = jnp.where(row == 0, c1, pltpu.roll(inner, 1, 0))
    prev2 = jnp.where(row == 0, c2, jnp.where(row == 1, c1, pltpu.roll(inner, 2, 0)))
    conv_o[...] = _conv(inner, hb[...], cw, prev1, prev2).astype(BF16)
    carry[...] = inner[tm - 8:, :]
    cst_o[...] = inner[tm - 2:, :]

    cs, sa, sb = cs_ref[...], sa_ref[...], sb_ref[...]
    for c in range(hq.shape[1] // LANES):
        sl = slice(c * LANES, (c + 1) * LANES)
        q_o[:, sl] = (_rope_chunk(hq[:, sl], cs, sa, sb) * QK_SCALE).astype(BF16)
    kvd = hkv.shape[1] // 2
    for c in range(kvd // LANES):
        sl = slice(c * LANES, (c + 1) * LANES)
        kr = _rope_chunk(hkv[:, sl], cs, sa, sb)
        k_o[:, sl] = kr.astype(BF16)
        kw_o[:, sl] = kr[tm - WINDOW:, :]
    v = hkv[:, kvd:]
    v_o[...] = v.astype(BF16)
    vw_o[...] = v[tm - WINDOW:, :]


def _prep_prompt(h, conv_w, l, tabs, batch, seq, cdim, adim, kvd, tm):
    m = h.shape[0]
    tps = seq // tm
    qblk = 3 * cdim // adim
    kvblk = (3 * cdim + adim) // (2 * kvd)
    row = lambda i: (i, 0)
    return pl.pallas_call(
        functools.partial(_prep_prompt_kernel, tiles_per_seq=tps),
        grid=(m // tm,),
        in_specs=[pl.BlockSpec((tm, cdim), lambda i: (i, 0)),
                  pl.BlockSpec((tm, cdim), lambda i: (i, 1)),
                  pl.BlockSpec((tm, cdim), lambda i: (i, 2)),
                  pl.BlockSpec((tm, adim), lambda i: (i, qblk)),
                  pl.BlockSpec((tm, 2 * kvd), lambda i: (i, kvblk)),
                  pl.BlockSpec((None, 3, cdim), lambda i: (l, 0, 0)),
                  pl.BlockSpec((tm, LANES), lambda i: (i % tps, 0)),
                  pl.BlockSpec((tm, LANES), lambda i: (i % tps, 0)),
                  pl.BlockSpec((tm, LANES), lambda i: (i % tps, 0))],
        out_specs=[pl.BlockSpec((tm, cdim), row),
                   pl.BlockSpec((tm, adim), row),
                   pl.BlockSpec((tm, kvd), row),
                   pl.BlockSpec((tm, kvd), row),
                   pl.BlockSpec((None, 2, cdim), lambda i: (i // tps, 0, 0)),
                   pl.BlockSpec((None, WINDOW, kvd), lambda i: (i // tps, 0, 0)),
                   pl.BlockSpec((None, WINDOW, kvd), lambda i: (i // tps, 0, 0))],
        out_shape=[jax.ShapeDtypeStruct((m, cdim), BF16),
                   jax.ShapeDtypeStruct((m, adim), BF16),
                   jax.ShapeDtypeStruct((m, kvd), BF16),
                   jax.ShapeDtypeStruct((m, kvd), BF16),
                   jax.ShapeDtypeStruct((batch, 2, cdim), F32),
                   jax.ShapeDtypeStruct((batch, WINDOW, kvd), F32),
                   jax.ShapeDtypeStruct((batch, WINDOW, kvd), F32)],
        scratch_shapes=[pltpu.VMEM((8, cdim), F32)],
        compiler_params=_params("arbitrary"),
        name="prep_prompt",
    )(h, h, h, h, h, conv_w, *tabs)


def _prep_sample_kernel(hc, hb, hu, hq, hkv, cw, cs_ref, sa_ref, sb_ref, st1, st2,
                        conv_o, q_o, k_o, v_o, inner_o, *, dec_seq):
    tm, cdim = hc.shape
    inner = hc[...] * hu[...]
    t = lax.broadcasted_iota(jnp.int32, (tm, cdim), 0) % dec_seq
    prev1 = jnp.where(t == 0, st1[...], pltpu.roll(inner, 1, 0))
    prev2 = jnp.where(t < 2, st2[...], pltpu.roll(inner, 2, 0))
    conv_o[...] = _conv(inner, hb[...], cw, prev1, prev2).astype(BF16)
    inner_o[...] = inner
    cs, sa, sb = cs_ref[...], sa_ref[...], sb_ref[...]
    for c in range(hq.shape[1] // LANES):
        sl = slice(c * LANES, (c + 1) * LANES)
        q_o[:, sl] = _rope_chunk(hq[:, sl], cs, sa, sb)
    kvd = hkv.shape[1] // 2
    for c in range(kvd // LANES):
        sl = slice(c * LANES, (c + 1) * LANES)
        k_o[:, sl] = _rope_chunk(hkv[:, sl], cs, sa, sb)
    v_o[...] = hkv[:, kvd:]


def _prep_sample(h, conv_w, l, tabs, st1, st2, dec_seq, cdim, adim, kvd):
    m = h.shape[0]
    qblk = 3 * cdim // adim
    kvblk = (3 * cdim + adim) // (2 * kvd)
    z = lambda i: (0, 0)
    return pl.pallas_call(
        functools.partial(_prep_sample_kernel, dec_seq=dec_seq),
        grid=(1,),
        in_specs=[pl.BlockSpec((m, cdim), lambda i: (0, 0)),
                  pl.BlockSpec((m, cdim), lambda i: (0, 1)),
                  pl.BlockSpec((m, cdim), lambda i: (0, 2)),
                  pl.BlockSpec((m, adim), lambda i: (0, qblk)),
                  pl.BlockSpec((m, 2 * kvd), lambda i: (0, kvblk)),
                  pl.BlockSpec((None, 3, cdim), lambda i: (l, 0, 0)),
                  pl.BlockSpec((m, LANES), z), pl.BlockSpec((m, LANES), z), pl.BlockSpec((m, LANES), z),
                  pl.BlockSpec((m, cdim), z), pl.BlockSpec((m, cdim), z)],
        out_specs=[pl.BlockSpec((m, cdim), z), pl.BlockSpec((m, adim), z),
                   pl.BlockSpec((m, kvd), z), pl.BlockSpec((m, kvd), z), pl.BlockSpec((m, cdim), z)],
        out_shape=[jax.ShapeDtypeStruct((m, cdim), BF16),
                   jax.ShapeDtypeStruct((m, adim), F32),
                   jax.ShapeDtypeStruct((m, kvd), F32),
                   jax.ShapeDtypeStruct((m, kvd), F32),
                   jax.ShapeDtypeStruct((m, cdim), F32)],
        compiler_params=_params("arbitrary"),
        name="prep_sample",
    )(h, h, h, h, h, conv_w, *tabs, st1, st2)


def _sink_softmax(s, mask, sink_col):
    s = jnp.where(mask, s, -jnp.inf)
    m = jnp.maximum(jnp.max(s, axis=-1, keepdims=True), sink_col)
    p = jnp.exp(s - m)
    den = jnp.sum(p, axis=-1, keepdims=True) + jnp.exp(sink_col - m)
    return p * (1.0 / den)


def _attn_prompt_kernel(sink_ref, q_ref, kp_ref, kc_ref, vp_ref, vc_ref, o_ref, bias_ref, *, l):
    n = pl.program_id(1)
    w = q_ref.shape[0]
    chunks = q_ref.shape[1] // LANES
    pairs = kc_ref.shape[1] // LANES
    cpp = chunks // pairs
    lo = lax.broadcasted_iota(jnp.int32, (w, LANES), 1) < HEAD_DIM
    qi = lax.broadcasted_iota(jnp.int32, (w, 2 * w), 0)
    sj = lax.broadcasted_iota(jnp.int32, (w, 2 * w), 1)
    mask = (sj <= qi + w) & (sj > qi) & ((n > 0) | (sj >= w))
    bias_ref[...] = jnp.where(mask, 0.0, -jnp.inf)
    zero = jnp.zeros((w, LANES), BF16)
    for p in range(pairs):
        ksl = slice(p * LANES, (p + 1) * LANES)
        kk = jnp.concatenate([kp_ref[:, ksl], kc_ref[:, ksl]], axis=0)
        vv = jnp.concatenate([vp_ref[:, ksl], vc_ref[:, ksl]], axis=0)
        for j in range(cpp):
            c = p * cpp + j
            qc = q_ref[:, c * LANES:(c + 1) * LANES]
            halves = []
            for half, qh in enumerate((jnp.where(lo, qc, zero), jnp.where(lo, zero, qc))):
                sink = sink_ref[l, 2 * c + half]
                s = lax.dot_general(qh, kk, (((1,), (1,)), ((), ())), preferred_element_type=F32) + bias_ref[...]
                m = jnp.maximum(jnp.max(s, axis=-1, keepdims=True), sink)
                pexp = jnp.exp(s - m)
                den = jnp.sum(pexp, axis=-1, keepdims=True) + jnp.exp(sink - m)
                pr = (pexp * (1.0 / den)).astype(BF16)
                halves.append(jnp.dot(pr, vv, preferred_element_type=F32))
            o_ref[:, c * LANES:(c + 1) * LANES] = jnp.where(lo, halves[0], halves[1]).astype(BF16)


def _attn_prompt(sinks, q, k, v, l, batch, seq):
    m, adim = q.shape
    kvd = k.shape[1]
    nb = seq // WINDOW
    cur = lambda b, n: (b * nb + n, 0)
    prev = lambda b, n: (b * nb + jnp.maximum(n - 1, 0), 0)
    return pl.pallas_call(
        functools.partial(_attn_prompt_kernel, l=l),
        grid=(batch, nb),
        in_specs=[pl.BlockSpec(memory_space=pltpu.SMEM),
                  pl.BlockSpec((WINDOW, adim), cur),
                  pl.BlockSpec((WINDOW, kvd), prev), pl.BlockSpec((WINDOW, kvd), cur),
                  pl.BlockSpec((WINDOW, kvd), prev), pl.BlockSpec((WINDOW, kvd), cur)],
        out_specs=pl.BlockSpec((WINDOW, adim), cur),
        out_shape=jax.ShapeDtypeStruct((m, adim), BF16),
        scratch_shapes=[pltpu.VMEM((WINDOW, 2 * WINDOW), F32)],
        compiler_params=_params("arbitrary", "arbitrary"),
        name="attn_prompt",
    )(sinks, q, k, k, v, v)


def _attn_sample_kernel(sink_ref, q_ref, kn_ref, vn_ref, ck_ref, cv_ref,
                        o_ref, kw_ref, vw_ref, kk_ref, vv_ref, qs_ref, *, l, scale):
    t = q_ref.shape[0]
    w = ck_ref.shape[0]
    tot = kk_ref.shape[0]
    chunks = q_ref.shape[1] // LANES
    pairs = ck_ref.shape[1] // LANES
    cpp = chunks // pairs
    rows = 2 * cpp * t
    kk_ref[0:w, :] = ck_ref[...]
    vv_ref[0:w, :] = cv_ref[...]
    pad = jnp.zeros((tot - w - t, kk_ref.shape[1]), F32)
    kk_ref[w:w + t, :] = kn_ref[...]
    vv_ref[w:w + t, :] = vn_ref[...]
    kk_ref[w + t:, :] = pad
    vv_ref[w + t:, :] = pad
    kw_ref[...] = pltpu.roll(kk_ref[...], tot - t, 0)[0:w, :]
    vw_ref[...] = pltpu.roll(vv_ref[...], tot - t, 0)[0:w, :]

    lo = lax.broadcasted_iota(jnp.int32, (t, LANES), 1) < HEAD_DIM
    qi = lax.broadcasted_iota(jnp.int32, (rows, tot), 0) % t
    sj = lax.broadcasted_iota(jnp.int32, (rows, tot), 1)
    mask = (sj <= qi + w) & (sj > qi)
    rowid = lax.broadcasted_iota(jnp.int32, (rows, 1), 0) // t
    for p in range(pairs):
        ksl = slice(p * LANES, (p + 1) * LANES)
        sink_col = jnp.zeros((rows, 1), F32)
        for j in range(cpp):
            c = p * cpp + j
            qc = q_ref[:, c * LANES:(c + 1) * LANES]
            qs_ref[(2 * j) * t:(2 * j + 1) * t, :] = jnp.where(lo, qc, 0.0)
            qs_ref[(2 * j + 1) * t:(2 * j + 2) * t, :] = jnp.where(lo, 0.0, qc)
            sink_col = jnp.where(rowid == 2 * j, sink_ref[l, 2 * c], sink_col)
            sink_col = jnp.where(rowid == 2 * j + 1, sink_ref[l, 2 * c + 1], sink_col)
        s = lax.dot_general(qs_ref[...].astype(BF16), kk_ref[:, ksl].astype(BF16),
                            (((1,), (1,)), ((), ())), preferred_element_type=F32) * scale
        pr = _sink_softmax(s, mask, sink_col)
        qs_ref[...] = jnp.dot(pr.astype(BF16), vv_ref[:, ksl].astype(BF16), preferred_element_type=F32)
        for j in range(cpp):
            c = p * cpp + j
            o_ref[:, c * LANES:(c + 1) * LANES] = jnp.where(
                lo, qs_ref[(2 * j) * t:(2 * j + 1) * t, :], qs_ref[(2 * j + 1) * t:(2 * j + 2) * t, :]).astype(BF16)


def _attn_sample(sinks, q, kn, vn, cache_k, cache_v, l):
    nseq, t, adim = q.shape
    w, kvd = cache_k.shape[2], cache_k.shape[3]
    tot = -(-(w + t) // 8) * 8
    rows = 2 * (adim // kvd) * t
    seq3 = lambda b: (b, 0, 0)
    cache = lambda b: (l, b, 0, 0)
    return pl.pallas_call(
        functools.partial(_attn_sample_kernel, l=l, scale=HEAD_DIM ** -0.5),
        grid=(nseq,),
        in_specs=[pl.BlockSpec(memory_space=pltpu.SMEM),
                  pl.BlockSpec((None, t, adim), seq3),
                  pl.BlockSpec((None, t, kvd), seq3), pl.BlockSpec((None, t, kvd), seq3),
                  pl.BlockSpec((None, None, w, kvd), cache), pl.BlockSpec((None, None, w, kvd), cache)],
        out_specs=[pl.BlockSpec((None, t, adim), seq3),
                   pl.BlockSpec((None, w, kvd), seq3), pl.BlockSpec((None, w, kvd), seq3)],
        out_shape=[jax.ShapeDtypeStruct((nseq, t, adim), BF16),
                   jax.ShapeDtypeStruct((nseq, w, kvd), F32),
                   jax.ShapeDtypeStruct((nseq, w, kvd), F32)],
        scratch_shapes=[pltpu.VMEM((tot, kvd), F32), pltpu.VMEM((tot, kvd), F32),
                        pltpu.VMEM((rows, LANES), F32)],
        compiler_params=_params("arbitrary"),
        name="attn_sample",
    )(sinks, q, kn, vn, cache_k, cache_v)


def _layer_norm(z, g, b):
    mu = jnp.mean(z, axis=-1, keepdims=True)
    d = z - mu
    var = jnp.mean(d * d, axis=-1, keepdims=True)
    return d * lax.rsqrt(var + LN_EPS) * g + b


def _top2(sg, lane, n):
    m1 = jnp.max(sg, axis=-1, keepdims=True)
    i1 = jnp.min(jnp.where(sg == m1, lane, n), axis=-1, keepdims=True)
    rest = jnp.where(lane == i1, -jnp.inf, sg)
    m2 = jnp.max(rest, axis=-1, keepdims=True)
    i2 = jnp.min(jnp.where(rest == m2, lane, n), axis=-1, keepdims=True)
    return m1, i1, m2, i2


def _mix_out_kernel(conv_ref, attn_ref, x_ref, wo_ref, g_ref, b_ref, wr_ref, br_ref, cnt_in,
                    x1_ref, route_ref, cnt_ref, carry, *, l, alpha, has_alias):
    cdim = conv_ref.shape[1]
    ne = wr_ref.shape[1]
    sub = min(x_ref.shape[0], MIX_SUB_ROWS)

    @pl.when(pl.program_id(0) == 0)
    def _():
        carry[...] = cnt_in[...]

    counts = carry[0:1, 0:ne]
    for r0 in range(0, x_ref.shape[0], sub):
        rows = slice(r0, r0 + sub)
        y = jnp.dot(conv_ref[rows, :], wo_ref[0:cdim, :], preferred_element_type=F32)
        y = y + jnp.dot(attn_ref[rows, :], wo_ref[cdim:, :], preferred_element_type=F32)
        x1 = _layer_norm(alpha * x_ref[rows, :] + y, g_ref[...], b_ref[...])
        x1_ref[rows, :] = x1
        logits = jnp.dot(x1.astype(BF16), wr_ref[...], preferred_element_type=F32)
        route, counts = _route(logits, br_ref[...], counts)
        route_ref[rows, :] = route
    carry[0:1, 0:ne] = counts
    cnt_ref[...] = carry[...]


def _route(logits, bias, counts):
    tm, ne = logits.shape
    epg = ne // N_EXPERT_GROUPS
    scores = jax.nn.sigmoid(logits)
    sel = scores + bias
    lane = lax.broadcasted_iota(jnp.int32, (tm, ne), 1).astype(F32)
    grp = jnp.floor(lane * (1.0 / epg))
    best_s = None
    best_g = None
    for g in range(N_EXPERT_GROUPS):
        m1, _, m2, _ = _top2(jnp.where(grp == g, sel, -jnp.inf), lane, float(ne))
        gs = m1 + m2
        if g == 0:
            best_s, best_g = gs, jnp.zeros_like(gs)
        else:
            upd = gs > best_s
            best_g = jnp.where(upd, float(g), best_g)
            best_s = jnp.where(upd, gs, best_s)
    _, e1, _, e2 = _top2(jnp.where(grp == best_g, sel, -jnp.inf), lane, float(ne))
    hit1 = lane == e1
    hit2 = lane == e2
    w1 = jnp.sum(jnp.where(hit1, scores, 0.0), axis=-1, keepdims=True)
    w2 = jnp.sum(jnp.where(hit2, scores, 0.0), axis=-1, keepdims=True)
    wsum = w1 + w2

    onehot = jnp.where(hit1 | hit2, 1.0, 0.0)
    tri = (lax.broadcasted_iota(jnp.int32, (tm, tm), 0) > lax.broadcasted_iota(jnp.int32, (tm, tm), 1))
    before = jnp.dot(jnp.where(tri, 1.0, 0.0).astype(BF16), onehot.astype(BF16), preferred_element_type=F32)
    before = before + counts
    r1 = jnp.sum(jnp.where(hit1, before, 0.0), axis=-1, keepdims=True)
    r2 = jnp.sum(jnp.where(hit2, before, 0.0), axis=-1, keepdims=True)

    col = lax.broadcasted_iota(jnp.int32, (tm, LANES), 1)
    out = jnp.zeros((tm, LANES), F32)
    for c, val in enumerate((e1, e2, w1 / wsum, w2 / wsum, r1, r2)):
        out = jnp.where(col == c, val, out)
    return out, counts + jnp.sum(onehot, axis=0, keepdims=True)


def _mix_out(conv, attn, x, wo, ln_g, ln_b, wr, br, cnt_in, l, alpha, tm, total_rows, row_block0, prev=None):
    m, d = x.shape
    cdim = conv.shape[1]
    ne = wr.shape[1]
    rowi = lambda i: (i, 0)
    rowo = lambda i: (i + row_block0, 0)
    const = lambda i: (0, 0)
    lay = lambda i: (l, 0, 0)
    in_specs = [pl.BlockSpec((tm, cdim), rowi), pl.BlockSpec((tm, d - cdim), rowi), pl.BlockSpec((tm, d), rowi),
                pl.BlockSpec((d, d), const, pipeline_mode=pl.Buffered(1)),
                pl.BlockSpec((None, 1, d), lay), pl.BlockSpec((None, 1, d), lay),
                pl.BlockSpec((d, ne), const), pl.BlockSpec((1, ne), const),
                pl.BlockSpec((8, LANES), const)]
    args = [conv, attn, x, wo, ln_g, ln_b, wr, br, cnt_in]
    aliases = {}
    if prev is not None:
        in_specs += [pl.BlockSpec(memory_space=pl.ANY), pl.BlockSpec(memory_space=pl.ANY)]
        args += list(prev)
        aliases = {len(args) - 2: 0, len(args) - 1: 1}

    def body(*refs):
        ins = refs[:9]
        outs = refs[-4:]
        _mix_out_kernel(*ins, *outs, l=l, alpha=alpha, has_alias=prev is not None)

    return pl.pallas_call(
        body,
        grid=(m // tm,),
        in_specs=in_specs,
        out_specs=[pl.BlockSpec((tm, d), rowo), pl.BlockSpec((tm, LANES), rowo), pl.BlockSpec((8, LANES), const)],
        out_shape=[jax.ShapeDtypeStruct((total_rows, d), F32),
                   jax.ShapeDtypeStruct((total_rows, LANES), F32),
                   jax.ShapeDtypeStruct((8, LANES), F32)],
        scratch_shapes=[pltpu.VMEM((8, LANES), F32)],
        input_output_aliases=aliases,
        compiler_params=_params("arbitrary"),
        name="mix_out_sample" if prev is not None else "mix_out_prompt",
    )(*args)


def _scatter_rows_kernel(slot_ref, x_ref, o_ref, sem, *, n_tok):
    ct = x_ref.shape[0]
    c = pl.program_id(0)
    last = pl.num_programs(0) - 1
    tail = n_tok - (pl.cdiv(n_tok, ct) - 1) * ct

    def run(count):
        def issue(jb, carry):
            j0 = pl.multiple_of(jb * ISSUE_UNROLL, ISSUE_UNROLL)
            for u in range(ISSUE_UNROLL):
                for k in range(TOP_K):
                    pltpu.make_async_copy(x_ref.at[pl.ds(j0 + u, 1)],
                                          o_ref.at[pl.ds(slot_ref[TOP_K * (j0 + u) + k], 1)], sem).start()
            return carry

        lax.fori_loop(0, count // ISSUE_UNROLL, issue, 0)
        for _ in range(TOP_K):
            pltpu.make_async_copy(x_ref.at[pl.ds(0, count)], x_ref.at[pl.ds(0, count)], sem).wait()

    if tail == ct:
        run(ct)
    else:
        pl.when(c < last)(lambda: run(ct))
        pl.when(c == last)(lambda: run(tail))


def _scatter_rows(x1, slot_flat, n_slots):
    total, d = x1.shape
    ct = SLOT_BLOCK // TOP_K
    return pl.pallas_call(
        functools.partial(_scatter_rows_kernel, n_tok=total),
        grid=(pl.cdiv(total, ct),),
        in_specs=[pl.BlockSpec((SLOT_BLOCK,), lambda c: (c,), memory_space=pltpu.SMEM),
                  pl.BlockSpec((ct, d), lambda c: (c, 0))],
        out_specs=pl.BlockSpec(memory_space=pl.ANY),
        out_shape=jax.ShapeDtypeStruct((n_slots, d), x1.dtype),
        scratch_shapes=[pltpu.SemaphoreType.DMA(())],
        compiler_params=_params("arbitrary"),
        name="scatter_rows",
    )(slot_flat, x1)


def _expert_kernel(te_ref, nv_ref, nu_ref, x_ref, wg_ref, wu_ref, wd_ref, o_ref, xb_ref, wgu_ref, h_ref):
    i = pl.program_id(0)
    s = pl.program_id(1)
    nf, _, fc = h_ref.shape

    @pl.when(i < nu_ref[0])
    def _():
        @pl.when(s == 0)
        def _():
            rid = lax.broadcasted_iota(jnp.int32, (x_ref.shape[0], 1), 0)
            xb_ref[...] = jnp.where(rid < nv_ref[i], x_ref[...], 0.0).astype(BF16)

        @pl.when(s < nf)
        def _():
            wgu_ref[:, :fc] = wg_ref[...].astype(BF16)
            wgu_ref[:, fc:] = wu_ref[...].astype(BF16)
            gu = jnp.dot(xb_ref[...], wgu_ref[...], preferred_element_type=F32)
            h_ref[s] = (jax.nn.silu(gu[:, :fc]) * gu[:, fc:]).astype(BF16)

        @pl.when(s >= nf)
        def _():
            y = jnp.dot(h_ref[0], wd_ref[0:fc, :].astype(BF16), preferred_element_type=F32)
            for f in range(1, nf):
                y = y + jnp.dot(h_ref[f], wd_ref[f * fc:(f + 1) * fc, :].astype(BF16), preferred_element_type=F32)
            o_ref[...] = y


def _expert_ffn(xs, w_gate, w_up, w_down, tile_e, tile_nv, n_used, l):
    rows, d = xs.shape
    dff = w_gate.shape[3]
    nf = dff // FF_CHUNK
    nc = d // OUT_CHUNK
    steps = nf + nc
    nt = rows // EXPERT_TILE

    def item(i, nu):
        return jnp.minimum(i, jnp.maximum(nu[0] - 1, 0))

    def step(i, s, nu):
        return jnp.where(i < nu[0], s, steps - 1)

    def gmap(i, s, te, nv, nu):
        s = step(i, s, nu)
        e = jnp.where(s < nf, te[item(i, nu)], te[item(i + 1, nu)])
        return (l, e, 0, jnp.where(s < nf, s, 0))

    col = lambda i, s, nu: jnp.maximum(step(i, s, nu) - nf, 0)
    xmap = lambda i, s, te, nv, nu: (item(i, nu), 0)
    dmap = lambda i, s, te, nv, nu: (l, te[item(i, nu)], 0, col(i, s, nu))
    omap = lambda i, s, te, nv, nu: (item(i, nu), col(i, s, nu))
    return pl.pallas_call(
        _expert_kernel,
        grid_spec=pltpu.PrefetchScalarGridSpec(
            num_scalar_prefetch=3,
            grid=(nt, steps),
            in_specs=[pl.BlockSpec((EXPERT_TILE, d), xmap),
                      pl.BlockSpec((None, None, d, FF_CHUNK), gmap),
                      pl.BlockSpec((None, None, d, FF_CHUNK), gmap),
                      pl.BlockSpec((None, None, dff, OUT_CHUNK), dmap)],
            out_specs=pl.BlockSpec((EXPERT_TILE, OUT_CHUNK), omap),
            scratch_shapes=[pltpu.VMEM((EXPERT_TILE, d), BF16),
                            pltpu.VMEM((d, 2 * FF_CHUNK), BF16),
                            pltpu.VMEM((nf, EXPERT_TILE, FF_CHUNK), BF16)]),
        out_shape=jax.ShapeDtypeStruct((rows, d), F32),
        compiler_params=_params("arbitrary", "arbitrary"),
        name="expert_ffn",
    )(tile_e, tile_nv, n_used, xs, w_gate, w_up, w_down)


def _combine_kernel(slot_cur, slot_nxt, x1_ref, r_ref, g_ref, b_ref, y_hbm, o_ref, ybuf, sems, *, alpha):
    i = pl.program_id(0)
    ct = x1_ref.shape[0]

    def fetch(slot_ref, buf):
        def issue(jb, carry):
            j0 = pl.multiple_of(jb * ISSUE_UNROLL, ISSUE_UNROLL)
            for u in range(ISSUE_UNROLL):
                for k in range(TOP_K):
                    pltpu.make_async_copy(y_hbm.at[pl.ds(slot_ref[TOP_K * (j0 + u) + k], 1)],
                                          ybuf.at[buf, k, pl.ds(j0 + u, 1)], sems.at[buf]).start()
            return carry

        lax.fori_loop(0, ct // ISSUE_UNROLL, issue, 0)

    @pl.when(i == 0)
    def _():
        fetch(slot_cur, 0)

    @pl.when(i + 1 < pl.num_programs(0))
    def _():
        fetch(slot_nxt, (i + 1) % 2)

    buf = i % 2
    for k in range(TOP_K):
        pltpu.make_async_copy(ybuf.at[buf, k], ybuf.at[buf, k], sems.at[buf]).wait()
    r = r_ref[...]
    moe = ybuf[buf, 0] * r[:, 2:3] + ybuf[buf, 1] * r[:, 3:4]
    o_ref[...] = _layer_norm(alpha * x1_ref[...] + moe, g_ref[...], b_ref[...])


def _combine(x1, yb, slot_flat, route, ln_g, ln_b, l, alpha, tm, row0, m):
    d = x1.shape[1]
    nsteps = m // tm
    rb0 = row0 // tm
    sb0 = row0 * TOP_K // SLOT_BLOCK
    sper = pl.cdiv(tm * TOP_K, SLOT_BLOCK)
    rowo = lambda i: (i + rb0, 0)
    lay = lambda i: (l, 0, 0)
    return pl.pallas_call(
        functools.partial(_combine_kernel, alpha=alpha),
        grid=(nsteps,),
        in_specs=[pl.BlockSpec((SLOT_BLOCK,), lambda i: (sb0 + i * sper,), memory_space=pltpu.SMEM),
                  pl.BlockSpec((SLOT_BLOCK,), lambda i: (sb0 + jnp.minimum(i + 1, nsteps - 1) * sper,),
                               memory_space=pltpu.SMEM),
                  pl.BlockSpec((tm, d), rowo),
                  pl.BlockSpec((tm, LANES), rowo),
                  pl.BlockSpec((None, 1, d), lay), pl.BlockSpec((None, 1, d), lay),
                  pl.BlockSpec(memory_space=pl.ANY)],
        out_specs=pl.BlockSpec((tm, d), lambda i: (i, 0)),
        out_shape=jax.ShapeDtypeStruct((m, d), F32),
        scratch_shapes=[pltpu.VMEM((2, TOP_K, tm, d), F32), pltpu.SemaphoreType.DMA((2,))],
        compiler_params=_params("arbitrary"),
        name="combine",
    )(slot_flat, slot_flat, x1, route, ln_g, ln_b, yb)


def _rope_tables(pos):
    half = ROT_DIM // 2
    inv = ROPE_THETA ** (-jnp.arange(half, dtype=F32) * 2.0 / ROT_DIM)
    ang = pos.astype(F32)[:, None] * inv[None, :]
    cos, sin = jnp.cos(ang), jnp.sin(ang)
    n = pos.shape[0]
    rest = HEAD_DIM - ROT_DIM
    cs = jnp.concatenate([cos, cos, jnp.ones((n, rest), F32)], axis=1)
    sa = jnp.concatenate([-sin, jnp.zeros((n, half + rest), F32)], axis=1)
    sb = jnp.concatenate([jnp.zeros((n, half), F32), sin, jnp.zeros((n, rest), F32)], axis=1)
    rep = LANES // HEAD_DIM
    return tuple(jnp.tile(t, (1, rep)) for t in (cs, sa, sb))


def _pair_heads(t, axis, n_heads):
    group = n_heads // N_KV_HEADS
    inner = t.shape[axis] // n_heads
    shp = t.shape[:axis] + (N_KV_HEADS // 2, 2, group, inner) + t.shape[axis + 1:]
    t = jnp.swapaxes(t.reshape(shp), axis + 1, axis + 2)
    return t.reshape(t.shape[:axis] + (n_heads * inner,) + t.shape[axis + 4:])


def kernel(x_prompt, x_sample, state_conv, cache_k, cache_v, w_in, conv_w, attn_sinks, w_o,
           ln1_g, ln1_b, w_router, b_router, w_gate, w_up, w_down, ln2_g, ln2_b):
    batch, seq, d = x_prompt.shape
    nseq, dec_seq, _ = x_sample.shape
    depth = w_in.shape[0]
    cdim = conv_w.shape[2]
    adim = d - cdim
    kvd = N_KV_HEADS * HEAD_DIM
    n_heads = adim // HEAD_DIM
    ne = w_router.shape[1]
    mp, ms = batch * seq, nseq * dec_seq
    total = mp + ms
    assign = total * TOP_K
    alpha = (2 * depth) ** 0.25

    q0, q1 = 3 * cdim, 3 * cdim + adim
    sinks = _pair_heads(attn_sinks, 1, n_heads)
    wr_b = w_router.astype(BF16)
    br = b_router.reshape(1, ne)

    tabs_p = _rope_tables(jnp.arange(seq))
    tabs_s = _rope_tables(jnp.tile(PAST_LEN + jnp.arange(dec_seq), nseq))
    cache_k2 = cache_k.reshape(depth, nseq, WINDOW, kvd)
    cache_v2 = cache_v.reshape(depth, nseq, WINDOW, kvd)
    ln1g, ln1b = ln1_g.reshape(depth, 1, d), ln1_b.reshape(depth, 1, d)
    ln2g, ln2b = ln2_g.reshape(depth, 1, d), ln2_b.reshape(depth, 1, d)

    n_items = pl.cdiv(assign, EXPERT_TILE) + ne
    slot_len = pl.cdiv(assign, SLOT_BLOCK) * SLOT_BLOCK
    items = jnp.arange(n_items, dtype=jnp.int32)

    xp = x_prompt.reshape(mp, d)
    xs_ = x_sample.reshape(ms, d)
    outs = [[] for _ in range(6)]
    for l in range(depth):
        w_in_b = jnp.concatenate([w_in[l, :, :q0].astype(BF16),
                                  _pair_heads(w_in[l, :, q0:q1].astype(BF16), 1, n_heads),
                                  w_in[l, :, q1:].astype(BF16)], axis=1)
        w_o_b = jnp.concatenate([w_o[l, :cdim].astype(BF16),
                                 _pair_heads(w_o[l, cdim:].astype(BF16), 0, n_heads)], axis=0)

        hp = _matmul(xp, w_in_b, 512, 1536, "proj_prompt")
        conv_p, q_p, k_p, v_p, cst_p, kw_p, vw_p = _prep_prompt(
            hp, conv_w, l, tabs_p, batch, seq, cdim, adim, kvd, 256)
        attn_p = _attn_prompt(sinks, q_p, k_p, v_p, l, batch, seq)

        hs = _matmul(xs_, w_in_b, ms, 1536, "proj_sample")
        st = state_conv[l]
        st1 = jnp.repeat(st[:, 1], dec_seq, axis=0)
        st2 = jnp.stack([st[:, 0], st[:, 1]] + [st[:, 1]] * (dec_seq - 2), axis=1).reshape(ms, cdim)
        conv_s, q_s, kn_s, vn_s, inner_s = _prep_sample(hs, conv_w, l, tabs_s, st1, st2, dec_seq, cdim, adim, kvd)
        attn_s, kw_s, vw_s = _attn_sample(
            sinks, q_s.reshape(nseq, dec_seq, adim), kn_s.reshape(nseq, dec_seq, kvd),
            vn_s.reshape(nseq, dec_seq, kvd), cache_k2, cache_v2, l)

        zeros_cnt = jnp.zeros((8, LANES), F32)
        x1, route, cnt = _mix_out(conv_p, attn_p, xp, w_o_b, ln1g, ln1b, wr_b, br, zeros_cnt,
                                  l, alpha, 512, total, 0)
        x1, route, cnt = _mix_out(conv_s, attn_s.reshape(ms, adim), xs_, w_o_b, ln1g, ln1b, wr_b, br, cnt,
                                  l, alpha, ms, total, mp // ms, prev=(x1, route))

        e = route[:, 0:TOP_K].astype(jnp.int32)
        rank = route[:, 4:4 + TOP_K].astype(jnp.int32)
        counts = cnt[0, :ne].astype(jnp.int32)
        tiles_e = (counts + EXPERT_TILE - 1) // EXPERT_TILE
        tile_end = jnp.cumsum(tiles_e)
        tile_start = tile_end - tiles_e
        onehot = e[:, :, None] == jnp.arange(ne)[None, None, :]
        pos = jnp.sum(jnp.where(onehot, (tile_start * EXPERT_TILE)[None, None, :], 0), axis=-1) + rank
        slot_flat = jnp.pad(pos.reshape(-1), (0, slot_len - assign))
        n_used = tile_end[-1:].astype(jnp.int32)
        item_e = jnp.minimum(jnp.sum(tile_end[None, :] <= items[:, None], axis=1), ne - 1).astype(jnp.int32)
        item_nv = jnp.clip(counts[item_e] - (items - tile_start[item_e]) * EXPERT_TILE, 0, EXPERT_TILE)

        xsorted = _scatter_rows(x1, slot_flat, n_items * EXPERT_TILE)
        yb = _expert_ffn(xsorted, w_gate, w_up, w_down, item_e, item_nv.astype(jnp.int32), n_used, l)

        xp = _combine(x1, yb, slot_flat, route, ln2g, ln2b, l, alpha, 512, 0, mp)
        xs_ = _combine(x1, yb, slot_flat, route, ln2g, ln2b, l, alpha, ms, mp, ms)

        inner3 = inner_s.reshape(nseq, dec_seq, cdim)
        for lst, val in zip(outs, (cst_p, kw_p.reshape(batch, WINDOW, N_KV_HEADS, HEAD_DIM),
                                   vw_p.reshape(batch, WINDOW, N_KV_HEADS, HEAD_DIM),
                                   inner3[:, dec_seq - 2:],
                                   kw_s.reshape(nseq, WINDOW, N_KV_HEADS, HEAD_DIM),
                                   vw_s.reshape(nseq, WINDOW, N_KV_HEADS, HEAD_DIM))):
            lst.append(val)

    return (xp.reshape(batch, seq, d), xs_.reshape(nseq, dec_seq, d), *[jnp.stack(o) for o in outs])
```

```python
import functools
import math

import jax
import jax.numpy as jnp
from jax import lax
from jax.experimental import pallas as pl
from jax.experimental.pallas import tpu as pltpu

F32 = jnp.float32
BF16 = jnp.bfloat16

HEAD_DIM = 64
N_KV_HEADS = 4
ROT_DIM = HEAD_DIM // 4
ROPE_THETA = 500000.0
WINDOW = 128
PAST_LEN = 16384
N_EXPERT_GROUPS = 4
TOP_K = 2
LN_EPS = 1e-5
QK_SCALE = HEAD_DIM ** -0.5
assert math.frexp(QK_SCALE)[0] == 0.5

LANES = 128
VMEM_LIMIT = 56 * 1024 * 1024

EXPERT_TILE = 640
FF_CHUNK = 512
OUT_CHUNK = 1024
W_SPLIT = 2
CONV_GROUP = 512
SLOT_BLOCK = 1024
ISSUE_UNROLL = 8
MIX_SUB_ROWS = 256


def _params(*sem):
    return pltpu.CompilerParams(dimension_semantics=sem, vmem_limit_bytes=VMEM_LIMIT)


def _rope_chunk(xc, cs, sa, sb):
    return xc * cs + pltpu.roll(xc, LANES - ROT_DIM // 2, 1) * sa + pltpu.roll(xc, ROT_DIM // 2, 1) * sb


def _conv(inner, gate, w_ref, prev1, prev2):
    return gate * (w_ref[0:1, :] * prev2 + w_ref[1:2, :] * prev1 + w_ref[2:3, :] * inner)


def _paired_head_order(n_heads):
    group = n_heads // N_KV_HEADS
    return [(2 * pair + half) * group + j
            for pair in range(N_KV_HEADS // 2) for j in range(group) for half in range(2)]


def _pair_q_chunks(chunks):
    order = _paired_head_order(2 * len(chunks))
    lo = lax.broadcasted_iota(jnp.int32, chunks[0].shape, 1) < HEAD_DIM
    out = []
    for c in range(len(chunks)):
        halves = []
        for half, head in enumerate(order[2 * c:2 * c + 2]):
            src = chunks[head // 2]
            halves.append(src if head % 2 == half else pltpu.roll(src, HEAD_DIM, 1))
        out.append(jnp.where(lo, halves[0], halves[1]))
    return out


def _conv_proj_kernel(x_ref, wc_ref, wb_ref, wu_ref, cw_ref, *refs, tiles_per_seq, dec_seq):
    if dec_seq is None:
        conv_o, cst_o, wbf, carry = refs
    else:
        st1, st2, conv_o, inner_o, wbf = refs
    i = pl.program_id(1)

    @pl.when(i == 0)
    def _():
        for k, w in enumerate((wc_ref, wb_ref, wu_ref)):
            wbf[k] = w[...].astype(BF16)

    xb = x_ref[...].astype(BF16)
    c, gate, u = (jnp.dot(xb, wbf[k], preferred_element_type=F32) for k in range(3))
    inner = c * u
    tm, cw = inner.shape
    row = lax.broadcasted_iota(jnp.int32, (tm, cw), 0)
    if dec_seq is None:
        @pl.when(i % tiles_per_seq == 0)
        def _():
            carry[...] = jnp.zeros_like(carry)

        c1, c2 = carry[7:8, :], carry[6:7, :]
        prev1 = jnp.where(row == 0, c1, pltpu.roll(inner, 1, 0))
        prev2 = jnp.where(row == 0, c2, jnp.where(row == 1, c1, pltpu.roll(inner, 2, 0)))
        carry[...] = inner[tm - 8:, :]
        cst_o[...] = inner[tm - 2:, :]
    else:
        t = row % dec_seq
        prev1 = jnp.where(t == 0, st1[...], pltpu.roll(inner, 1, 0))
        prev2 = jnp.where(t < 2, st2[...], pltpu.roll(inner, 2, 0))
        inner_o[...] = inner
    conv_o[...] = _conv(inner, gate, cw_ref, prev1, prev2).astype(BF16)


def _conv_proj(x, w_in, conv_w, l, tm, cdim, seq=None, batch=None, state=None, dec_seq=None):
    m, k = x.shape
    cw = CONV_GROUP
    groups = cdim // cw
    sec = lambda s: pl.BlockSpec((None, k, cw), lambda g, i: (l, 0, s * groups + g))
    blk = pl.BlockSpec((tm, cw), lambda g, i: (i, g))
    in_specs = [pl.BlockSpec((tm, k), lambda g, i: (i, 0)), sec(0), sec(1), sec(2),
                pl.BlockSpec((None, 3, cw), lambda g, i: (l, 0, g))]
    scratch = [pltpu.VMEM((3, k, cw), BF16)]
    if state is None:
        tps = seq // tm
        args = ()
        out_specs = [blk, pl.BlockSpec((None, 2, cw), lambda g, i: (i // tps, 0, g))]
        out_shape = [jax.ShapeDtypeStruct((m, cdim), BF16), jax.ShapeDtypeStruct((batch, 2, cdim), F32)]
        scratch.append(pltpu.VMEM((8, cw), F32))
    else:
        tps = None
        args = state
        in_specs += [blk, blk]
        out_specs = [blk, blk]
        out_shape = [jax.ShapeDtypeStruct((m, cdim), BF16), jax.ShapeDtypeStruct((m, cdim), F32)]
    return pl.pallas_call(
        functools.partial(_conv_proj_kernel, tiles_per_seq=tps, dec_seq=dec_seq),
        grid=(groups, m // tm),
        in_specs=in_specs, out_specs=out_specs, out_shape=out_shape, scratch_shapes=scratch,
        compiler_params=_params("arbitrary", "arbitrary"),
        name="conv_proj_prompt" if state is None else "conv_proj_sample",
    )(x, w_in, w_in, w_in, conv_w, *args)


def _qkv_proj_kernel(x_ref, w_ref, cs_ref, sa_ref, sb_ref, q_o, k_o, v_o, *refs, adim, q_scale):
    wbf = refs[-1]

    @pl.when(pl.program_id(0) == 0)
    def _():
        wbf[...] = w_ref[...].astype(BF16)

    h = jnp.dot(x_ref[...].astype(BF16), wbf[...], preferred_element_type=F32)
    tm = h.shape[0]
    kvd = (h.shape[1] - adim) // 2
    cs, sa, sb = cs_ref[...], sa_ref[...], sb_ref[...]
    roped = [_rope_chunk(h[:, c * LANES:(c + 1) * LANES], cs, sa, sb) for c in range((adim + kvd) // LANES)]
    for c, qc in enumerate(_pair_q_chunks(roped[:adim // LANES])):
        q_o[:, c * LANES:(c + 1) * LANES] = (qc * q_scale).astype(q_o.dtype)
    for c, kc in enumerate(roped[adim // LANES:]):
        k_o[:, c * LANES:(c + 1) * LANES] = kc.astype(k_o.dtype)
    v = h[:, adim + kvd:]
    v_o[...] = v.astype(v_o.dtype)
    if len(refs) > 1:
        kw_o, vw_o = refs[:2]
        for c, kc in enumerate(roped[adim // LANES:]):
            kw_o[:, c * LANES:(c + 1) * LANES] = kc[tm - WINDOW:, :]
        vw_o[...] = v[tm - WINDOW:, :]


def _qkv_proj(x, w_in, l, tabs, tm, cdim, adim, kvd, seq=None, batch=None):
    m, k = x.shape
    n = adim + 2 * kvd
    row = lambda i: (i, 0)
    prompt = seq is not None
    tps = seq // tm if prompt else 1
    tab = pl.BlockSpec((tm, LANES), lambda i: (i % tps, 0))
    dt = BF16 if prompt else F32
    out_specs = [pl.BlockSpec((tm, adim), row), pl.BlockSpec((tm, kvd), row), pl.BlockSpec((tm, kvd), row)]
    out_shape = [jax.ShapeDtypeStruct((m, adim), dt), jax.ShapeDtypeStruct((m, kvd), dt),
                 jax.ShapeDtypeStruct((m, kvd), dt)]
    if prompt:
        win = pl.BlockSpec((None, WINDOW, kvd), lambda i: (i // tps, 0, 0))
        out_specs += [win, win]
        out_shape += [jax.ShapeDtypeStruct((batch, WINDOW, kvd), F32)] * 2
    return pl.pallas_call(
        functools.partial(_qkv_proj_kernel, adim=adim, q_scale=QK_SCALE if prompt else 1.0),
        grid=(m // tm,),
        in_specs=[pl.BlockSpec((tm, k), row),
                  pl.BlockSpec((None, k, n), lambda i: (l, 0, 3 * cdim // n), pipeline_mode=pl.Buffered(1)),
                  tab, tab, tab],
        out_specs=out_specs, out_shape=out_shape,
        scratch_shapes=[pltpu.VMEM((k, n), BF16)],
        compiler_params=_params("arbitrary"),
        name="qkv_proj_prompt" if prompt else "qkv_proj_sample",
    )(x, w_in, *tabs)


def _sink_softmax(s, mask, sink_col):
    s = jnp.where(mask, s, -jnp.inf)
    m = jnp.maximum(jnp.max(s, axis=-1, keepdims=True), sink_col)
    p = jnp.exp(s - m)
    den = jnp.sum(p, axis=-1, keepdims=True) + jnp.exp(sink_col - m)
    return p * (1.0 / den)


def _attn_prompt_kernel(sink_ref, q_ref, kp_ref, kc_ref, vp_ref, vc_ref, o_ref, bias_ref, *, l):
    n = pl.program_id(1)
    w = q_ref.shape[0]
    chunks = q_ref.shape[1] // LANES
    pairs = kc_ref.shape[1] // LANES
    cpp = chunks // pairs
    lo = lax.broadcasted_iota(jnp.int32, (w, LANES), 1) < HEAD_DIM
    qi = lax.broadcasted_iota(jnp.int32, (w, 2 * w), 0)
    sj = lax.broadcasted_iota(jnp.int32, (w, 2 * w), 1)
    mask = (sj <= qi + w) & (sj > qi) & ((n > 0) | (sj >= w))
    bias_ref[...] = jnp.where(mask, 0.0, -jnp.inf)
    zero = jnp.zeros((w, LANES), BF16)
    for p in range(pairs):
        ksl = slice(p * LANES, (p + 1) * LANES)
        kk = jnp.concatenate([kp_ref[:, ksl], kc_ref[:, ksl]], axis=0)
        vv = jnp.concatenate([vp_ref[:, ksl], vc_ref[:, ksl]], axis=0)
        for j in range(cpp):
            c = p * cpp + j
            qc = q_ref[:, c * LANES:(c + 1) * LANES]
            halves = []
            for half, qh in enumerate((jnp.where(lo, qc, zero), jnp.where(lo, zero, qc))):
                sink = sink_ref[l, 2 * c + half]
                s = lax.dot_general(qh, kk, (((1,), (1,)), ((), ())), preferred_element_type=F32) + bias_ref[...]
                m = jnp.maximum(jnp.max(s, axis=-1, keepdims=True), sink)
                pexp = jnp.exp(s - m)
                den = jnp.sum(pexp, axis=-1, keepdims=True) + jnp.exp(sink - m)
                pr = (pexp * (1.0 / den)).astype(BF16)
                halves.append(jnp.dot(pr, vv, preferred_element_type=F32))
            o_ref[:, c * LANES:(c + 1) * LANES] = jnp.where(lo, halves[0], halves[1]).astype(BF16)


def _attn_prompt(sinks, q, k, v, l, batch, seq):
    m, adim = q.shape
    kvd = k.shape[1]
    nb = seq // WINDOW
    cur = lambda b, n: (b * nb + n, 0)
    prev = lambda b, n: (b * nb + jnp.maximum(n - 1, 0), 0)
    return pl.pallas_call(
        functools.partial(_attn_prompt_kernel, l=l),
        grid=(batch, nb),
        in_specs=[pl.BlockSpec(memory_space=pltpu.SMEM),
                  pl.BlockSpec((WINDOW, adim), cur),
                  pl.BlockSpec((WINDOW, kvd), prev), pl.BlockSpec((WINDOW, kvd), cur),
                  pl.BlockSpec((WINDOW, kvd), prev), pl.BlockSpec((WINDOW, kvd), cur)],
        out_specs=pl.BlockSpec((WINDOW, adim), cur),
        out_shape=jax.ShapeDtypeStruct((m, adim), BF16),
        scratch_shapes=[pltpu.VMEM((WINDOW, 2 * WINDOW), F32)],
        compiler_params=_params("arbitrary", "arbitrary"),
        name="attn_prompt",
    )(sinks, q, k, k, v, v)


def _attn_sample_kernel(sink_ref, q_ref, kn_ref, vn_ref, ck_ref, cv_ref,
                        o_ref, kw_ref, vw_ref, kk_ref, vv_ref, qs_ref, *, l, scale):
    t = q_ref.shape[0]
    w = ck_ref.shape[0]
    tot = kk_ref.shape[0]
    chunks = q_ref.shape[1] // LANES
    pairs = ck_ref.shape[1] // LANES
    cpp = chunks // pairs
    rows = 2 * cpp * t
    kk_ref[0:w, :] = ck_ref[...]
    vv_ref[0:w, :] = cv_ref[...]
    pad = jnp.zeros((tot - w - t, kk_ref.shape[1]), F32)
    kk_ref[w:w + t, :] = kn_ref[...]
    vv_ref[w:w + t, :] = vn_ref[...]
    kk_ref[w + t:, :] = pad
    vv_ref[w + t:, :] = pad
    kw_ref[...] = pltpu.roll(kk_ref[...], tot - t, 0)[0:w, :]
    vw_ref[...] = pltpu.roll(vv_ref[...], tot - t, 0)[0:w, :]

    lo = lax.broadcasted_iota(jnp.int32, (t, LANES), 1) < HEAD_DIM
    qi = lax.broadcasted_iota(jnp.int32, (rows, tot), 0) % t
    sj = lax.broadcasted_iota(jnp.int32, (rows, tot), 1)
    mask = (sj <= qi + w) & (sj > qi)
    rowid = lax.broadcasted_iota(jnp.int32, (rows, 1), 0) // t
    for p in range(pairs):
        ksl = slice(p * LANES, (p + 1) * LANES)
        sink_col = jnp.zeros((rows, 1), F32)
        for j in range(cpp):
            c = p * cpp + j
            qc = q_ref[:, c * LANES:(c + 1) * LANES]
            qs_ref[(2 * j) * t:(2 * j + 1) * t, :] = jnp.where(lo, qc, 0.0)
            qs_ref[(2 * j + 1) * t:(2 * j + 2) * t, :] = jnp.where(lo, 0.0, qc)
            sink_col = jnp.where(rowid == 2 * j, sink_ref[l, 2 * c], sink_col)
            sink_col = jnp.where(rowid == 2 * j + 1, sink_ref[l, 2 * c + 1], sink_col)
        s = lax.dot_general(qs_ref[...].astype(BF16), kk_ref[:, ksl].astype(BF16),
                            (((1,), (1,)), ((), ())), preferred_element_type=F32) * scale
        pr = _sink_softmax(s, mask, sink_col)
        qs_ref[...] = jnp.dot(pr.astype(BF16), vv_ref[:, ksl].astype(BF16), preferred_element_type=F32)
        for j in range(cpp):
            c = p * cpp + j
            o_ref[:, c * LANES:(c + 1) * LANES] = jnp.where(
                lo, qs_ref[(2 * j) * t:(2 * j + 1) * t, :], qs_ref[(2 * j + 1) * t:(2 * j + 2) * t, :]).astype(BF16)


def _attn_sample(sinks, q, kn, vn, cache_k, cache_v, l):
    nseq, t, adim = q.shape
    w, kvd = cache_k.shape[2], cache_k.shape[3]
    tot = -(-(w + t) // 8) * 8
    rows = 2 * (adim // kvd) * t
    seq3 = lambda b: (b, 0, 0)
    cache = lambda b: (l, b, 0, 0)
    return pl.pallas_call(
        functools.partial(_attn_sample_kernel, l=l, scale=HEAD_DIM ** -0.5),
        grid=(nseq,),
        in_specs=[pl.BlockSpec(memory_space=pltpu.SMEM),
                  pl.BlockSpec((None, t, adim), seq3),
                  pl.BlockSpec((None, t, kvd), seq3), pl.BlockSpec((None, t, kvd), seq3),
                  pl.BlockSpec((None, None, w, kvd), cache), pl.BlockSpec((None, None, w, kvd), cache)],
        out_specs=[pl.BlockSpec((None, t, adim), seq3),
                   pl.BlockSpec((None, w, kvd), seq3), pl.BlockSpec((None, w, kvd), seq3)],
        out_shape=[jax.ShapeDtypeStruct((nseq, t, adim), BF16),
                   jax.ShapeDtypeStruct((nseq, w, kvd), F32),
                   jax.ShapeDtypeStruct((nseq, w, kvd), F32)],
        scratch_shapes=[pltpu.VMEM((tot, kvd), F32), pltpu.VMEM((tot, kvd), F32),
                        pltpu.VMEM((rows, LANES), F32)],
        compiler_params=_params("arbitrary"),
        name="attn_sample",
    )(sinks, q, kn, vn, cache_k, cache_v)


def _layer_norm(z, g, b):
    mu = jnp.mean(z, axis=-1, keepdims=True)
    d = z - mu
    var = jnp.mean(d * d, axis=-1, keepdims=True)
    return d * lax.rsqrt(var + LN_EPS) * g + b


def _top2(sg, lane, n):
    m1 = jnp.max(sg, axis=-1, keepdims=True)
    i1 = jnp.min(jnp.where(sg == m1, lane, n), axis=-1, keepdims=True)
    rest = jnp.where(lane == i1, -jnp.inf, sg)
    m2 = jnp.max(rest, axis=-1, keepdims=True)
    i2 = jnp.min(jnp.where(rest == m2, lane, n), axis=-1, keepdims=True)
    return m1, i1, m2, i2


def _mix_out_kernel(conv_ref, attn_ref, x_ref, wo_ref, g_ref, b_ref, wr_ref, br_ref, cnt_in,
                    x1_ref, route_ref, cnt_ref, carry, wob, wrb, *, alpha):
    cdim = conv_ref.shape[1]
    ne = wr_ref.shape[1]
    sub = min(x_ref.shape[0], MIX_SUB_ROWS)

    @pl.when(pl.program_id(0) == 0)
    def _():
        carry[...] = cnt_in[...]
        wrb[...] = wr_ref[...].astype(BF16)
        wob[0:cdim, :] = wo_ref[0:cdim, :].astype(BF16)
        order = _paired_head_order((wo_ref.shape[0] - cdim) // HEAD_DIM)
        for pos, head in enumerate(order):
            wob[cdim + pos * HEAD_DIM:cdim + (pos + 1) * HEAD_DIM, :] = (
                wo_ref[cdim + head * HEAD_DIM:cdim + (head + 1) * HEAD_DIM, :].astype(BF16))

    counts = carry[0:1, 0:ne]
    for r0 in range(0, x_ref.shape[0], sub):
        rows = slice(r0, r0 + sub)
        y = jnp.dot(conv_ref[rows, :], wob[0:cdim, :], preferred_element_type=F32)
        y = y + jnp.dot(attn_ref[rows, :], wob[cdim:, :], preferred_element_type=F32)
        x1 = _layer_norm(alpha * x_ref[rows, :] + y, g_ref[...], b_ref[...])
        x1_ref[rows, :] = x1
        logits = jnp.dot(x1.astype(BF16), wrb[...], preferred_element_type=F32)
        route, counts = _route(logits, br_ref[...], counts)
        route_ref[rows, :] = route
    carry[0:1, 0:ne] = counts
    cnt_ref[...] = carry[...]


def _route(logits, bias, counts):
    tm, ne = logits.shape
    epg = ne // N_EXPERT_GROUPS
    scores = jax.nn.sigmoid(logits)
    sel = scores + bias
    lane = lax.broadcasted_iota(jnp.int32, (tm, ne), 1).astype(F32)
    grp = jnp.floor(lane * (1.0 / epg))
    best_s = None
    best_g = None
    for g in range(N_EXPERT_GROUPS):
        m1, _, m2, _ = _top2(jnp.where(grp == g, sel, -jnp.inf), lane, float(ne))
        gs = m1 + m2
        if g == 0:
            best_s, best_g = gs, jnp.zeros_like(gs)
        else:
            upd = gs > best_s
            best_g = jnp.where(upd, float(g), best_g)
            best_s = jnp.where(upd, gs, best_s)
    _, e1, _, e2 = _top2(jnp.where(grp == best_g, sel, -jnp.inf), lane, float(ne))
    hit1 = lane == e1
    hit2 = lane == e2
    w1 = jnp.sum(jnp.where(hit1, scores, 0.0), axis=-1, keepdims=True)
    w2 = jnp.sum(jnp.where(hit2, scores, 0.0), axis=-1, keepdims=True)
    wsum = w1 + w2

    onehot = jnp.where(hit1 | hit2, 1.0, 0.0)
    tri = (lax.broadcasted_iota(jnp.int32, (tm, tm), 0) > lax.broadcasted_iota(jnp.int32, (tm, tm), 1))
    before = jnp.dot(jnp.where(tri, 1.0, 0.0).astype(BF16), onehot.astype(BF16), preferred_element_type=F32)
    before = before + counts
    r1 = jnp.sum(jnp.where(hit1, before, 0.0), axis=-1, keepdims=True)
    r2 = jnp.sum(jnp.where(hit2, before, 0.0), axis=-1, keepdims=True)

    col = lax.broadcasted_iota(jnp.int32, (tm, LANES), 1)
    out = jnp.zeros((tm, LANES), F32)
    for c, val in enumerate((e1, e2, w1 / wsum, w2 / wsum, r1, r2)):
        out = jnp.where(col == c, val, out)
    return out, counts + jnp.sum(onehot, axis=0, keepdims=True)


def _mix_out(conv, attn, x, wo, ln_g, ln_b, wr, br, cnt_in, l, alpha, tm, total_rows, row_block0, prev=None):
    m, d = x.shape
    cdim = conv.shape[1]
    ne = wr.shape[1]
    rowi = lambda i: (i, 0)
    rowo = lambda i: (i + row_block0, 0)
    const = lambda i: (0, 0)
    lay = lambda i: (l, 0, 0)
    in_specs = [pl.BlockSpec((tm, cdim), rowi), pl.BlockSpec((tm, d - cdim), rowi), pl.BlockSpec((tm, d), rowi),
                pl.BlockSpec((None, d, d), lay, pipeline_mode=pl.Buffered(1)),
                pl.BlockSpec((None, 1, d), lay), pl.BlockSpec((None, 1, d), lay),
                pl.BlockSpec((d, ne), const), pl.BlockSpec((1, ne), const),
                pl.BlockSpec((8, LANES), const)]
    args = [conv, attn, x, wo, ln_g, ln_b, wr, br, cnt_in]
    aliases = {}
    if prev is not None:
        in_specs += [pl.BlockSpec(memory_space=pl.ANY), pl.BlockSpec(memory_space=pl.ANY)]
        args += list(prev)
        aliases = {len(args) - 2: 0, len(args) - 1: 1}

    def body(*refs):
        _mix_out_kernel(*refs[:9], *refs[-6:], alpha=alpha)

    return pl.pallas_call(
        body,
        grid=(m // tm,),
        in_specs=in_specs,
        out_specs=[pl.BlockSpec((tm, d), rowo), pl.BlockSpec((tm, LANES), rowo), pl.BlockSpec((8, LANES), const)],
        out_shape=[jax.ShapeDtypeStruct((total_rows, d), F32),
                   jax.ShapeDtypeStruct((total_rows, LANES), F32),
                   jax.ShapeDtypeStruct((8, LANES), F32)],
        scratch_shapes=[pltpu.VMEM((8, LANES), F32), pltpu.VMEM((d, d), BF16), pltpu.VMEM((d, ne), BF16)],
        input_output_aliases=aliases,
        compiler_params=_params("arbitrary"),
        name="mix_out_sample" if prev is not None else "mix_out_prompt",
    )(*args)


def _scatter_rows_kernel(slot_ref, x_ref, o_ref, sem, *, n_tok):
    ct = x_ref.shape[0]
    c = pl.program_id(0)
    last = pl.num_programs(0) - 1
    tail = n_tok - (pl.cdiv(n_tok, ct) - 1) * ct

    def run(count):
        def issue(jb, carry):
            j0 = pl.multiple_of(jb * ISSUE_UNROLL, ISSUE_UNROLL)
            for u in range(ISSUE_UNROLL):
                for k in range(TOP_K):
                    pltpu.make_async_copy(x_ref.at[pl.ds(j0 + u, 1)],
                                          o_ref.at[pl.ds(slot_ref[TOP_K * (j0 + u) + k], 1)], sem).start()
            return carry

        lax.fori_loop(0, count // ISSUE_UNROLL, issue, 0)
        for _ in range(TOP_K):
            pltpu.make_async_copy(x_ref.at[pl.ds(0, count)], x_ref.at[pl.ds(0, count)], sem).wait()

    if tail == ct:
        run(ct)
    else:
        pl.when(c < last)(lambda: run(ct))
        pl.when(c == last)(lambda: run(tail))


def _scatter_rows(x1, slot_flat, n_slots):
    total, d = x1.shape
    ct = SLOT_BLOCK // TOP_K
    return pl.pallas_call(
        functools.partial(_scatter_rows_kernel, n_tok=total),
        grid=(pl.cdiv(total, ct),),
        in_specs=[pl.BlockSpec((SLOT_BLOCK,), lambda c: (c,), memory_space=pltpu.SMEM),
                  pl.BlockSpec((ct, d), lambda c: (c, 0))],
        out_specs=pl.BlockSpec(memory_space=pl.ANY),
        out_shape=jax.ShapeDtypeStruct((n_slots, d), x1.dtype),
        scratch_shapes=[pltpu.SemaphoreType.DMA(())],
        compiler_params=_params("arbitrary"),
        name="scatter_rows",
    )(slot_flat, x1)


def _expert_kernel(te_ref, nv_ref, nu_ref, x_ref, *refs):
    nf, _, fc = refs[-1].shape
    wg_refs, wu_refs = refs[:W_SPLIT], refs[W_SPLIT:2 * W_SPLIT]
    wd_refs = refs[2 * W_SPLIT:2 * W_SPLIT + nf]
    o_ref, xb_ref, wgu_ref, h_ref = refs[2 * W_SPLIT + nf:]
    i = pl.program_id(0)
    s = pl.program_id(1)

    @pl.when(i < nu_ref[0])
    def _():
        @pl.when(s == 0)
        def _():
            rid = lax.broadcasted_iota(jnp.int32, (x_ref.shape[0], 1), 0)
            xb_ref[...] = jnp.where(rid < nv_ref[i], x_ref[...], 0.0).astype(BF16)

        @pl.when(s < nf)
        def _():
            wc = fc // W_SPLIT
            for k in range(W_SPLIT):
                wgu_ref[:, k * wc:(k + 1) * wc] = wg_refs[k][...].astype(BF16)
                wgu_ref[:, fc + k * wc:fc + (k + 1) * wc] = wu_refs[k][...].astype(BF16)
            gu = jnp.dot(xb_ref[...], wgu_ref[...], preferred_element_type=F32)
            h_ref[s] = (jax.nn.silu(gu[:, :fc]) * gu[:, fc:]).astype(BF16)

        @pl.when(s >= nf)
        def _():
            y = jnp.dot(h_ref[0], wd_refs[0][...].astype(BF16), preferred_element_type=F32)
            for f in range(1, nf):
                y = y + jnp.dot(h_ref[f], wd_refs[f][...].astype(BF16), preferred_element_type=F32)
            o_ref[...] = y


def _expert_ffn(xs, w_gate, w_up, w_down, tile_e, tile_nv, n_used, l):
    rows, d = xs.shape
    dff = w_gate.shape[3]
    nf = dff // FF_CHUNK
    nc = d // OUT_CHUNK
    steps = nf + nc
    nt = rows // EXPERT_TILE

    def item(i, nu):
        return jnp.minimum(i, jnp.maximum(nu[0] - 1, 0))

    def step(i, s, nu):
        return jnp.where(i < nu[0], s, steps - 1)

    def gmap(k):
        def index(i, s, te, nv, nu):
            s = step(i, s, nu)
            e = jnp.where(s < nf, te[item(i, nu)], te[item(i + 1, nu)])
            return (l, e, 0, jnp.where(s < nf, s, 0) * W_SPLIT + k)
        return index

    col = lambda i, s, nu: jnp.maximum(step(i, s, nu) - nf, 0)
    xmap = lambda i, s, te, nv, nu: (item(i, nu), 0)
    dmap = lambda f: (lambda i, s, te, nv, nu: (l, te[item(i, nu)], f, col(i, s, nu)))
    omap = lambda i, s, te, nv, nu: (item(i, nu), col(i, s, nu))
    gate_up = [pl.BlockSpec((None, None, d, FF_CHUNK // W_SPLIT), gmap(k)) for k in range(W_SPLIT)]
    return pl.pallas_call(
        _expert_kernel,
        grid_spec=pltpu.PrefetchScalarGridSpec(
            num_scalar_prefetch=3,
            grid=(nt, steps),
            in_specs=[pl.BlockSpec((EXPERT_TILE, d), xmap)] + gate_up + gate_up
            + [pl.BlockSpec((None, None, FF_CHUNK, OUT_CHUNK), dmap(f)) for f in range(nf)],
            out_specs=pl.BlockSpec((EXPERT_TILE, OUT_CHUNK), omap),
            scratch_shapes=[pltpu.VMEM((EXPERT_TILE, d), BF16),
                            pltpu.VMEM((d, 2 * FF_CHUNK), BF16),
                            pltpu.VMEM((nf, EXPERT_TILE, FF_CHUNK), BF16)]),
        out_shape=jax.ShapeDtypeStruct((rows, d), F32),
        compiler_params=_params("arbitrary", "arbitrary"),
        name="expert_ffn",
    )(tile_e, tile_nv, n_used, xs, *([w_gate] * W_SPLIT), *([w_up] * W_SPLIT), *([w_down] * nf))


def _combine_kernel(slot_cur, slot_nxt, x1_ref, r_ref, g_ref, b_ref, y_hbm, o_ref, ybuf, sems, *, alpha):
    i = pl.program_id(0)
    ct = x1_ref.shape[0]

    def fetch(slot_ref, buf):
        def issue(jb, carry):
            j0 = pl.multiple_of(jb * ISSUE_UNROLL, ISSUE_UNROLL)
            for u in range(ISSUE_UNROLL):
                for k in range(TOP_K):
                    pltpu.make_async_copy(y_hbm.at[pl.ds(slot_ref[TOP_K * (j0 + u) + k], 1)],
                                          ybuf.at[buf, k, pl.ds(j0 + u, 1)], sems.at[buf]).start()
            return carry

        lax.fori_loop(0, ct // ISSUE_UNROLL, issue, 0)

    @pl.when(i == 0)
    def _():
        fetch(slot_cur, 0)

    @pl.when(i + 1 < pl.num_programs(0))
    def _():
        fetch(slot_nxt, (i + 1) % 2)

    buf = i % 2
    for k in range(TOP_K):
        pltpu.make_async_copy(ybuf.at[buf, k], ybuf.at[buf, k], sems.at[buf]).wait()
    r = r_ref[...]
    moe = ybuf[buf, 0] * r[:, 2:3] + ybuf[buf, 1] * r[:, 3:4]
    o_ref[...] = _layer_norm(alpha * x1_ref[...] + moe, g_ref[...], b_ref[...])


def _combine(x1, yb, slot_flat, route, ln_g, ln_b, l, alpha, tm, row0, m):
    d = x1.shape[1]
    nsteps = m // tm
    rb0 = row0 // tm
    sb0 = row0 * TOP_K // SLOT_BLOCK
    sper = pl.cdiv(tm * TOP_K, SLOT_BLOCK)
    rowo = lambda i: (i + rb0, 0)
    lay = lambda i: (l, 0, 0)
    return pl.pallas_call(
        functools.partial(_combine_kernel, alpha=alpha),
        grid=(nsteps,),
        in_specs=[pl.BlockSpec((SLOT_BLOCK,), lambda i: (sb0 + i * sper,), memory_space=pltpu.SMEM),
                  pl.BlockSpec((SLOT_BLOCK,), lambda i: (sb0 + jnp.minimum(i + 1, nsteps - 1) * sper,),
                               memory_space=pltpu.SMEM),
                  pl.BlockSpec((tm, d), rowo),
                  pl.BlockSpec((tm, LANES), rowo),
                  pl.BlockSpec((None, 1, d), lay), pl.BlockSpec((None, 1, d), lay),
                  pl.BlockSpec(memory_space=pl.ANY)],
        out_specs=pl.BlockSpec((tm, d), lambda i: (i, 0)),
        out_shape=jax.ShapeDtypeStruct((m, d), F32),
        scratch_shapes=[pltpu.VMEM((2, TOP_K, tm, d), F32), pltpu.SemaphoreType.DMA((2,))],
        compiler_params=_params("arbitrary"),
        name="combine",
    )(slot_flat, slot_flat, x1, route, ln_g, ln_b, yb)


def _rope_tables(pos):
    half = ROT_DIM // 2
    inv = ROPE_THETA ** (-jnp.arange(half, dtype=F32) * 2.0 / ROT_DIM)
    ang = pos.astype(F32)[:, None] * inv[None, :]
    cos, sin = jnp.cos(ang), jnp.sin(ang)
    n = pos.shape[0]
    rest = HEAD_DIM - ROT_DIM
    cs = jnp.concatenate([cos, cos, jnp.ones((n, rest), F32)], axis=1)
    sa = jnp.concatenate([-sin, jnp.zeros((n, half + rest), F32)], axis=1)
    sb = jnp.concatenate([jnp.zeros((n, half), F32), sin, jnp.zeros((n, rest), F32)], axis=1)
    rep = LANES // HEAD_DIM
    return tuple(jnp.tile(t, (1, rep)) for t in (cs, sa, sb))


def kernel(x_prompt, x_sample, state_conv, cache_k, cache_v, w_in, conv_w, attn_sinks, w_o,
           ln1_g, ln1_b, w_router, b_router, w_gate, w_up, w_down, ln2_g, ln2_b):
    batch, seq, d = x_prompt.shape
    nseq, dec_seq, _ = x_sample.shape
    depth = w_in.shape[0]
    cdim = conv_w.shape[2]
    adim = d - cdim
    kvd = N_KV_HEADS * HEAD_DIM
    n_heads = adim // HEAD_DIM
    ne = w_router.shape[1]
    mp, ms = batch * seq, nseq * dec_seq
    total = mp + ms
    assign = total * TOP_K
    alpha = (2 * depth) ** 0.25

    sinks = attn_sinks[:, jnp.asarray(_paired_head_order(n_heads))]
    br = b_router.reshape(1, ne)

    tabs_p = _rope_tables(jnp.arange(seq))
    tabs_s = _rope_tables(jnp.tile(PAST_LEN + jnp.arange(dec_seq), nseq))
    cache_k2 = cache_k.reshape(depth, nseq, WINDOW, kvd)
    cache_v2 = cache_v.reshape(depth, nseq, WINDOW, kvd)
    ln1g, ln1b = ln1_g.reshape(depth, 1, d), ln1_b.reshape(depth, 1, d)
    ln2g, ln2b = ln2_g.reshape(depth, 1, d), ln2_b.reshape(depth, 1, d)

    n_items = pl.cdiv(assign, EXPERT_TILE) + ne
    slot_len = pl.cdiv(assign, SLOT_BLOCK) * SLOT_BLOCK
    items = jnp.arange(n_items, dtype=jnp.int32)

    xp = x_prompt.reshape(mp, d)
    xs_ = x_sample.reshape(ms, d)
    outs = [[] for _ in range(6)]
    for l in range(depth):
        conv_p, cst_p = _conv_proj(xp, w_in, conv_w, l, 512, cdim, seq=seq, batch=batch)
        q_p, k_p, v_p, kw_p, vw_p = _qkv_proj(xp, w_in, l, tabs_p, 512, cdim, adim, kvd, seq=seq, batch=batch)
        attn_p = _attn_prompt(sinks, q_p, k_p, v_p, l, batch, seq)

        st = state_conv[l]
        st1 = jnp.repeat(st[:, 1], dec_seq, axis=0)
        st2 = jnp.stack([st[:, 0], st[:, 1]] + [st[:, 1]] * (dec_seq - 2), axis=1).reshape(ms, cdim)
        conv_s, inner_s = _conv_proj(xs_, w_in, conv_w, l, ms, cdim, state=(st1, st2), dec_seq=dec_seq)
        q_s, kn_s, vn_s = _qkv_proj(xs_, w_in, l, tabs_s, ms, cdim, adim, kvd)
        attn_s, kw_s, vw_s = _attn_sample(
            sinks, q_s.reshape(nseq, dec_seq, adim), kn_s.reshape(nseq, dec_seq, kvd),
            vn_s.reshape(nseq, dec_seq, kvd), cache_k2, cache_v2, l)

        zeros_cnt = jnp.zeros((8, LANES), F32)
        x1, route, cnt = _mix_out(conv_p, attn_p, xp, w_o, ln1g, ln1b, w_router, br, zeros_cnt,
                                  l, alpha, 256, total, 0)
        x1, route, cnt = _mix_out(conv_s, attn_s.reshape(ms, adim), xs_, w_o, ln1g, ln1b, w_router, br, cnt,
                                  l, alpha, ms, total, mp // ms, prev=(x1, route))

        e = route[:, 0:TOP_K].astype(jnp.int32)
        rank = route[:, 4:4 + TOP_K].astype(jnp.int32)
        counts = cnt[0, :ne].astype(jnp.int32)
        tiles_e = (counts + EXPERT_TILE - 1) // EXPERT_TILE
        tile_end = jnp.cumsum(tiles_e)
        tile_start = tile_end - tiles_e
        onehot = e[:, :, None] == jnp.arange(ne)[None, None, :]
        pos = jnp.sum(jnp.where(onehot, (tile_start * EXPERT_TILE)[None, None, :], 0), axis=-1) + rank
        slot_flat = jnp.pad(pos.reshape(-1), (0, slot_len - assign))
        n_used = tile_end[-1:].astype(jnp.int32)
        item_e = jnp.minimum(jnp.sum(tile_end[None, :] <= items[:, None], axis=1), ne - 1).astype(jnp.int32)
        item_nv = jnp.clip(counts[item_e] - (items - tile_start[item_e]) * EXPERT_TILE, 0, EXPERT_TILE)

        xsorted = _scatter_rows(x1, slot_flat, n_items * EXPERT_TILE)
        yb = _expert_ffn(xsorted, w_gate, w_up, w_down, item_e, item_nv.astype(jnp.int32), n_used, l)

        xp = _combine(x1, yb, slot_flat, route, ln2g, ln2b, l, alpha, 512, 0, mp)
        xs_ = _combine(x1, yb, slot_flat, route, ln2g, ln2b, l, alpha, ms, mp, ms)

        inner3 = inner_s.reshape(nseq, dec_seq, cdim)
        for lst, val in zip(outs, (cst_p, kw_p.reshape(batch, WINDOW, N_KV_HEADS, HEAD_DIM),
                                   vw_p.reshape(batch, WINDOW, N_KV_HEADS, HEAD_DIM),
                                   inner3[:, dec_seq - 2:],
                                   kw_s.reshape(nseq, WINDOW, N_KV_HEADS, HEAD_DIM),
                                   vw_s.reshape(nseq, WINDOW, N_KV_HEADS, HEAD_DIM))):
            lst.append(val)

    return (xp.reshape(batch, seq, d), xs_.reshape(nseq, dec_seq, d), *[jnp.stack(o) for o in outs])
```

```python
import functools
import math

import jax
import jax.numpy as jnp
from jax import lax
from jax.experimental import pallas as pl
from jax.experimental.pallas import tpu as pltpu

F32 = jnp.float32
BF16 = jnp.bfloat16

HEAD_DIM = 64
N_KV_HEADS = 4
ROT_DIM = HEAD_DIM // 4
ROPE_THETA = 500000.0
WINDOW = 128
PAST_LEN = 16384
N_EXPERT_GROUPS = 4
TOP_K = 2
LN_EPS = 1e-5
QK_SCALE = HEAD_DIM ** -0.5
assert math.frexp(QK_SCALE)[0] == 0.5

LANES = 128
VMEM_LIMIT = 56 * 1024 * 1024

EXPERT_TILE = 640
FF_CHUNK = 512
OUT_CHUNK = 1024
W_SPLIT = 2
CONV_GROUP = 512
ROUTE_ROWS = 8
SLOT_BLOCK = 1024
ISSUE_UNROLL = 8
MIX_SUB_ROWS = 256


def _params(*sem):
    return pltpu.CompilerParams(dimension_semantics=sem, vmem_limit_bytes=VMEM_LIMIT)


def _rope_chunk(xc, cs, sa, sb):
    return xc * cs + pltpu.roll(xc, LANES - ROT_DIM // 2, 1) * sa + pltpu.roll(xc, ROT_DIM // 2, 1) * sb


def _conv(inner, gate, w_ref, prev1, prev2):
    return gate * (w_ref[0:1, :] * prev2 + w_ref[1:2, :] * prev1 + w_ref[2:3, :] * inner)


def _paired_head_order(n_heads):
    group = n_heads // N_KV_HEADS
    return [(2 * pair + half) * group + j
            for pair in range(N_KV_HEADS // 2) for j in range(group) for half in range(2)]


def _pair_q_chunks(chunks):
    order = _paired_head_order(2 * len(chunks))
    lo = lax.broadcasted_iota(jnp.int32, chunks[0].shape, 1) < HEAD_DIM
    out = []
    for c in range(len(chunks)):
        halves = []
        for half, head in enumerate(order[2 * c:2 * c + 2]):
            src = chunks[head // 2]
            halves.append(src if head % 2 == half else pltpu.roll(src, HEAD_DIM, 1))
        out.append(jnp.where(lo, halves[0], halves[1]))
    return out


def _conv_proj_kernel(x_ref, wc_ref, wb_ref, wu_ref, cw_ref, *refs, tiles_per_seq, dec_seq):
    if dec_seq is None:
        conv_o, cst_o, wbf, carry = refs
    else:
        st1, st2, conv_o, inner_o, wbf = refs
    i = pl.program_id(1)

    @pl.when(i == 0)
    def _():
        for k, w in enumerate((wc_ref, wb_ref, wu_ref)):
            wbf[k] = w[...].astype(BF16)

    xb = x_ref[...].astype(BF16)
    c, gate, u = (jnp.dot(xb, wbf[k], preferred_element_type=F32) for k in range(3))
    inner = c * u
    tm, cw = inner.shape
    row = lax.broadcasted_iota(jnp.int32, (tm, cw), 0)
    if dec_seq is None:
        @pl.when(i % tiles_per_seq == 0)
        def _():
            carry[...] = jnp.zeros_like(carry)

        c1, c2 = carry[7:8, :], carry[6:7, :]
        prev1 = jnp.where(row == 0, c1, pltpu.roll(inner, 1, 0))
        prev2 = jnp.where(row == 0, c2, jnp.where(row == 1, c1, pltpu.roll(inner, 2, 0)))
        carry[...] = inner[tm - 8:, :]
        cst_o[...] = inner[tm - 2:, :]
    else:
        t = row % dec_seq
        prev1 = jnp.where(t == 0, st1[...], pltpu.roll(inner, 1, 0))
        prev2 = jnp.where(t < 2, st2[...], pltpu.roll(inner, 2, 0))
        inner_o[...] = inner
    conv_o[...] = _conv(inner, gate, cw_ref, prev1, prev2).astype(BF16)


def _conv_proj(x, w_in, conv_w, l, tm, cdim, seq=None, batch=None, state=None, dec_seq=None):
    m, k = x.shape
    cw = CONV_GROUP
    groups = cdim // cw
    sec = lambda s: pl.BlockSpec((None, k, cw), lambda g, i: (l, 0, s * groups + g))
    blk = pl.BlockSpec((tm, cw), lambda g, i: (i, g))
    in_specs = [pl.BlockSpec((tm, k), lambda g, i: (i, 0)), sec(0), sec(1), sec(2),
                pl.BlockSpec((None, 3, cw), lambda g, i: (l, 0, g))]
    scratch = [pltpu.VMEM((3, k, cw), BF16)]
    if state is None:
        tps = seq // tm
        args = ()
        out_specs = [blk, pl.BlockSpec((None, 2, cw), lambda g, i: (i // tps, 0, g))]
        out_shape = [jax.ShapeDtypeStruct((m, cdim), BF16), jax.ShapeDtypeStruct((batch, 2, cdim), F32)]
        scratch.append(pltpu.VMEM((8, cw), F32))
    else:
        tps = None
        args = state
        in_specs += [blk, blk]
        out_specs = [blk, blk]
        out_shape = [jax.ShapeDtypeStruct((m, cdim), BF16), jax.ShapeDtypeStruct((m, cdim), F32)]
    return pl.pallas_call(
        functools.partial(_conv_proj_kernel, tiles_per_seq=tps, dec_seq=dec_seq),
        grid=(groups, m // tm),
        in_specs=in_specs, out_specs=out_specs, out_shape=out_shape, scratch_shapes=scratch,
        compiler_params=_params("arbitrary", "arbitrary"),
        name="conv_proj_prompt" if state is None else "conv_proj_sample",
    )(x, w_in, w_in, w_in, conv_w, *args)


def _qkv_proj_kernel(x_ref, w_ref, cs_ref, sa_ref, sb_ref, q_o, k_o, v_o, *refs, adim, q_scale):
    wbf = refs[-1]

    @pl.when(pl.program_id(0) == 0)
    def _():
        wbf[...] = w_ref[...].astype(BF16)

    h = jnp.dot(x_ref[...].astype(BF16), wbf[...], preferred_element_type=F32)
    tm = h.shape[0]
    kvd = (h.shape[1] - adim) // 2
    cs, sa, sb = cs_ref[...], sa_ref[...], sb_ref[...]
    roped = [_rope_chunk(h[:, c * LANES:(c + 1) * LANES], cs, sa, sb) for c in range((adim + kvd) // LANES)]
    for c, qc in enumerate(_pair_q_chunks(roped[:adim // LANES])):
        q_o[:, c * LANES:(c + 1) * LANES] = (qc * q_scale).astype(q_o.dtype)
    for c, kc in enumerate(roped[adim // LANES:]):
        k_o[:, c * LANES:(c + 1) * LANES] = kc.astype(k_o.dtype)
    v = h[:, adim + kvd:]
    v_o[...] = v.astype(v_o.dtype)
    if len(refs) > 1:
        kw_o, vw_o = refs[:2]
        for c, kc in enumerate(roped[adim // LANES:]):
            kw_o[:, c * LANES:(c + 1) * LANES] = kc[tm - WINDOW:, :]
        vw_o[...] = v[tm - WINDOW:, :]


def _qkv_proj(x, w_in, l, tabs, tm, cdim, adim, kvd, seq=None, batch=None):
    m, k = x.shape
    n = adim + 2 * kvd
    row = lambda i: (i, 0)
    prompt = seq is not None
    tps = seq // tm if prompt else 1
    tab = pl.BlockSpec((tm, LANES), lambda i: (i % tps, 0))
    dt = BF16 if prompt else F32
    out_specs = [pl.BlockSpec((tm, adim), row), pl.BlockSpec((tm, kvd), row), pl.BlockSpec((tm, kvd), row)]
    out_shape = [jax.ShapeDtypeStruct((m, adim), dt), jax.ShapeDtypeStruct((m, kvd), dt),
                 jax.ShapeDtypeStruct((m, kvd), dt)]
    if prompt:
        win = pl.BlockSpec((None, WINDOW, kvd), lambda i: (i // tps, 0, 0))
        out_specs += [win, win]
        out_shape += [jax.ShapeDtypeStruct((batch, WINDOW, kvd), F32)] * 2
    return pl.pallas_call(
        functools.partial(_qkv_proj_kernel, adim=adim, q_scale=QK_SCALE if prompt else 1.0),
        grid=(m // tm,),
        in_specs=[pl.BlockSpec((tm, k), row),
                  pl.BlockSpec((None, k, n), lambda i: (l, 0, 3 * cdim // n), pipeline_mode=pl.Buffered(1)),
                  tab, tab, tab],
        out_specs=out_specs, out_shape=out_shape,
        scratch_shapes=[pltpu.VMEM((k, n), BF16)],
        compiler_params=_params("arbitrary"),
        name="qkv_proj_prompt" if prompt else "qkv_proj_sample",
    )(x, w_in, *tabs)


def _sink_softmax(s, mask, sink_col):
    s = jnp.where(mask, s, -jnp.inf)
    m = jnp.maximum(jnp.max(s, axis=-1, keepdims=True), sink_col)
    p = jnp.exp(s - m)
    den = jnp.sum(p, axis=-1, keepdims=True) + jnp.exp(sink_col - m)
    return p * (1.0 / den)


def _attn_prompt_kernel(sink_ref, q_ref, kp_ref, kc_ref, vp_ref, vc_ref, o_ref, bias_ref, *, l):
    n = pl.program_id(1)
    w = q_ref.shape[0]
    chunks = q_ref.shape[1] // LANES
    pairs = kc_ref.shape[1] // LANES
    cpp = chunks // pairs
    lo = lax.broadcasted_iota(jnp.int32, (w, LANES), 1) < HEAD_DIM
    qi = lax.broadcasted_iota(jnp.int32, (w, 2 * w), 0)
    sj = lax.broadcasted_iota(jnp.int32, (w, 2 * w), 1)
    mask = (sj <= qi + w) & (sj > qi) & ((n > 0) | (sj >= w))
    bias_ref[...] = jnp.where(mask, 0.0, -jnp.inf)
    zero = jnp.zeros((w, LANES), BF16)
    for p in range(pairs):
        ksl = slice(p * LANES, (p + 1) * LANES)
        kk = jnp.concatenate([kp_ref[:, ksl], kc_ref[:, ksl]], axis=0)
        vv = jnp.concatenate([vp_ref[:, ksl], vc_ref[:, ksl]], axis=0)
        for j in range(cpp):
            c = p * cpp + j
            qc = q_ref[:, c * LANES:(c + 1) * LANES]
            halves = []
            for half, qh in enumerate((jnp.where(lo, qc, zero), jnp.where(lo, zero, qc))):
                sink = sink_ref[l, 2 * c + half]
                s = lax.dot_general(qh, kk, (((1,), (1,)), ((), ())), preferred_element_type=F32) + bias_ref[...]
                m = jnp.maximum(jnp.max(s, axis=-1, keepdims=True), sink)
                pexp = jnp.exp(s - m)
                den = jnp.sum(pexp, axis=-1, keepdims=True) + jnp.exp(sink - m)
                pr = (pexp * (1.0 / den)).astype(BF16)
                halves.append(jnp.dot(pr, vv, preferred_element_type=F32))
            o_ref[:, c * LANES:(c + 1) * LANES] = jnp.where(lo, halves[0], halves[1]).astype(BF16)


def _attn_prompt(sinks, q, k, v, l, batch, seq):
    m, adim = q.shape
    kvd = k.shape[1]
    nb = seq // WINDOW
    cur = lambda b, n: (b * nb + n, 0)
    prev = lambda b, n: (b * nb + jnp.maximum(n - 1, 0), 0)
    return pl.pallas_call(
        functools.partial(_attn_prompt_kernel, l=l),
        grid=(batch, nb),
        in_specs=[pl.BlockSpec(memory_space=pltpu.SMEM),
                  pl.BlockSpec((WINDOW, adim), cur),
                  pl.BlockSpec((WINDOW, kvd), prev), pl.BlockSpec((WINDOW, kvd), cur),
                  pl.BlockSpec((WINDOW, kvd), prev), pl.BlockSpec((WINDOW, kvd), cur)],
        out_specs=pl.BlockSpec((WINDOW, adim), cur),
        out_shape=jax.ShapeDtypeStruct((m, adim), BF16),
        scratch_shapes=[pltpu.VMEM((WINDOW, 2 * WINDOW), F32)],
        compiler_params=_params("arbitrary", "arbitrary"),
        name="attn_prompt",
    )(sinks, q, k, k, v, v)


def _attn_sample_kernel(sink_ref, q_ref, kn_ref, vn_ref, ck_ref, cv_ref,
                        o_ref, kw_ref, vw_ref, kk_ref, vv_ref, qs_ref, *, l, scale):
    t = q_ref.shape[0]
    w = ck_ref.shape[0]
    tot = kk_ref.shape[0]
    chunks = q_ref.shape[1] // LANES
    pairs = ck_ref.shape[1] // LANES
    cpp = chunks // pairs
    rows = 2 * cpp * t
    kk_ref[0:w, :] = ck_ref[...]
    vv_ref[0:w, :] = cv_ref[...]
    pad = jnp.zeros((tot - w - t, kk_ref.shape[1]), F32)
    kk_ref[w:w + t, :] = kn_ref[...]
    vv_ref[w:w + t, :] = vn_ref[...]
    kk_ref[w + t:, :] = pad
    vv_ref[w + t:, :] = pad
    kw_ref[...] = pltpu.roll(kk_ref[...], tot - t, 0)[0:w, :]
    vw_ref[...] = pltpu.roll(vv_ref[...], tot - t, 0)[0:w, :]

    lo = lax.broadcasted_iota(jnp.int32, (t, LANES), 1) < HEAD_DIM
    qi = lax.broadcasted_iota(jnp.int32, (rows, tot), 0) % t
    sj = lax.broadcasted_iota(jnp.int32, (rows, tot), 1)
    mask = (sj <= qi + w) & (sj > qi)
    rowid = lax.broadcasted_iota(jnp.int32, (rows, 1), 0) // t
    for p in range(pairs):
        ksl = slice(p * LANES, (p + 1) * LANES)
        sink_col = jnp.zeros((rows, 1), F32)
        for j in range(cpp):
            c = p * cpp + j
            qc = q_ref[:, c * LANES:(c + 1) * LANES]
            qs_ref[(2 * j) * t:(2 * j + 1) * t, :] = jnp.where(lo, qc, 0.0)
            qs_ref[(2 * j + 1) * t:(2 * j + 2) * t, :] = jnp.where(lo, 0.0, qc)
            sink_col = jnp.where(rowid == 2 * j, sink_ref[l, 2 * c], sink_col)
            sink_col = jnp.where(rowid == 2 * j + 1, sink_ref[l, 2 * c + 1], sink_col)
        s = lax.dot_general(qs_ref[...].astype(BF16), kk_ref[:, ksl].astype(BF16),
                            (((1,), (1,)), ((), ())), preferred_element_type=F32) * scale
        pr = _sink_softmax(s, mask, sink_col)
        qs_ref[...] = jnp.dot(pr.astype(BF16), vv_ref[:, ksl].astype(BF16), preferred_element_type=F32)
        for j in range(cpp):
            c = p * cpp + j
            o_ref[:, c * LANES:(c + 1) * LANES] = jnp.where(
                lo, qs_ref[(2 * j) * t:(2 * j + 1) * t, :], qs_ref[(2 * j + 1) * t:(2 * j + 2) * t, :]).astype(BF16)


def _attn_sample(sinks, q, kn, vn, cache_k, cache_v, l):
    nseq, t, adim = q.shape
    w, kvd = cache_k.shape[2], cache_k.shape[3]
    tot = -(-(w + t) // 8) * 8
    rows = 2 * (adim // kvd) * t
    seq3 = lambda b: (b, 0, 0)
    cache = lambda b: (l, b, 0, 0)
    return pl.pallas_call(
        functools.partial(_attn_sample_kernel, l=l, scale=HEAD_DIM ** -0.5),
        grid=(nseq,),
        in_specs=[pl.BlockSpec(memory_space=pltpu.SMEM),
                  pl.BlockSpec((None, t, adim), seq3),
                  pl.BlockSpec((None, t, kvd), seq3), pl.BlockSpec((None, t, kvd), seq3),
                  pl.BlockSpec((None, None, w, kvd), cache), pl.BlockSpec((None, None, w, kvd), cache)],
        out_specs=[pl.BlockSpec((None, t, adim), seq3),
                   pl.BlockSpec((None, w, kvd), seq3), pl.BlockSpec((None, w, kvd), seq3)],
        out_shape=[jax.ShapeDtypeStruct((nseq, t, adim), BF16),
                   jax.ShapeDtypeStruct((nseq, w, kvd), F32),
                   jax.ShapeDtypeStruct((nseq, w, kvd), F32)],
        scratch_shapes=[pltpu.VMEM((tot, kvd), F32), pltpu.VMEM((tot, kvd), F32),
                        pltpu.VMEM((rows, LANES), F32)],
        compiler_params=_params("arbitrary"),
        name="attn_sample",
    )(sinks, q, kn, vn, cache_k, cache_v)


def _layer_norm(z, g, b):
    mu = jnp.mean(z, axis=-1, keepdims=True)
    d = z - mu
    var = jnp.mean(d * d, axis=-1, keepdims=True)
    return d * lax.rsqrt(var + LN_EPS) * g + b


def _top2(sg, idx, n):
    m1 = jnp.max(sg, axis=0, keepdims=True)
    i1 = jnp.min(jnp.where(sg == m1, idx, n), axis=0, keepdims=True)
    rest = jnp.where(idx == i1, -jnp.inf, sg)
    m2 = jnp.max(rest, axis=0, keepdims=True)
    i2 = jnp.min(jnp.where(rest == m2, idx, n), axis=0, keepdims=True)
    return m1, i1, m2, i2


def _pack_bf16_pairs(x):
    half = x.shape[1] // 2
    lo = pltpu.bitcast(x[:, :half].astype(BF16).astype(F32), jnp.uint32)
    hi = pltpu.bitcast(x[:, half:].astype(BF16).astype(F32), jnp.uint32)
    return hi | (lo >> 16)


def _unpack_bf16_pairs(w):
    lo = pltpu.bitcast(w << 16, F32).astype(BF16)
    hi = pltpu.bitcast(w & jnp.uint32(0xFFFF0000), F32).astype(BF16)
    return lo, hi


def _mix_out_kernel(conv_ref, attn_ref, x_ref, wo_ref, g_ref, b_ref, wr_ref, br_ref, cnt_in,
                    x1_ref, x1p_ref, route_ref, cnt_ref, carry, wob, wrb, *, alpha):
    cdim = conv_ref.shape[1]
    sub = min(x_ref.shape[0], MIX_SUB_ROWS)

    @pl.when(pl.program_id(0) == 0)
    def _():
        carry[...] = cnt_in[...]
        wrb[...] = wr_ref[...].astype(BF16)
        wob[0:cdim, :] = wo_ref[0:cdim, :].astype(BF16)
        order = _paired_head_order((wo_ref.shape[0] - cdim) // HEAD_DIM)
        for pos, head in enumerate(order):
            wob[cdim + pos * HEAD_DIM:cdim + (pos + 1) * HEAD_DIM, :] = (
                wo_ref[cdim + head * HEAD_DIM:cdim + (head + 1) * HEAD_DIM, :].astype(BF16))

    counts = carry[...]
    for r0 in range(0, x_ref.shape[0], sub):
        rows = slice(r0, r0 + sub)
        y = jnp.dot(conv_ref[rows, :], wob[0:cdim, :], preferred_element_type=F32)
        y = y + jnp.dot(attn_ref[rows, :], wob[cdim:, :], preferred_element_type=F32)
        x1 = _layer_norm(alpha * x_ref[rows, :] + y, g_ref[...], b_ref[...])
        x1_ref[rows, :] = x1
        x1p_ref[rows, :] = _pack_bf16_pairs(x1)
        logits = lax.dot_general(wrb[...], x1.astype(BF16), (((1,), (1,)), ((), ())), preferred_element_type=F32)
        record, counts = _route(logits, br_ref[...], counts)
        route_ref[:, rows] = record
    carry[...] = counts
    cnt_ref[...] = counts


def _route(logits, bias, counts):
    ne, tm = logits.shape
    epg = ne // N_EXPERT_GROUPS
    lanes = lambda a: jnp.concatenate([a] * (tm // LANES), axis=1)
    scores = jax.nn.sigmoid(logits)
    sel = scores + lanes(bias)
    eidx = lax.broadcasted_iota(jnp.int32, (ne, tm), 0).astype(F32)
    best = None
    for g in range(N_EXPERT_GROUPS):
        gidx = lax.broadcasted_iota(jnp.int32, (epg, tm), 0).astype(F32) + float(g * epg)
        m1, i1, m2, i2 = _top2(sel[g * epg:(g + 1) * epg], gidx, float(ne))
        cand = (m1 + m2, i1, i2)
        if best is None:
            best = cand
        else:
            upd = cand[0] > best[0]
            best = tuple(jnp.where(upd, new, old) for new, old in zip(cand, best))
    _, e1, e2 = best
    hit1 = eidx == e1
    hit2 = eidx == e2
    w1 = jnp.sum(jnp.where(hit1, scores, 0.0), axis=0, keepdims=True)
    w2 = jnp.sum(jnp.where(hit2, scores, 0.0), axis=0, keepdims=True)
    wsum = w1 + w2

    onehot = jnp.where(hit1 | hit2, 1.0, 0.0)
    earlier = (lax.broadcasted_iota(jnp.int32, (tm, tm), 0) < lax.broadcasted_iota(jnp.int32, (tm, tm), 1))
    before = jnp.dot(onehot.astype(BF16), jnp.where(earlier, 1.0, 0.0).astype(BF16), preferred_element_type=F32)
    before = before + lanes(counts)
    r1 = jnp.sum(jnp.where(hit1, before, 0.0), axis=0, keepdims=True)
    r2 = jnp.sum(jnp.where(hit2, before, 0.0), axis=0, keepdims=True)

    row = lax.broadcasted_iota(jnp.int32, (ROUTE_ROWS, tm), 0)
    out = jnp.zeros((ROUTE_ROWS, tm), F32)
    for c, val in enumerate((e1, e2, w1 / wsum, w2 / wsum, r1, r2)):
        out = jnp.where(row == c, val, out)
    return out, counts + jnp.sum(onehot, axis=1, keepdims=True)


def _mix_out(conv, attn, x, wo, ln_g, ln_b, wr_t, br, cnt_in, l, alpha, tm, total_rows, row_block0, prev=None):
    m, d = x.shape
    cdim = conv.shape[1]
    ne = wr_t.shape[0]
    rowi = lambda i: (i, 0)
    rowo = lambda i: (i + row_block0, 0)
    const = lambda i: (0, 0)
    lay = lambda i: (l, 0, 0)
    in_specs = [pl.BlockSpec((tm, cdim), rowi), pl.BlockSpec((tm, d - cdim), rowi), pl.BlockSpec((tm, d), rowi),
                pl.BlockSpec((None, d, d), lay, pipeline_mode=pl.Buffered(1)),
                pl.BlockSpec((None, 1, d), lay), pl.BlockSpec((None, 1, d), lay),
                pl.BlockSpec((ne, d), const), pl.BlockSpec((ne, LANES), const),
                pl.BlockSpec((ne, LANES), const)]
    args = [conv, attn, x, wo, ln_g, ln_b, wr_t, br, cnt_in]
    aliases = {}
    if prev is not None:
        in_specs += [pl.BlockSpec(memory_space=pl.ANY)] * len(prev)
        aliases = {len(args) + k: k for k in range(len(prev))}
        args += list(prev)

    def body(*refs):
        _mix_out_kernel(*refs[:9], *refs[-7:], alpha=alpha)

    return pl.pallas_call(
        body,
        grid=(m // tm,),
        in_specs=in_specs,
        out_specs=[pl.BlockSpec((tm, d), rowo), pl.BlockSpec((tm, d // 2), rowo),
                   pl.BlockSpec((ROUTE_ROWS, tm), lambda i: (0, i + row_block0)),
                   pl.BlockSpec((ne, LANES), const)],
        out_shape=[jax.ShapeDtypeStruct((total_rows, d), F32),
                   jax.ShapeDtypeStruct((total_rows, d // 2), jnp.uint32),
                   jax.ShapeDtypeStruct((ROUTE_ROWS, total_rows), F32),
                   jax.ShapeDtypeStruct((ne, LANES), F32)],
        scratch_shapes=[pltpu.VMEM((ne, LANES), F32), pltpu.VMEM((d, d), BF16), pltpu.VMEM((ne, d), BF16)],
        input_output_aliases=aliases,
        compiler_params=_params("arbitrary"),
        name="mix_out_sample" if prev is not None else "mix_out_prompt",
    )(*args)


def _scatter_rows_kernel(slot_ref, x_ref, o_ref, sem, *, n_tok):
    ct = x_ref.shape[0]
    c = pl.program_id(0)
    last = pl.num_programs(0) - 1
    tail = n_tok - (pl.cdiv(n_tok, ct) - 1) * ct

    def run(count):
        def issue(jb, carry):
            j0 = pl.multiple_of(jb * ISSUE_UNROLL, ISSUE_UNROLL)
            for u in range(ISSUE_UNROLL):
                for k in range(TOP_K):
                    pltpu.make_async_copy(x_ref.at[pl.ds(j0 + u, 1)],
                                          o_ref.at[pl.ds(slot_ref[TOP_K * (j0 + u) + k], 1)], sem).start()
            return carry

        lax.fori_loop(0, count // ISSUE_UNROLL, issue, 0)
        for _ in range(TOP_K):
            pltpu.make_async_copy(x_ref.at[pl.ds(0, count)], x_ref.at[pl.ds(0, count)], sem).wait()

    if tail == ct:
        run(ct)
    else:
        pl.when(c < last)(lambda: run(ct))
        pl.when(c == last)(lambda: run(tail))


def _scatter_rows(x1, slot_flat, n_slots):
    total, d = x1.shape
    ct = SLOT_BLOCK // TOP_K
    return pl.pallas_call(
        functools.partial(_scatter_rows_kernel, n_tok=total),
        grid=(pl.cdiv(total, ct),),
        in_specs=[pl.BlockSpec((SLOT_BLOCK,), lambda c: (c,), memory_space=pltpu.SMEM),
                  pl.BlockSpec((ct, d), lambda c: (c, 0))],
        out_specs=pl.BlockSpec(memory_space=pl.ANY),
        out_shape=jax.ShapeDtypeStruct((n_slots, d), x1.dtype),
        scratch_shapes=[pltpu.SemaphoreType.DMA(())],
        compiler_params=_params("arbitrary"),
        name="scatter_rows",
    )(slot_flat, x1)


def _expert_kernel(te_ref, nv_ref, nu_ref, x_ref, *refs):
    nf, _, fc = refs[-1].shape
    wg_refs, wu_refs = refs[:W_SPLIT], refs[W_SPLIT:2 * W_SPLIT]
    wd_refs = refs[2 * W_SPLIT:2 * W_SPLIT + nf]
    o_ref, xb_ref, wgu_ref, h_ref = refs[2 * W_SPLIT + nf:]
    i = pl.program_id(0)
    s = pl.program_id(1)

    @pl.when(i < nu_ref[0])
    def _():
        @pl.when(s == 0)
        def _():
            rid = lax.broadcasted_iota(jnp.int32, (x_ref.shape[0], 1), 0)
            lo, hi = _unpack_bf16_pairs(jnp.where(rid < nv_ref[i], x_ref[...], jnp.uint32(0)))
            xb_ref[:, :x_ref.shape[1]] = lo
            xb_ref[:, x_ref.shape[1]:] = hi

        @pl.when(s < nf)
        def _():
            wc = fc // W_SPLIT
            for k in range(W_SPLIT):
                wgu_ref[:, k * wc:(k + 1) * wc] = wg_refs[k][...].astype(BF16)
                wgu_ref[:, fc + k * wc:fc + (k + 1) * wc] = wu_refs[k][...].astype(BF16)
            gu = jnp.dot(xb_ref[...], wgu_ref[...], preferred_element_type=F32)
            h_ref[s] = (jax.nn.silu(gu[:, :fc]) * gu[:, fc:]).astype(BF16)

        @pl.when(s >= nf)
        def _():
            y = jnp.dot(h_ref[0], wd_refs[0][...].astype(BF16), preferred_element_type=F32)
            for f in range(1, nf):
                y = y + jnp.dot(h_ref[f], wd_refs[f][...].astype(BF16), preferred_element_type=F32)
            o_ref[...] = y


def _expert_ffn(xs, w_gate, w_up, w_down, tile_e, tile_nv, n_used, l):
    rows = xs.shape[0]
    d, dff = w_gate.shape[2:]
    nf = dff // FF_CHUNK
    nc = d // OUT_CHUNK
    steps = nf + nc
    nt = rows // EXPERT_TILE

    def item(i, nu):
        return jnp.minimum(i, jnp.maximum(nu[0] - 1, 0))

    def step(i, s, nu):
        return jnp.where(i < nu[0], s, steps - 1)

    def gmap(k):
        def index(i, s, te, nv, nu):
            s = step(i, s, nu)
            e = jnp.where(s < nf, te[item(i, nu)], te[item(i + 1, nu)])
            return (l, e, 0, jnp.where(s < nf, s, 0) * W_SPLIT + k)
        return index

    col = lambda i, s, nu: jnp.maximum(step(i, s, nu) - nf, 0)
    xmap = lambda i, s, te, nv, nu: (item(i, nu), 0)
    dmap = lambda f: (lambda i, s, te, nv, nu: (l, te[item(i, nu)], f, col(i, s, nu)))
    omap = lambda i, s, te, nv, nu: (item(i, nu), col(i, s, nu))
    gate_up = [pl.BlockSpec((None, None, d, FF_CHUNK // W_SPLIT), gmap(k)) for k in range(W_SPLIT)]
    return pl.pallas_call(
        _expert_kernel,
        grid_spec=pltpu.PrefetchScalarGridSpec(
            num_scalar_prefetch=3,
            grid=(nt, steps),
            in_specs=[pl.BlockSpec((EXPERT_TILE, xs.shape[1]), xmap)] + gate_up + gate_up
            + [pl.BlockSpec((None, None, FF_CHUNK, OUT_CHUNK), dmap(f)) for f in range(nf)],
            out_specs=pl.BlockSpec((EXPERT_TILE, OUT_CHUNK), omap),
            scratch_shapes=[pltpu.VMEM((EXPERT_TILE, d), BF16),
                            pltpu.VMEM((d, 2 * FF_CHUNK), BF16),
                            pltpu.VMEM((nf, EXPERT_TILE, FF_CHUNK), BF16)]),
        out_shape=jax.ShapeDtypeStruct((rows, d), F32),
        compiler_params=_params("arbitrary", "arbitrary"),
        name="expert_ffn",
    )(tile_e, tile_nv, n_used, xs, *([w_gate] * W_SPLIT), *([w_up] * W_SPLIT), *([w_down] * nf))


def _combine_kernel(slot_cur, slot_nxt, x1_ref, r_ref, g_ref, b_ref, y_hbm, o_ref, ybuf, sems, *, alpha):
    i = pl.program_id(0)
    ct = x1_ref.shape[0]

    def fetch(slot_ref, buf):
        def issue(jb, carry):
            j0 = pl.multiple_of(jb * ISSUE_UNROLL, ISSUE_UNROLL)
            for u in range(ISSUE_UNROLL):
                for k in range(TOP_K):
                    pltpu.make_async_copy(y_hbm.at[pl.ds(slot_ref[TOP_K * (j0 + u) + k], 1)],
                                          ybuf.at[buf, k, pl.ds(j0 + u, 1)], sems.at[buf]).start()
            return carry

        lax.fori_loop(0, ct // ISSUE_UNROLL, issue, 0)

    @pl.when(i == 0)
    def _():
        fetch(slot_cur, 0)

    @pl.when(i + 1 < pl.num_programs(0))
    def _():
        fetch(slot_nxt, (i + 1) % 2)

    buf = i % 2
    for k in range(TOP_K):
        pltpu.make_async_copy(ybuf.at[buf, k], ybuf.at[buf, k], sems.at[buf]).wait()
    moe = ybuf[buf, 0] * r_ref[0] + ybuf[buf, 1] * r_ref[1]
    o_ref[...] = _layer_norm(alpha * x1_ref[...] + moe, g_ref[...], b_ref[...])


def _combine(x1, yb, slot_flat, gates, ln_g, ln_b, l, alpha, tm, row0, m):
    d = x1.shape[1]
    nsteps = m // tm
    rb0 = row0 // tm
    sb0 = row0 * TOP_K // SLOT_BLOCK
    sper = pl.cdiv(tm * TOP_K, SLOT_BLOCK)
    rowo = lambda i: (i + rb0, 0)
    lay = lambda i: (l, 0, 0)
    return pl.pallas_call(
        functools.partial(_combine_kernel, alpha=alpha),
        grid=(nsteps,),
        in_specs=[pl.BlockSpec((SLOT_BLOCK,), lambda i: (sb0 + i * sper,), memory_space=pltpu.SMEM),
                  pl.BlockSpec((SLOT_BLOCK,), lambda i: (sb0 + jnp.minimum(i + 1, nsteps - 1) * sper,),
                               memory_space=pltpu.SMEM),
                  pl.BlockSpec((tm, d), rowo),
                  pl.BlockSpec((TOP_K, tm, 1), lambda i: (0, i + rb0, 0)),
                  pl.BlockSpec((None, 1, d), lay), pl.BlockSpec((None, 1, d), lay),
                  pl.BlockSpec(memory_space=pl.ANY)],
        out_specs=pl.BlockSpec((tm, d), lambda i: (i, 0)),
        out_shape=jax.ShapeDtypeStruct((m, d), F32),
        scratch_shapes=[pltpu.VMEM((2, TOP_K, tm, d), F32), pltpu.SemaphoreType.DMA((2,))],
        compiler_params=_params("arbitrary"),
        name="combine",
    )(slot_flat, slot_flat, x1, gates, ln_g, ln_b, yb)


def _rope_tables(pos):
    half = ROT_DIM // 2
    inv = ROPE_THETA ** (-jnp.arange(half, dtype=F32) * 2.0 / ROT_DIM)
    ang = pos.astype(F32)[:, None] * inv[None, :]
    cos, sin = jnp.cos(ang), jnp.sin(ang)
    n = pos.shape[0]
    rest = HEAD_DIM - ROT_DIM
    cs = jnp.concatenate([cos, cos, jnp.ones((n, rest), F32)], axis=1)
    sa = jnp.concatenate([-sin, jnp.zeros((n, half + rest), F32)], axis=1)
    sb = jnp.concatenate([jnp.zeros((n, half), F32), sin, jnp.zeros((n, rest), F32)], axis=1)
    rep = LANES // HEAD_DIM
    return tuple(jnp.tile(t, (1, rep)) for t in (cs, sa, sb))


def kernel(x_prompt, x_sample, state_conv, cache_k, cache_v, w_in, conv_w, attn_sinks, w_o,
           ln1_g, ln1_b, w_router, b_router, w_gate, w_up, w_down, ln2_g, ln2_b):
    batch, seq, d = x_prompt.shape
    nseq, dec_seq, _ = x_sample.shape
    depth = w_in.shape[0]
    cdim = conv_w.shape[2]
    adim = d - cdim
    kvd = N_KV_HEADS * HEAD_DIM
    n_heads = adim // HEAD_DIM
    ne = w_router.shape[1]
    mp, ms = batch * seq, nseq * dec_seq
    total = mp + ms
    assign = total * TOP_K
    alpha = (2 * depth) ** 0.25

    sinks = attn_sinks[:, jnp.asarray(_paired_head_order(n_heads))]
    wr_t = w_router.T
    br = jnp.broadcast_to(b_router[:, None], (ne, LANES))

    tabs_p = _rope_tables(jnp.arange(seq))
    tabs_s = _rope_tables(jnp.tile(PAST_LEN + jnp.arange(dec_seq), nseq))
    cache_k2 = cache_k.reshape(depth, nseq, WINDOW, kvd)
    cache_v2 = cache_v.reshape(depth, nseq, WINDOW, kvd)
    ln1g, ln1b = ln1_g.reshape(depth, 1, d), ln1_b.reshape(depth, 1, d)
    ln2g, ln2b = ln2_g.reshape(depth, 1, d), ln2_b.reshape(depth, 1, d)

    n_items = pl.cdiv(assign, EXPERT_TILE) + ne
    slot_len = pl.cdiv(assign, SLOT_BLOCK) * SLOT_BLOCK
    items = jnp.arange(n_items, dtype=jnp.int32)

    xp = x_prompt.reshape(mp, d)
    xs_ = x_sample.reshape(ms, d)
    outs = [[] for _ in range(6)]
    for l in range(depth):
        conv_p, cst_p = _conv_proj(xp, w_in, conv_w, l, 512, cdim, seq=seq, batch=batch)
        q_p, k_p, v_p, kw_p, vw_p = _qkv_proj(xp, w_in, l, tabs_p, 512, cdim, adim, kvd, seq=seq, batch=batch)
        attn_p = _attn_prompt(sinks, q_p, k_p, v_p, l, batch, seq)

        st = state_conv[l]
        st1 = jnp.repeat(st[:, 1], dec_seq, axis=0)
        st2 = jnp.stack([st[:, 0], st[:, 1]] + [st[:, 1]] * (dec_seq - 2), axis=1).reshape(ms, cdim)
        conv_s, inner_s = _conv_proj(xs_, w_in, conv_w, l, ms, cdim, state=(st1, st2), dec_seq=dec_seq)
        q_s, kn_s, vn_s = _qkv_proj(xs_, w_in, l, tabs_s, ms, cdim, adim, kvd)
        attn_s, kw_s, vw_s = _attn_sample(
            sinks, q_s.reshape(nseq, dec_seq, adim), kn_s.reshape(nseq, dec_seq, kvd),
            vn_s.reshape(nseq, dec_seq, kvd), cache_k2, cache_v2, l)

        zeros_cnt = jnp.zeros((ne, LANES), F32)
        *bufs, cnt = _mix_out(conv_p, attn_p, xp, w_o, ln1g, ln1b, wr_t, br, zeros_cnt, l, alpha, 256, total, 0)
        x1, x1p, route, cnt = _mix_out(conv_s, attn_s.reshape(ms, adim), xs_, w_o, ln1g, ln1b, wr_t, br, cnt,
                                       l, alpha, ms, total, mp // ms, prev=bufs)

        e = route[0:TOP_K].astype(jnp.int32)
        rank = route[4:4 + TOP_K].astype(jnp.int32)
        gates = route[2:2 + TOP_K][:, :, None]
        counts = cnt[:, 0].astype(jnp.int32)
        tiles_e = (counts + EXPERT_TILE - 1) // EXPERT_TILE
        tile_end = jnp.cumsum(tiles_e)
        tile_start = tile_end - tiles_e
        onehot = e[:, :, None] == jnp.arange(ne)[None, None, :]
        pos = jnp.sum(jnp.where(onehot, (tile_start * EXPERT_TILE)[None, None, :], 0), axis=-1) + rank
        slot_flat = jnp.pad(pos.T.reshape(-1), (0, slot_len - assign))
        n_used = tile_end[-1:].astype(jnp.int32)
        item_e = jnp.minimum(jnp.sum(tile_end[None, :] <= items[:, None], axis=1), ne - 1).astype(jnp.int32)
        item_nv = jnp.clip(counts[item_e] - (items - tile_start[item_e]) * EXPERT_TILE, 0, EXPERT_TILE)

        xsorted = _scatter_rows(x1p, slot_flat, n_items * EXPERT_TILE)
        yb = _expert_ffn(xsorted, w_gate, w_up, w_down, item_e, item_nv.astype(jnp.int32), n_used, l)

        xp = _combine(x1, yb, slot_flat, gates, ln2g, ln2b, l, alpha, 512, 0, mp)
        xs_ = _combine(x1, yb, slot_flat, gates, ln2g, ln2b, l, alpha, ms, mp, ms)

        inner3 = inner_s.reshape(nseq, dec_seq, cdim)
        for lst, val in zip(outs, (cst_p, kw_p.reshape(batch, WINDOW, N_KV_HEADS, HEAD_DIM),
                                   vw_p.reshape(batch, WINDOW, N_KV_HEADS, HEAD_DIM),
                                   inner3[:, dec_seq - 2:],
                                   kw_s.reshape(nseq, WINDOW, N_KV_HEADS, HEAD_DIM),
                                   vw_s.reshape(nseq, WINDOW, N_KV_HEADS, HEAD_DIM))):
            lst.append(val)

    return (xp.reshape(batch, seq, d), xs_.reshape(nseq, dec_seq, d), *[jnp.stack(o) for o in outs])
```

```python
import functools
import math

import jax
import jax.numpy as jnp
from jax import lax
from jax.experimental import pallas as pl
from jax.experimental.pallas import tpu as pltpu

F32 = jnp.float32
BF16 = jnp.bfloat16

HEAD_DIM = 64
N_KV_HEADS = 4
ROT_DIM = HEAD_DIM // 4
ROPE_THETA = 500000.0
WINDOW = 128
PAST_LEN = 16384
N_EXPERT_GROUPS = 4
TOP_K = 2
LN_EPS = 1e-5
QK_SCALE = HEAD_DIM ** -0.5
assert math.frexp(QK_SCALE)[0] == 0.5

LANES = 128
VMEM_LIMIT = 56 * 1024 * 1024

EXPERT_TILE = 640
FF_CHUNK = 512
OUT_CHUNK = 1024
W_SPLIT = 2
CONV_GROUP = 512
ROUTE_ROWS = 8
SAMPLE_SEQS_PER_STEP = 4
SLOT_BLOCK = 1024
ISSUE_UNROLL = 8
MIX_SUB_ROWS = 256


def _params(*sem):
    return pltpu.CompilerParams(dimension_semantics=sem, vmem_limit_bytes=VMEM_LIMIT)


def _rope_chunk(xc, cs, sa, sb):
    return xc * cs + pltpu.roll(xc, LANES - ROT_DIM // 2, 1) * sa + pltpu.roll(xc, ROT_DIM // 2, 1) * sb


def _conv(inner, gate, w_ref, prev1, prev2):
    return gate * (w_ref[0:1, :] * prev2 + w_ref[1:2, :] * prev1 + w_ref[2:3, :] * inner)


def _paired_head_order(n_heads):
    group = n_heads // N_KV_HEADS
    return [(2 * pair + half) * group + j
            for pair in range(N_KV_HEADS // 2) for j in range(group) for half in range(2)]


def _pair_q_chunks(chunks):
    order = _paired_head_order(2 * len(chunks))
    lo = lax.broadcasted_iota(jnp.int32, chunks[0].shape, 1) < HEAD_DIM
    out = []
    for c in range(len(chunks)):
        halves = []
        for half, head in enumerate(order[2 * c:2 * c + 2]):
            src = chunks[head // 2]
            halves.append(src if head % 2 == half else pltpu.roll(src, HEAD_DIM, 1))
        out.append(jnp.where(lo, halves[0], halves[1]))
    return out


def _conv_proj_kernel(x_ref, wc_ref, wb_ref, wu_ref, cw_ref, *refs, tiles_per_seq, dec_seq):
    if dec_seq is None:
        conv_o, cst_o, wbf, carry = refs
    else:
        st1, st2, conv_o, inner_o, wbf = refs
    i = pl.program_id(1)

    @pl.when(i == 0)
    def _():
        for k, w in enumerate((wc_ref, wb_ref, wu_ref)):
            wbf[k] = w[...].astype(BF16)

    xb = x_ref[...].astype(BF16)
    c, gate, u = (jnp.dot(xb, wbf[k], preferred_element_type=F32) for k in range(3))
    inner = c * u
    tm, cw = inner.shape
    row = lax.broadcasted_iota(jnp.int32, (tm, cw), 0)
    if dec_seq is None:
        @pl.when(i % tiles_per_seq == 0)
        def _():
            carry[...] = jnp.zeros_like(carry)

        c1, c2 = carry[7:8, :], carry[6:7, :]
        prev1 = jnp.where(row == 0, c1, pltpu.roll(inner, 1, 0))
        prev2 = jnp.where(row == 0, c2, jnp.where(row == 1, c1, pltpu.roll(inner, 2, 0)))
        carry[...] = inner[tm - 8:, :]
        cst_o[...] = inner[tm - 2:, :]
    else:
        t = row % dec_seq
        prev1 = jnp.where(t == 0, st1[...], pltpu.roll(inner, 1, 0))
        prev2 = jnp.where(t < 2, st2[...], pltpu.roll(inner, 2, 0))
        inner_o[...] = inner
    conv_o[...] = _conv(inner, gate, cw_ref, prev1, prev2).astype(BF16)


def _conv_proj(x, w_in, conv_w, l, tm, cdim, seq=None, batch=None, state=None, dec_seq=None):
    m, k = x.shape
    cw = CONV_GROUP
    groups = cdim // cw
    sec = lambda s: pl.BlockSpec((None, k, cw), lambda g, i: (l, 0, s * groups + g))
    blk = pl.BlockSpec((tm, cw), lambda g, i: (i, g))
    in_specs = [pl.BlockSpec((tm, k), lambda g, i: (i, 0)), sec(0), sec(1), sec(2),
                pl.BlockSpec((None, 3, cw), lambda g, i: (l, 0, g))]
    scratch = [pltpu.VMEM((3, k, cw), BF16)]
    if state is None:
        tps = seq // tm
        args = ()
        out_specs = [blk, pl.BlockSpec((None, 2, cw), lambda g, i: (i // tps, 0, g))]
        out_shape = [jax.ShapeDtypeStruct((m, cdim), BF16), jax.ShapeDtypeStruct((batch, 2, cdim), F32)]
        scratch.append(pltpu.VMEM((8, cw), F32))
    else:
        tps = None
        args = state
        in_specs += [blk, blk]
        out_specs = [blk, blk]
        out_shape = [jax.ShapeDtypeStruct((m, cdim), BF16), jax.ShapeDtypeStruct((m, cdim), F32)]
    return pl.pallas_call(
        functools.partial(_conv_proj_kernel, tiles_per_seq=tps, dec_seq=dec_seq),
        grid=(groups, m // tm),
        in_specs=in_specs, out_specs=out_specs, out_shape=out_shape, scratch_shapes=scratch,
        compiler_params=_params("arbitrary", "arbitrary"),
        name="conv_proj_prompt" if state is None else "conv_proj_sample",
    )(x, w_in, w_in, w_in, conv_w, *args)


def _qkv_proj_kernel(x_ref, w_ref, cs_ref, sa_ref, sb_ref, q_o, k_o, v_o, *refs, adim, q_scale):
    wbf = refs[-1]

    @pl.when(pl.program_id(0) == 0)
    def _():
        wbf[...] = w_ref[...].astype(BF16)

    h = jnp.dot(x_ref[...].astype(BF16), wbf[...], preferred_element_type=F32)
    tm = h.shape[0]
    kvd = (h.shape[1] - adim) // 2
    cs, sa, sb = cs_ref[...], sa_ref[...], sb_ref[...]
    roped = [_rope_chunk(h[:, c * LANES:(c + 1) * LANES], cs, sa, sb) for c in range((adim + kvd) // LANES)]
    for c, qc in enumerate(_pair_q_chunks(roped[:adim // LANES])):
        q_o[:, c * LANES:(c + 1) * LANES] = (qc * q_scale).astype(q_o.dtype)
    for c, kc in enumerate(roped[adim // LANES:]):
        k_o[:, c * LANES:(c + 1) * LANES] = kc.astype(k_o.dtype)
    v = h[:, adim + kvd:]
    v_o[...] = v.astype(v_o.dtype)
    if len(refs) > 1:
        kw_o, vw_o = refs[:2]
        for c, kc in enumerate(roped[adim // LANES:]):
            kw_o[:, c * LANES:(c + 1) * LANES] = kc[tm - WINDOW:, :]
        vw_o[...] = v[tm - WINDOW:, :]


def _qkv_proj(x, w_in, l, tabs, tm, cdim, adim, kvd, seq=None, batch=None):
    m, k = x.shape
    n = adim + 2 * kvd
    row = lambda i: (i, 0)
    prompt = seq is not None
    tps = seq // tm if prompt else 1
    tab = pl.BlockSpec((tm, LANES), lambda i: (i % tps, 0))
    dt = BF16 if prompt else F32
    out_specs = [pl.BlockSpec((tm, adim), row), pl.BlockSpec((tm, kvd), row), pl.BlockSpec((tm, kvd), row)]
    out_shape = [jax.ShapeDtypeStruct((m, adim), dt), jax.ShapeDtypeStruct((m, kvd), dt),
                 jax.ShapeDtypeStruct((m, kvd), dt)]
    if prompt:
        win = pl.BlockSpec((None, WINDOW, kvd), lambda i: (i // tps, 0, 0))
        out_specs += [win, win]
        out_shape += [jax.ShapeDtypeStruct((batch, WINDOW, kvd), F32)] * 2
    return pl.pallas_call(
        functools.partial(_qkv_proj_kernel, adim=adim, q_scale=QK_SCALE if prompt else 1.0),
        grid=(m // tm,),
        in_specs=[pl.BlockSpec((tm, k), row),
                  pl.BlockSpec((None, k, n), lambda i: (l, 0, 3 * cdim // n), pipeline_mode=pl.Buffered(1)),
                  tab, tab, tab],
        out_specs=out_specs, out_shape=out_shape,
        scratch_shapes=[pltpu.VMEM((k, n), BF16)],
        compiler_params=_params("arbitrary"),
        name="qkv_proj_prompt" if prompt else "qkv_proj_sample",
    )(x, w_in, *tabs)


def _sink_softmax(s, mask, sink_col):
    s = jnp.where(mask, s, -jnp.inf)
    m = jnp.maximum(jnp.max(s, axis=-1, keepdims=True), sink_col)
    p = jnp.exp(s - m)
    den = jnp.sum(p, axis=-1, keepdims=True) + jnp.exp(sink_col - m)
    return p * (1.0 / den)


def _attn_prompt_kernel(sink_ref, q_ref, kp_ref, kc_ref, vp_ref, vc_ref, o_ref, bias_ref, *, l):
    n = pl.program_id(1)
    w = q_ref.shape[0]
    chunks = q_ref.shape[1] // LANES
    pairs = kc_ref.shape[1] // LANES
    cpp = chunks // pairs
    lo = lax.broadcasted_iota(jnp.int32, (w, LANES), 1) < HEAD_DIM
    qi = lax.broadcasted_iota(jnp.int32, (w, 2 * w), 0)
    sj = lax.broadcasted_iota(jnp.int32, (w, 2 * w), 1)
    mask = (sj <= qi + w) & (sj > qi) & ((n > 0) | (sj >= w))
    bias_ref[...] = jnp.where(mask, 0.0, -jnp.inf)
    zero = jnp.zeros((w, LANES), BF16)
    for p in range(pairs):
        ksl = slice(p * LANES, (p + 1) * LANES)
        kk = jnp.concatenate([kp_ref[:, ksl], kc_ref[:, ksl]], axis=0)
        vv = jnp.concatenate([vp_ref[:, ksl], vc_ref[:, ksl]], axis=0)
        for j in range(cpp):
            c = p * cpp + j
            qc = q_ref[:, c * LANES:(c + 1) * LANES]
            halves = []
            for half, qh in enumerate((jnp.where(lo, qc, zero), jnp.where(lo, zero, qc))):
                sink = sink_ref[l, 2 * c + half]
                s = lax.dot_general(qh, kk, (((1,), (1,)), ((), ())), preferred_element_type=F32) + bias_ref[...]
                m = jnp.maximum(jnp.max(s, axis=-1, keepdims=True), sink)
                pexp = jnp.exp(s - m)
                den = jnp.sum(pexp, axis=-1, keepdims=True) + jnp.exp(sink - m)
                pr = (pexp * (1.0 / den)).astype(BF16)
                halves.append(jnp.dot(pr, vv, preferred_element_type=F32))
            o_ref[:, c * LANES:(c + 1) * LANES] = jnp.where(lo, halves[0], halves[1]).astype(BF16)


def _attn_prompt(sinks, q, k, v, l, batch, seq):
    m, adim = q.shape
    kvd = k.shape[1]
    nb = seq // WINDOW
    cur = lambda b, n: (b * nb + n, 0)
    prev = lambda b, n: (b * nb + jnp.maximum(n - 1, 0), 0)
    return pl.pallas_call(
        functools.partial(_attn_prompt_kernel, l=l),
        grid=(batch, nb),
        in_specs=[pl.BlockSpec(memory_space=pltpu.SMEM),
                  pl.BlockSpec((WINDOW, adim), cur),
                  pl.BlockSpec((WINDOW, kvd), prev), pl.BlockSpec((WINDOW, kvd), cur),
                  pl.BlockSpec((WINDOW, kvd), prev), pl.BlockSpec((WINDOW, kvd), cur)],
        out_specs=pl.BlockSpec((WINDOW, adim), cur),
        out_shape=jax.ShapeDtypeStruct((m, adim), BF16),
        scratch_shapes=[pltpu.VMEM((WINDOW, 2 * WINDOW), F32)],
        compiler_params=_params("arbitrary", "arbitrary"),
        name="attn_prompt",
    )(sinks, q, k, k, v, v)


def _attn_sample_kernel(sink_ref, q_ref, kn_ref, vn_ref, ck_ref, cv_ref,
                        o_ref, kw_ref, vw_ref, kk_ref, vv_ref, qs_ref, oacc_ref, *, l, t):
    nsq, w, kvd = ck_ref.shape
    tot = kk_ref.shape[1]
    chunks = q_ref.shape[1] // LANES
    pairs = kvd // LANES
    cpp = chunks // pairs
    rows = 2 * cpp * t
    lo = lax.broadcasted_iota(jnp.int32, (t, LANES), 1) < HEAD_DIM
    qi = lax.broadcasted_iota(jnp.int32, (rows, tot), 0) % t
    sj = lax.broadcasted_iota(jnp.int32, (rows, tot), 1)
    mask = (sj <= qi + w) & (sj > qi)
    rowid = lax.broadcasted_iota(jnp.int32, (rows, 1), 0) // t
    pad = jnp.zeros((tot - w - t, kvd), F32)
    for g in range(nsq):
        tok = slice(g * t, (g + 1) * t)
        kk_ref[g, 0:w, :] = ck_ref[g]
        vv_ref[g, 0:w, :] = cv_ref[g]
        kk_ref[g, w:w + t, :] = kn_ref[tok, :]
        vv_ref[g, w:w + t, :] = vn_ref[tok, :]
        kk_ref[g, w + t:, :] = pad
        vv_ref[g, w + t:, :] = pad
        kw_ref[g] = pltpu.roll(kk_ref[g], tot - t, 0)[0:w, :]
        vw_ref[g] = pltpu.roll(vv_ref[g], tot - t, 0)[0:w, :]
        for p in range(pairs):
            ksl = slice(p * LANES, (p + 1) * LANES)
            sink_col = jnp.zeros((rows, 1), F32)
            for j in range(cpp):
                c = p * cpp + j
                qc = q_ref[tok, c * LANES:(c + 1) * LANES]
                qs_ref[g, (2 * j) * t:(2 * j + 1) * t, :] = jnp.where(lo, qc, 0.0)
                qs_ref[g, (2 * j + 1) * t:(2 * j + 2) * t, :] = jnp.where(lo, 0.0, qc)
                sink_col = jnp.where(rowid == 2 * j, sink_ref[l, 2 * c], sink_col)
                sink_col = jnp.where(rowid == 2 * j + 1, sink_ref[l, 2 * c + 1], sink_col)
            s = lax.dot_general(qs_ref[g].astype(BF16), kk_ref[g, :, ksl].astype(BF16),
                                (((1,), (1,)), ((), ())), preferred_element_type=F32) * QK_SCALE
            pr = _sink_softmax(s, mask, sink_col)
            qs_ref[g] = jnp.dot(pr.astype(BF16), vv_ref[g, :, ksl].astype(BF16), preferred_element_type=F32)
            for j in range(cpp):
                c = p * cpp + j
                oacc_ref[tok, c * LANES:(c + 1) * LANES] = jnp.where(
                    lo, qs_ref[g, (2 * j) * t:(2 * j + 1) * t, :], qs_ref[g, (2 * j + 1) * t:(2 * j + 2) * t, :])
    o_ref[...] = oacc_ref[...].astype(BF16)


def _attn_sample(sinks, q, kn, vn, cache_k, cache_v, l, t):
    m, adim = q.shape
    nseq, w, kvd = cache_k.shape[1:]
    nsq = SAMPLE_SEQS_PER_STEP
    tot = -(-(w + t) // 8) * 8
    rows = 2 * (adim // kvd) * t
    row = lambda b: (b, 0)
    seq3 = lambda b: (b, 0, 0)
    cache = lambda b: (l, b, 0, 0)
    return pl.pallas_call(
        functools.partial(_attn_sample_kernel, l=l, t=t),
        grid=(nseq // nsq,),
        in_specs=[pl.BlockSpec(memory_space=pltpu.SMEM),
                  pl.BlockSpec((nsq * t, adim), row),
                  pl.BlockSpec((nsq * t, kvd), row), pl.BlockSpec((nsq * t, kvd), row),
                  pl.BlockSpec((None, nsq, w, kvd), cache), pl.BlockSpec((None, nsq, w, kvd), cache)],
        out_specs=[pl.BlockSpec((nsq * t, adim), row),
                   pl.BlockSpec((nsq, w, kvd), seq3), pl.BlockSpec((nsq, w, kvd), seq3)],
        out_shape=[jax.ShapeDtypeStruct((m, adim), BF16),
                   jax.ShapeDtypeStruct((nseq, w, kvd), F32),
                   jax.ShapeDtypeStruct((nseq, w, kvd), F32)],
        scratch_shapes=[pltpu.VMEM((nsq, tot, kvd), F32), pltpu.VMEM((nsq, tot, kvd), F32),
                        pltpu.VMEM((nsq, rows, LANES), F32), pltpu.VMEM((nsq * t, adim), F32)],
        compiler_params=_params("arbitrary"),
        name="attn_sample",
    )(sinks, q, kn, vn, cache_k, cache_v)


def _layer_norm(z, g, b):
    mu = jnp.mean(z, axis=-1, keepdims=True)
    d = z - mu
    var = jnp.mean(d * d, axis=-1, keepdims=True)
    return d * lax.rsqrt(var + LN_EPS) * g + b


def _top2(sg, idx, n):
    m1 = jnp.max(sg, axis=0, keepdims=True)
    i1 = jnp.min(jnp.where(sg == m1, idx, n), axis=0, keepdims=True)
    rest = jnp.where(idx == i1, -jnp.inf, sg)
    m2 = jnp.max(rest, axis=0, keepdims=True)
    i2 = jnp.min(jnp.where(rest == m2, idx, n), axis=0, keepdims=True)
    return m1, i1, m2, i2


def _pack_bf16_pairs(x):
    half = x.shape[1] // 2
    lo = pltpu.bitcast(x[:, :half].astype(BF16).astype(F32), jnp.uint32)
    hi = pltpu.bitcast(x[:, half:].astype(BF16).astype(F32), jnp.uint32)
    return hi | (lo >> 16)


def _unpack_bf16_pairs(w):
    lo = pltpu.bitcast(w << 16, F32).astype(BF16)
    hi = pltpu.bitcast(w & jnp.uint32(0xFFFF0000), F32).astype(BF16)
    return lo, hi


def _mix_out_kernel(conv_ref, attn_ref, x_ref, wo_ref, g_ref, b_ref, wr_ref, br_ref, cnt_in,
                    x1_ref, x1p_ref, route_ref, cnt_ref, carry, wob, wrb, *, alpha):
    cdim = conv_ref.shape[1]
    sub = min(x_ref.shape[0], MIX_SUB_ROWS)

    @pl.when(pl.program_id(0) == 0)
    def _():
        carry[...] = cnt_in[...]
        wrb[...] = wr_ref[...].astype(BF16)
        wob[0:cdim, :] = wo_ref[0:cdim, :].astype(BF16)
        order = _paired_head_order((wo_ref.shape[0] - cdim) // HEAD_DIM)
        for pos, head in enumerate(order):
            wob[cdim + pos * HEAD_DIM:cdim + (pos + 1) * HEAD_DIM, :] = (
                wo_ref[cdim + head * HEAD_DIM:cdim + (head + 1) * HEAD_DIM, :].astype(BF16))

    counts = carry[...]
    for r0 in range(0, x_ref.shape[0], sub):
        rows = slice(r0, r0 + sub)
        y = jnp.dot(conv_ref[rows, :], wob[0:cdim, :], preferred_element_type=F32)
        y = y + jnp.dot(attn_ref[rows, :], wob[cdim:, :], preferred_element_type=F32)
        x1 = _layer_norm(alpha * x_ref[rows, :] + y, g_ref[...], b_ref[...])
        x1_ref[rows, :] = x1
        x1p_ref[rows, :] = _pack_bf16_pairs(x1)
        logits = lax.dot_general(wrb[...], x1.astype(BF16), (((1,), (1,)), ((), ())), preferred_element_type=F32)
        record, counts = _route(logits, br_ref[...], counts)
        route_ref[:, rows] = record
    carry[...] = counts
    cnt_ref[...] = counts


def _route(logits, bias, counts):
    ne, tm = logits.shape
    epg = ne // N_EXPERT_GROUPS
    lanes = lambda a: jnp.concatenate([a] * (tm // LANES), axis=1)
    scores = jax.nn.sigmoid(logits)
    sel = scores + lanes(bias)
    eidx = lax.broadcasted_iota(jnp.int32, (ne, tm), 0).astype(F32)
    best = None
    for g in range(N_EXPERT_GROUPS):
        gidx = lax.broadcasted_iota(jnp.int32, (epg, tm), 0).astype(F32) + float(g * epg)
        m1, i1, m2, i2 = _top2(sel[g * epg:(g + 1) * epg], gidx, float(ne))
        cand = (m1 + m2, i1, i2)
        if best is None:
            best = cand
        else:
            upd = cand[0] > best[0]
            best = tuple(jnp.where(upd, new, old) for new, old in zip(cand, best))
    _, e1, e2 = best
    hit1 = eidx == e1
    hit2 = eidx == e2
    w1 = jnp.sum(jnp.where(hit1, scores, 0.0), axis=0, keepdims=True)
    w2 = jnp.sum(jnp.where(hit2, scores, 0.0), axis=0, keepdims=True)
    wsum = w1 + w2

    onehot = jnp.where(hit1 | hit2, 1.0, 0.0)
    earlier = (lax.broadcasted_iota(jnp.int32, (tm, tm), 0) < lax.broadcasted_iota(jnp.int32, (tm, tm), 1))
    before = jnp.dot(onehot.astype(BF16), jnp.where(earlier, 1.0, 0.0).astype(BF16), preferred_element_type=F32)
    before = before + lanes(counts)
    r1 = jnp.sum(jnp.where(hit1, before, 0.0), axis=0, keepdims=True)
    r2 = jnp.sum(jnp.where(hit2, before, 0.0), axis=0, keepdims=True)

    row = lax.broadcasted_iota(jnp.int32, (ROUTE_ROWS, tm), 0)
    out = jnp.zeros((ROUTE_ROWS, tm), F32)
    for c, val in enumerate((e1, e2, w1 / wsum, w2 / wsum, r1, r2)):
        out = jnp.where(row == c, val, out)
    return out, counts + jnp.sum(onehot, axis=1, keepdims=True)


def _mix_out(conv, attn, x, wo, ln_g, ln_b, wr_t, br, cnt_in, l, alpha, tm, total_rows, row_block0, prev=None):
    m, d = x.shape
    cdim = conv.shape[1]
    ne = wr_t.shape[0]
    rowi = lambda i: (i, 0)
    rowo = lambda i: (i + row_block0, 0)
    const = lambda i: (0, 0)
    lay = lambda i: (l, 0, 0)
    in_specs = [pl.BlockSpec((tm, cdim), rowi), pl.BlockSpec((tm, d - cdim), rowi), pl.BlockSpec((tm, d), rowi),
                pl.BlockSpec((None, d, d), lay, pipeline_mode=pl.Buffered(1)),
                pl.BlockSpec((None, 1, d), lay), pl.BlockSpec((None, 1, d), lay),
                pl.BlockSpec((ne, d), const), pl.BlockSpec((ne, LANES), const),
                pl.BlockSpec((ne, LANES), const)]
    args = [conv, attn, x, wo, ln_g, ln_b, wr_t, br, cnt_in]
    aliases = {}
    if prev is not None:
        in_specs += [pl.BlockSpec(memory_space=pl.ANY)] * len(prev)
        aliases = {len(args) + k: k for k in range(len(prev))}
        args += list(prev)

    def body(*refs):
        _mix_out_kernel(*refs[:9], *refs[-7:], alpha=alpha)

    return pl.pallas_call(
        body,
        grid=(m // tm,),
        in_specs=in_specs,
        out_specs=[pl.BlockSpec((tm, d), rowo), pl.BlockSpec((tm, d // 2), rowo),
                   pl.BlockSpec((ROUTE_ROWS, tm), lambda i: (0, i + row_block0)),
                   pl.BlockSpec((ne, LANES), const)],
        out_shape=[jax.ShapeDtypeStruct((total_rows, d), F32),
                   jax.ShapeDtypeStruct((total_rows, d // 2), jnp.uint32),
                   jax.ShapeDtypeStruct((ROUTE_ROWS, total_rows), F32),
                   jax.ShapeDtypeStruct((ne, LANES), F32)],
        scratch_shapes=[pltpu.VMEM((ne, LANES), F32), pltpu.VMEM((d, d), BF16), pltpu.VMEM((ne, d), BF16)],
        input_output_aliases=aliases,
        compiler_params=_params("arbitrary"),
        name="mix_out_sample" if prev is not None else "mix_out_prompt",
    )(*args)


def _scatter_rows_kernel(slot_ref, x_ref, o_ref, sem, *, n_tok):
    ct = x_ref.shape[0]
    c = pl.program_id(0)
    last = pl.num_programs(0) - 1
    tail = n_tok - (pl.cdiv(n_tok, ct) - 1) * ct

    def run(count):
        for j in range(count):
            for k in range(TOP_K):
                pltpu.make_async_copy(x_ref.at[pl.ds(j, 1)],
                                      o_ref.at[pl.ds(slot_ref[TOP_K * j + k], 1)], sem).start()
        for _ in range(TOP_K):
            pltpu.make_async_copy(x_ref.at[pl.ds(0, count)], x_ref.at[pl.ds(0, count)], sem).wait()

    if tail == ct:
        run(ct)
    else:
        pl.when(c < last)(lambda: run(ct))
        pl.when(c == last)(lambda: run(tail))


def _scatter_rows(x1, slot_flat, n_slots):
    total, d = x1.shape
    ct = SLOT_BLOCK // TOP_K
    return pl.pallas_call(
        functools.partial(_scatter_rows_kernel, n_tok=total),
        grid=(pl.cdiv(total, ct),),
        in_specs=[pl.BlockSpec((SLOT_BLOCK,), lambda c: (c,), memory_space=pltpu.SMEM),
                  pl.BlockSpec((ct, d), lambda c: (c, 0))],
        out_specs=pl.BlockSpec(memory_space=pl.ANY),
        out_shape=jax.ShapeDtypeStruct((n_slots, d), x1.dtype),
        scratch_shapes=[pltpu.SemaphoreType.DMA(())],
        compiler_params=_params("arbitrary"),
        name="scatter_rows",
    )(slot_flat, x1)


def _expert_kernel(te_ref, nv_ref, nu_ref, x_ref, *refs):
    nf, _, fc = refs[-1].shape
    wg_refs, wu_refs = refs[:W_SPLIT], refs[W_SPLIT:2 * W_SPLIT]
    wd_refs = refs[2 * W_SPLIT:2 * W_SPLIT + nf]
    o_ref, xb_ref, wgu_ref, h_ref = refs[2 * W_SPLIT + nf:]
    i = pl.program_id(0)
    s = pl.program_id(1)

    @pl.when(i < nu_ref[0])
    def _():
        @pl.when(s == 0)
        def _():
            rid = lax.broadcasted_iota(jnp.int32, (x_ref.shape[0], 1), 0)
            lo, hi = _unpack_bf16_pairs(jnp.where(rid < nv_ref[i], x_ref[...], jnp.uint32(0)))
            xb_ref[:, :x_ref.shape[1]] = lo
            xb_ref[:, x_ref.shape[1]:] = hi

        @pl.when(s < nf)
        def _():
            wc = fc // W_SPLIT
            for k in range(W_SPLIT):
                wgu_ref[:, k * wc:(k + 1) * wc] = wg_refs[k][...].astype(BF16)
                wgu_ref[:, fc + k * wc:fc + (k + 1) * wc] = wu_refs[k][...].astype(BF16)
            gu = jnp.dot(xb_ref[...], wgu_ref[...], preferred_element_type=F32)
            h_ref[s] = (jax.nn.silu(gu[:, :fc]) * gu[:, fc:]).astype(BF16)

        @pl.when(s >= nf)
        def _():
            y = jnp.dot(h_ref[0], wd_refs[0][...].astype(BF16), preferred_element_type=F32)
            for f in range(1, nf):
                y = y + jnp.dot(h_ref[f], wd_refs[f][...].astype(BF16), preferred_element_type=F32)
            o_ref[...] = y


def _expert_ffn(xs, w_gate, w_up, w_down, tile_e, tile_nv, n_used, l):
    rows = xs.shape[0]
    d, dff = w_gate.shape[2:]
    nf = dff // FF_CHUNK
    nc = d // OUT_CHUNK
    steps = nf + nc
    nt = rows // EXPERT_TILE

    def item(i, nu):
        return jnp.minimum(i, jnp.maximum(nu[0] - 1, 0))

    def step(i, s, nu):
        return jnp.where(i < nu[0], s, steps - 1)

    def gmap(k):
        def index(i, s, te, nv, nu):
            s = step(i, s, nu)
            e = jnp.where(s < nf, te[item(i, nu)], te[item(i + 1, nu)])
            return (l, e, 0, jnp.where(s < nf, s, 0) * W_SPLIT + k)
        return index

    col = lambda i, s, nu: jnp.maximum(step(i, s, nu) - nf, 0)
    xmap = lambda i, s, te, nv, nu: (item(i, nu), 0)
    dmap = lambda f: (lambda i, s, te, nv, nu: (l, te[item(i, nu)], f, col(i, s, nu)))
    omap = lambda i, s, te, nv, nu: (item(i, nu), col(i, s, nu))
    gate_up = [pl.BlockSpec((None, None, d, FF_CHUNK // W_SPLIT), gmap(k)) for k in range(W_SPLIT)]
    return pl.pallas_call(
        _expert_kernel,
        grid_spec=pltpu.PrefetchScalarGridSpec(
            num_scalar_prefetch=3,
            grid=(nt, steps),
            in_specs=[pl.BlockSpec((EXPERT_TILE, xs.shape[1]), xmap)] + gate_up + gate_up
            + [pl.BlockSpec((None, None, FF_CHUNK, OUT_CHUNK), dmap(f)) for f in range(nf)],
            out_specs=pl.BlockSpec((EXPERT_TILE, OUT_CHUNK), omap),
            scratch_shapes=[pltpu.VMEM((EXPERT_TILE, d), BF16),
                            pltpu.VMEM((d, 2 * FF_CHUNK), BF16),
                            pltpu.VMEM((nf, EXPERT_TILE, FF_CHUNK), BF16)]),
        out_shape=jax.ShapeDtypeStruct((rows, d), F32),
        compiler_params=_params("arbitrary", "arbitrary"),
        name="expert_ffn",
    )(tile_e, tile_nv, n_used, xs, *([w_gate] * W_SPLIT), *([w_up] * W_SPLIT), *([w_down] * nf))


def _combine_kernel(slot_cur, slot_nxt, x1_ref, r_ref, g_ref, b_ref, y_hbm, o_ref, ybuf, sems, *, alpha):
    i = pl.program_id(0)
    ct = x1_ref.shape[0]

    def fetch(slot_ref, buf):
        def issue(jb, carry):
            j0 = pl.multiple_of(jb * ISSUE_UNROLL, ISSUE_UNROLL)
            for u in range(ISSUE_UNROLL):
                for k in range(TOP_K):
                    pltpu.make_async_copy(y_hbm.at[pl.ds(slot_ref[TOP_K * (j0 + u) + k], 1)],
                                          ybuf.at[buf, k, pl.ds(j0 + u, 1)], sems.at[buf]).start()
            return carry

        lax.fori_loop(0, ct // ISSUE_UNROLL, issue, 0)

    @pl.when(i == 0)
    def _():
        fetch(slot_cur, 0)

    @pl.when(i + 1 < pl.num_programs(0))
    def _():
        fetch(slot_nxt, (i + 1) % 2)

    buf = i % 2
    for k in range(TOP_K):
        pltpu.make_async_copy(ybuf.at[buf, k], ybuf.at[buf, k], sems.at[buf]).wait()
    moe = ybuf[buf, 0] * r_ref[0] + ybuf[buf, 1] * r_ref[1]
    o_ref[...] = _layer_norm(alpha * x1_ref[...] + moe, g_ref[...], b_ref[...])


def _combine(x1, yb, slot_flat, gates, ln_g, ln_b, l, alpha, tm, row0, m):
    d = x1.shape[1]
    nsteps = m // tm
    rb0 = row0 // tm
    sb0 = row0 * TOP_K // SLOT_BLOCK
    sper = pl.cdiv(tm * TOP_K, SLOT_BLOCK)
    rowo = lambda i: (i + rb0, 0)
    lay = lambda i: (l, 0, 0)
    return pl.pallas_call(
        functools.partial(_combine_kernel, alpha=alpha),
        grid=(nsteps,),
        in_specs=[pl.BlockSpec((SLOT_BLOCK,), lambda i: (sb0 + i * sper,), memory_space=pltpu.SMEM),
                  pl.BlockSpec((SLOT_BLOCK,), lambda i: (sb0 + jnp.minimum(i + 1, nsteps - 1) * sper,),
                               memory_space=pltpu.SMEM),
                  pl.BlockSpec((tm, d), rowo),
                  pl.BlockSpec((TOP_K, tm, 1), lambda i: (0, i + rb0, 0)),
                  pl.BlockSpec((None, 1, d), lay), pl.BlockSpec((None, 1, d), lay),
                  pl.BlockSpec(memory_space=pl.ANY)],
        out_specs=pl.BlockSpec((tm, d), lambda i: (i, 0)),
        out_shape=jax.ShapeDtypeStruct((m, d), F32),
        scratch_shapes=[pltpu.VMEM((2, TOP_K, tm, d), F32), pltpu.SemaphoreType.DMA((2,))],
        compiler_params=_params("arbitrary"),
        name="combine",
    )(slot_flat, slot_flat, x1, gates, ln_g, ln_b, yb)


def _rope_tables(pos):
    half = ROT_DIM // 2
    inv = ROPE_THETA ** (-jnp.arange(half, dtype=F32) * 2.0 / ROT_DIM)
    ang = pos.astype(F32)[:, None] * inv[None, :]
    cos, sin = jnp.cos(ang), jnp.sin(ang)
    n = pos.shape[0]
    rest = HEAD_DIM - ROT_DIM
    cs = jnp.concatenate([cos, cos, jnp.ones((n, rest), F32)], axis=1)
    sa = jnp.concatenate([-sin, jnp.zeros((n, half + rest), F32)], axis=1)
    sb = jnp.concatenate([jnp.zeros((n, half), F32), sin, jnp.zeros((n, rest), F32)], axis=1)
    rep = LANES // HEAD_DIM
    return tuple(jnp.tile(t, (1, rep)) for t in (cs, sa, sb))


def kernel(x_prompt, x_sample, state_conv, cache_k, cache_v, w_in, conv_w, attn_sinks, w_o,
           ln1_g, ln1_b, w_router, b_router, w_gate, w_up, w_down, ln2_g, ln2_b):
    batch, seq, d = x_prompt.shape
    nseq, dec_seq, _ = x_sample.shape
    depth = w_in.shape[0]
    cdim = conv_w.shape[2]
    adim = d - cdim
    kvd = N_KV_HEADS * HEAD_DIM
    n_heads = adim // HEAD_DIM
    ne = w_router.shape[1]
    mp, ms = batch * seq, nseq * dec_seq
    total = mp + ms
    assign = total * TOP_K
    alpha = (2 * depth) ** 0.25

    sinks = attn_sinks[:, jnp.asarray(_paired_head_order(n_heads))]
    wr_t = w_router.T
    br = jnp.broadcast_to(b_router[:, None], (ne, LANES))

    tabs_p = _rope_tables(jnp.arange(seq))
    tabs_s = _rope_tables(jnp.tile(PAST_LEN + jnp.arange(dec_seq), nseq))
    cache_k2 = cache_k.reshape(depth, nseq, WINDOW, kvd)
    cache_v2 = cache_v.reshape(depth, nseq, WINDOW, kvd)
    ln1g, ln1b = ln1_g.reshape(depth, 1, d), ln1_b.reshape(depth, 1, d)
    ln2g, ln2b = ln2_g.reshape(depth, 1, d), ln2_b.reshape(depth, 1, d)

    n_items = pl.cdiv(assign, EXPERT_TILE) + ne
    slot_len = pl.cdiv(assign, SLOT_BLOCK) * SLOT_BLOCK
    items = jnp.arange(n_items, dtype=jnp.int32)

    xp = x_prompt.reshape(mp, d)
    xs_ = x_sample.reshape(ms, d)
    outs = [[] for _ in range(6)]
    for l in range(depth):
        conv_p, cst_p = _conv_proj(xp, w_in, conv_w, l, 512, cdim, seq=seq, batch=batch)
        q_p, k_p, v_p, kw_p, vw_p = _qkv_proj(xp, w_in, l, tabs_p, 512, cdim, adim, kvd, seq=seq, batch=batch)
        attn_p = _attn_prompt(sinks, q_p, k_p, v_p, l, batch, seq)

        st = state_conv[l]
        st1 = jnp.repeat(st[:, 1], dec_seq, axis=0)
        st2 = jnp.stack([st[:, 0], st[:, 1]] + [st[:, 1]] * (dec_seq - 2), axis=1).reshape(ms, cdim)
        conv_s, inner_s = _conv_proj(xs_, w_in, conv_w, l, ms, cdim, state=(st1, st2), dec_seq=dec_seq)
        q_s, kn_s, vn_s = _qkv_proj(xs_, w_in, l, tabs_s, ms, cdim, adim, kvd)
        attn_s, kw_s, vw_s = _attn_sample(sinks, q_s, kn_s, vn_s, cache_k2, cache_v2, l, dec_seq)

        zeros_cnt = jnp.zeros((ne, LANES), F32)
        *bufs, cnt = _mix_out(conv_p, attn_p, xp, w_o, ln1g, ln1b, wr_t, br, zeros_cnt, l, alpha, 256, total, 0)
        x1, x1p, route, cnt = _mix_out(conv_s, attn_s, xs_, w_o, ln1g, ln1b, wr_t, br, cnt,
                                       l, alpha, ms, total, mp // ms, prev=bufs)

        e = route[0:TOP_K].astype(jnp.int32)
        rank = route[4:4 + TOP_K].astype(jnp.int32)
        gates = route[2:2 + TOP_K][:, :, None]
        counts = cnt[:, 0].astype(jnp.int32)
        tiles_e = (counts + EXPERT_TILE - 1) // EXPERT_TILE
        tile_end = jnp.cumsum(tiles_e)
        tile_start = tile_end - tiles_e
        onehot = e[:, :, None] == jnp.arange(ne)[None, None, :]
        pos = jnp.sum(jnp.where(onehot, (tile_start * EXPERT_TILE)[None, None, :], 0), axis=-1) + rank
        slot_flat = jnp.pad(pos.T.reshape(-1), (0, slot_len - assign))
        n_used = tile_end[-1:].astype(jnp.int32)
        item_e = jnp.minimum(jnp.sum(tile_end[None, :] <= items[:, None], axis=1), ne - 1).astype(jnp.int32)
        item_nv = jnp.clip(counts[item_e] - (items - tile_start[item_e]) * EXPERT_TILE, 0, EXPERT_TILE)

        xsorted = _scatter_rows(x1p, slot_flat, n_items * EXPERT_TILE)
        yb = _expert_ffn(xsorted, w_gate, w_up, w_down, item_e, item_nv.astype(jnp.int32), n_used, l)

        xp = _combine(x1, yb, slot_flat, gates, ln2g, ln2b, l, alpha, 512, 0, mp)
        xs_ = _combine(x1, yb, slot_flat, gates, ln2g, ln2b, l, alpha, ms, mp, ms)

        inner3 = inner_s.reshape(nseq, dec_seq, cdim)
        for lst, val in zip(outs, (cst_p, kw_p.reshape(batch, WINDOW, N_KV_HEADS, HEAD_DIM),
                                   vw_p.reshape(batch, WINDOW, N_KV_HEADS, HEAD_DIM),
                                   inner3[:, dec_seq - 2:],
                                   kw_s.reshape(nseq, WINDOW, N_KV_HEADS, HEAD_DIM),
                                   vw_s.reshape(nseq, WINDOW, N_KV_HEADS, HEAD_DIM))):
            lst.append(val)

    return (xp.reshape(batch, seq, d), xs_.reshape(nseq, dec_seq, d), *[jnp.stack(o) for o in outs])
```

```python
import functools
import math

import jax
import jax.numpy as jnp
from jax import lax
from jax.experimental import pallas as pl
from jax.experimental.pallas import tpu as pltpu

F32 = jnp.float32
BF16 = jnp.bfloat16

HEAD_DIM = 64
N_KV_HEADS = 4
ROT_DIM = HEAD_DIM // 4
ROPE_THETA = 500000.0
WINDOW = 128
PAST_LEN = 16384
N_EXPERT_GROUPS = 4
TOP_K = 2
LN_EPS = 1e-5
QK_SCALE = HEAD_DIM ** -0.5
assert math.frexp(QK_SCALE)[0] == 0.5

LANES = 128
VMEM_LIMIT = 56 * 1024 * 1024

EXPERT_TILE = 640
FF_CHUNK = 512
OUT_CHUNK = 1024
WEIGHT_LOOKAHEAD = 2
CONV_GROUP = 512
ROUTE_ROWS = 8
SAMPLE_SEQS_PER_STEP = 4
SLOT_BLOCK = 1024
ISSUE_UNROLL = 8
PROJ_ROWS = 512
MIX_ROWS = 256
COMBINE_ROWS = 512


def _params(*sem):
    return pltpu.CompilerParams(dimension_semantics=sem, vmem_limit_bytes=VMEM_LIMIT)


def _rope_chunk(xc, cs, sa, sb):
    return xc * cs + pltpu.roll(xc, LANES - ROT_DIM // 2, 1) * sa + pltpu.roll(xc, ROT_DIM // 2, 1) * sb


def _conv(inner, gate, w_ref, prev1, prev2):
    return gate * (w_ref[0:1, :] * prev2 + w_ref[1:2, :] * prev1 + w_ref[2:3, :] * inner)


def _paired_head_order(n_heads):
    group = n_heads // N_KV_HEADS
    return [(2 * pair + half) * group + j
            for pair in range(N_KV_HEADS // 2) for j in range(group) for half in range(2)]


def _pair_q_chunks(chunks):
    order = _paired_head_order(2 * len(chunks))
    lo = lax.broadcasted_iota(jnp.int32, chunks[0].shape, 1) < HEAD_DIM
    out = []
    for c in range(len(chunks)):
        halves = []
        for half, head in enumerate(order[2 * c:2 * c + 2]):
            src = chunks[head // 2]
            halves.append(src if head % 2 == half else pltpu.roll(src, HEAD_DIM, 1))
        out.append(jnp.where(lo, halves[0], halves[1]))
    return out


def _conv_proj_kernel(x_ref, wc_ref, wb_ref, wu_ref, cw_ref, *refs, tiles_per_seq, dec_seq):
    if dec_seq is None:
        conv_o, cst_o, wbf, carry = refs
    else:
        st1, st2, conv_o, inner_o, wbf = refs
    i = pl.program_id(1)

    @pl.when(i == 0)
    def _():
        for k, w in enumerate((wc_ref, wb_ref, wu_ref)):
            wbf[k] = w[...].astype(BF16)

    xb = x_ref[...].astype(BF16)
    c, gate, u = (jnp.dot(xb, wbf[k], preferred_element_type=F32) for k in range(3))
    inner = c * u
    tm, cw = inner.shape
    row = lax.broadcasted_iota(jnp.int32, (tm, cw), 0)
    if dec_seq is None:
        @pl.when(i % tiles_per_seq == 0)
        def _():
            carry[...] = jnp.zeros_like(carry)

        c1, c2 = carry[7:8, :], carry[6:7, :]
        prev1 = jnp.where(row == 0, c1, pltpu.roll(inner, 1, 0))
        prev2 = jnp.where(row == 0, c2, jnp.where(row == 1, c1, pltpu.roll(inner, 2, 0)))
        carry[...] = inner[tm - 8:, :]
        cst_o[...] = inner[tm - 2:, :]
    else:
        t = row % dec_seq
        prev1 = jnp.where(t == 0, st1[...], pltpu.roll(inner, 1, 0))
        prev2 = jnp.where(t < 2, st2[...], pltpu.roll(inner, 2, 0))
        inner_o[...] = inner
    conv_o[...] = _conv(inner, gate, cw_ref, prev1, prev2).astype(BF16)


def _conv_proj(x, w_in, conv_w, l, tm, cdim, seq=None, batch=None, state=None, dec_seq=None):
    m, k = x.shape
    cw = CONV_GROUP
    groups = cdim // cw
    sec = lambda s: pl.BlockSpec((None, k, cw), lambda g, i: (l, 0, s * groups + g))
    blk = pl.BlockSpec((tm, cw), lambda g, i: (i, g))
    in_specs = [pl.BlockSpec((tm, k), lambda g, i: (i, 0)), sec(0), sec(1), sec(2),
                pl.BlockSpec((None, 3, cw), lambda g, i: (l, 0, g))]
    scratch = [pltpu.VMEM((3, k, cw), BF16)]
    if state is None:
        tps = seq // tm
        args = ()
        out_specs = [blk, pl.BlockSpec((None, 2, cw), lambda g, i: (i // tps, 0, g))]
        out_shape = [jax.ShapeDtypeStruct((m, cdim), BF16), jax.ShapeDtypeStruct((batch, 2, cdim), F32)]
        scratch.append(pltpu.VMEM((8, cw), F32))
    else:
        tps = None
        args = state
        in_specs += [blk, blk]
        out_specs = [blk, blk]
        out_shape = [jax.ShapeDtypeStruct((m, cdim), BF16), jax.ShapeDtypeStruct((m, cdim), F32)]
    return pl.pallas_call(
        functools.partial(_conv_proj_kernel, tiles_per_seq=tps, dec_seq=dec_seq),
        grid=(groups, m // tm),
        in_specs=in_specs, out_specs=out_specs, out_shape=out_shape, scratch_shapes=scratch,
        compiler_params=_params("arbitrary", "arbitrary"),
        name="conv_proj_prompt" if state is None else "conv_proj_sample",
    )(x, w_in, w_in, w_in, conv_w, *args)


def _qkv_proj_kernel(x_ref, w_ref, cs_ref, sa_ref, sb_ref, q_o, k_o, v_o, *refs, adim, q_scale):
    wbf = refs[-1]

    @pl.when(pl.program_id(0) == 0)
    def _():
        wbf[...] = w_ref[...].astype(BF16)

    h = jnp.dot(x_ref[...].astype(BF16), wbf[...], preferred_element_type=F32)
    tm = h.shape[0]
    kvd = (h.shape[1] - adim) // 2
    cs, sa, sb = cs_ref[...], sa_ref[...], sb_ref[...]
    roped = [_rope_chunk(h[:, c * LANES:(c + 1) * LANES], cs, sa, sb) for c in range((adim + kvd) // LANES)]
    for c, qc in enumerate(_pair_q_chunks(roped[:adim // LANES])):
        q_o[:, c * LANES:(c + 1) * LANES] = (qc * q_scale).astype(q_o.dtype)
    for c, kc in enumerate(roped[adim // LANES:]):
        k_o[:, c * LANES:(c + 1) * LANES] = kc.astype(k_o.dtype)
    v = h[:, adim + kvd:]
    v_o[...] = v.astype(v_o.dtype)
    if len(refs) > 1:
        kw_o, vw_o = refs[:2]
        for c, kc in enumerate(roped[adim // LANES:]):
            kw_o[:, c * LANES:(c + 1) * LANES] = kc[tm - WINDOW:, :]
        vw_o[...] = v[tm - WINDOW:, :]


def _qkv_proj(x, w_in, l, tabs, tm, cdim, adim, kvd, seq=None, batch=None):
    m, k = x.shape
    n = adim + 2 * kvd
    row = lambda i: (i, 0)
    prompt = seq is not None
    tps = seq // tm if prompt else 1
    tab = pl.BlockSpec((tm, LANES), lambda i: (i % tps, 0))
    dt = BF16 if prompt else F32
    out_specs = [pl.BlockSpec((tm, adim), row), pl.BlockSpec((tm, kvd), row), pl.BlockSpec((tm, kvd), row)]
    out_shape = [jax.ShapeDtypeStruct((m, adim), dt), jax.ShapeDtypeStruct((m, kvd), dt),
                 jax.ShapeDtypeStruct((m, kvd), dt)]
    if prompt:
        win = pl.BlockSpec((None, WINDOW, kvd), lambda i: (i // tps, 0, 0))
        out_specs += [win, win]
        out_shape += [jax.ShapeDtypeStruct((batch, WINDOW, kvd), F32)] * 2
    return pl.pallas_call(
        functools.partial(_qkv_proj_kernel, adim=adim, q_scale=QK_SCALE if prompt else 1.0),
        grid=(m // tm,),
        in_specs=[pl.BlockSpec((tm, k), row),
                  pl.BlockSpec((None, k, n), lambda i: (l, 0, 3 * cdim // n), pipeline_mode=pl.Buffered(1)),
                  tab, tab, tab],
        out_specs=out_specs, out_shape=out_shape,
        scratch_shapes=[pltpu.VMEM((k, n), BF16)],
        compiler_params=_params("arbitrary"),
        name="qkv_proj_prompt" if prompt else "qkv_proj_sample",
    )(x, w_in, *tabs)


def _sink_softmax(s, mask, sink_col):
    s = jnp.where(mask, s, -jnp.inf)
    m = jnp.maximum(jnp.max(s, axis=-1, keepdims=True), sink_col)
    p = jnp.exp(s - m)
    den = jnp.sum(p, axis=-1, keepdims=True) + jnp.exp(sink_col - m)
    return p * (1.0 / den)


def _attn_prompt_kernel(sink_ref, q_ref, kp_ref, kc_ref, vp_ref, vc_ref, o_ref, bias_ref, *, l):
    n = pl.program_id(1)
    w = q_ref.shape[0]
    chunks = q_ref.shape[1] // LANES
    pairs = kc_ref.shape[1] // LANES
    cpp = chunks // pairs
    lo = lax.broadcasted_iota(jnp.int32, (w, LANES), 1) < HEAD_DIM
    qi = lax.broadcasted_iota(jnp.int32, (w, 2 * w), 0)
    sj = lax.broadcasted_iota(jnp.int32, (w, 2 * w), 1)
    mask = (sj <= qi + w) & (sj > qi) & ((n > 0) | (sj >= w))
    bias_ref[...] = jnp.where(mask, 0.0, -jnp.inf)
    zero = jnp.zeros((w, LANES), BF16)
    for p in range(pairs):
        ksl = slice(p * LANES, (p + 1) * LANES)
        kk = jnp.concatenate([kp_ref[:, ksl], kc_ref[:, ksl]], axis=0)
        vv = jnp.concatenate([vp_ref[:, ksl], vc_ref[:, ksl]], axis=0)
        for j in range(cpp):
            c = p * cpp + j
            qc = q_ref[:, c * LANES:(c + 1) * LANES]
            halves = []
            for half, qh in enumerate((jnp.where(lo, qc, zero), jnp.where(lo, zero, qc))):
                sink = sink_ref[l, 2 * c + half]
                s = lax.dot_general(qh, kk, (((1,), (1,)), ((), ())), preferred_element_type=F32) + bias_ref[...]
                m = jnp.maximum(jnp.max(s, axis=-1, keepdims=True), sink)
                pexp = jnp.exp(s - m)
                den = jnp.sum(pexp, axis=-1, keepdims=True) + jnp.exp(sink - m)
                pr = (pexp * (1.0 / den)).astype(BF16)
                halves.append(jnp.dot(pr, vv, preferred_element_type=F32))
            o_ref[:, c * LANES:(c + 1) * LANES] = jnp.where(lo, halves[0], halves[1]).astype(BF16)


def _attn_prompt(sinks, q, k, v, l, batch, seq):
    m, adim = q.shape
    kvd = k.shape[1]
    nb = seq // WINDOW
    cur = lambda b, n: (b * nb + n, 0)
    prev = lambda b, n: (b * nb + jnp.maximum(n - 1, 0), 0)
    return pl.pallas_call(
        functools.partial(_attn_prompt_kernel, l=l),
        grid=(batch, nb),
        in_specs=[pl.BlockSpec(memory_space=pltpu.SMEM),
                  pl.BlockSpec((WINDOW, adim), cur),
                  pl.BlockSpec((WINDOW, kvd), prev), pl.BlockSpec((WINDOW, kvd), cur),
                  pl.BlockSpec((WINDOW, kvd), prev), pl.BlockSpec((WINDOW, kvd), cur)],
        out_specs=pl.BlockSpec((WINDOW, adim), cur),
        out_shape=jax.ShapeDtypeStruct((m, adim), BF16),
        scratch_shapes=[pltpu.VMEM((WINDOW, 2 * WINDOW), F32)],
        compiler_params=_params("arbitrary", "arbitrary"),
        name="attn_prompt",
    )(sinks, q, k, k, v, v)


def _attn_sample_kernel(sink_ref, q_ref, kn_ref, vn_ref, ck_ref, cv_ref,
                        o_ref, kw_ref, vw_ref, kk_ref, vv_ref, qs_ref, oacc_ref, *, l, t):
    nsq, w, kvd = ck_ref.shape
    tot = kk_ref.shape[1]
    chunks = q_ref.shape[1] // LANES
    pairs = kvd // LANES
    cpp = chunks // pairs
    rows = 2 * cpp * t
    lo = lax.broadcasted_iota(jnp.int32, (t, LANES), 1) < HEAD_DIM
    qi = lax.broadcasted_iota(jnp.int32, (rows, tot), 0) % t
    sj = lax.broadcasted_iota(jnp.int32, (rows, tot), 1)
    mask = (sj <= qi + w) & (sj > qi)
    rowid = lax.broadcasted_iota(jnp.int32, (rows, 1), 0) // t
    pad = jnp.zeros((tot - w - t, kvd), F32)
    for g in range(nsq):
        tok = slice(g * t, (g + 1) * t)
        kk_ref[g, 0:w, :] = ck_ref[g]
        vv_ref[g, 0:w, :] = cv_ref[g]
        kk_ref[g, w:w + t, :] = kn_ref[tok, :]
        vv_ref[g, w:w + t, :] = vn_ref[tok, :]
        kk_ref[g, w + t:, :] = pad
        vv_ref[g, w + t:, :] = pad
        kw_ref[g] = pltpu.roll(kk_ref[g], tot - t, 0)[0:w, :]
        vw_ref[g] = pltpu.roll(vv_ref[g], tot - t, 0)[0:w, :]
        for p in range(pairs):
            ksl = slice(p * LANES, (p + 1) * LANES)
            sink_col = jnp.zeros((rows, 1), F32)
            for j in range(cpp):
                c = p * cpp + j
                qc = q_ref[tok, c * LANES:(c + 1) * LANES]
                qs_ref[g, (2 * j) * t:(2 * j + 1) * t, :] = jnp.where(lo, qc, 0.0)
                qs_ref[g, (2 * j + 1) * t:(2 * j + 2) * t, :] = jnp.where(lo, 0.0, qc)
                sink_col = jnp.where(rowid == 2 * j, sink_ref[l, 2 * c], sink_col)
                sink_col = jnp.where(rowid == 2 * j + 1, sink_ref[l, 2 * c + 1], sink_col)
            s = lax.dot_general(qs_ref[g].astype(BF16), kk_ref[g, :, ksl].astype(BF16),
                                (((1,), (1,)), ((), ())), preferred_element_type=F32) * QK_SCALE
            pr = _sink_softmax(s, mask, sink_col)
            qs_ref[g] = jnp.dot(pr.astype(BF16), vv_ref[g, :, ksl].astype(BF16), preferred_element_type=F32)
            for j in range(cpp):
                c = p * cpp + j
                oacc_ref[tok, c * LANES:(c + 1) * LANES] = jnp.where(
                    lo, qs_ref[g, (2 * j) * t:(2 * j + 1) * t, :], qs_ref[g, (2 * j + 1) * t:(2 * j + 2) * t, :])
    o_ref[...] = oacc_ref[...].astype(BF16)


def _attn_sample(sinks, q, kn, vn, cache_k, cache_v, l, t):
    m, adim = q.shape
    nseq, w, kvd = cache_k.shape[1:]
    nsq = SAMPLE_SEQS_PER_STEP
    tot = -(-(w + t) // 8) * 8
    rows = 2 * (adim // kvd) * t
    row = lambda b: (b, 0)
    seq3 = lambda b: (b, 0, 0)
    cache = lambda b: (l, b, 0, 0)
    return pl.pallas_call(
        functools.partial(_attn_sample_kernel, l=l, t=t),
        grid=(nseq // nsq,),
        in_specs=[pl.BlockSpec(memory_space=pltpu.SMEM),
                  pl.BlockSpec((nsq * t, adim), row),
                  pl.BlockSpec((nsq * t, kvd), row), pl.BlockSpec((nsq * t, kvd), row),
                  pl.BlockSpec((None, nsq, w, kvd), cache), pl.BlockSpec((None, nsq, w, kvd), cache)],
        out_specs=[pl.BlockSpec((nsq * t, adim), row),
                   pl.BlockSpec((nsq, w, kvd), seq3), pl.BlockSpec((nsq, w, kvd), seq3)],
        out_shape=[jax.ShapeDtypeStruct((m, adim), BF16),
                   jax.ShapeDtypeStruct((nseq, w, kvd), F32),
                   jax.ShapeDtypeStruct((nseq, w, kvd), F32)],
        scratch_shapes=[pltpu.VMEM((nsq, tot, kvd), F32), pltpu.VMEM((nsq, tot, kvd), F32),
                        pltpu.VMEM((nsq, rows, LANES), F32), pltpu.VMEM((nsq * t, adim), F32)],
        compiler_params=_params("arbitrary"),
        name="attn_sample",
    )(sinks, q, kn, vn, cache_k, cache_v)


def _layer_norm(z, g, b):
    mu = jnp.mean(z, axis=-1, keepdims=True)
    d = z - mu
    var = jnp.mean(d * d, axis=-1, keepdims=True)
    return d * lax.rsqrt(var + LN_EPS) * g + b


def _top2(sg, idx, n):
    m1 = jnp.max(sg, axis=0, keepdims=True)
    i1 = jnp.min(jnp.where(sg == m1, idx, n), axis=0, keepdims=True)
    rest = jnp.where(idx == i1, -jnp.inf, sg)
    m2 = jnp.max(rest, axis=0, keepdims=True)
    i2 = jnp.min(jnp.where(rest == m2, idx, n), axis=0, keepdims=True)
    return m1, i1, m2, i2


def _pack_bf16_pairs(x):
    half = x.shape[1] // 2
    lo = pltpu.bitcast(x[:, :half].astype(BF16).astype(F32), jnp.uint32)
    hi = pltpu.bitcast(x[:, half:].astype(BF16).astype(F32), jnp.uint32)
    return hi | (lo >> 16)


def _unpack_bf16_pairs(w):
    lo = pltpu.bitcast(w << 16, F32).astype(BF16)
    hi = pltpu.bitcast(w & jnp.uint32(0xFFFF0000), F32).astype(BF16)
    return lo, hi


def _mix_out_kernel(conv_ref, attn_ref, x_ref, wo_ref, g_ref, b_ref, wr_ref, br_ref, cnt_in,
                    x1_ref, x1p_ref, route_ref, cnt_ref, carry, wob, wrb, *, alpha):
    cdim = conv_ref.shape[1]

    @pl.when(pl.program_id(0) == 0)
    def _():
        carry[...] = cnt_in[...]
        wrb[...] = wr_ref[...].astype(BF16)
        wob[0:cdim, :] = wo_ref[0:cdim, :].astype(BF16)
        order = _paired_head_order((wo_ref.shape[0] - cdim) // HEAD_DIM)
        for pos, head in enumerate(order):
            wob[cdim + pos * HEAD_DIM:cdim + (pos + 1) * HEAD_DIM, :] = (
                wo_ref[cdim + head * HEAD_DIM:cdim + (head + 1) * HEAD_DIM, :].astype(BF16))

    y = jnp.dot(conv_ref[...], wob[0:cdim, :], preferred_element_type=F32)
    y = y + jnp.dot(attn_ref[...], wob[cdim:, :], preferred_element_type=F32)
    x1 = _layer_norm(alpha * x_ref[...] + y, g_ref[...], b_ref[...])
    x1_ref[...] = x1
    x1p_ref[...] = _pack_bf16_pairs(x1)
    logits = lax.dot_general(wrb[...], x1.astype(BF16), (((1,), (1,)), ((), ())), preferred_element_type=F32)
    route_ref[...], counts = _route(logits, br_ref[...], carry[...])
    carry[...] = counts
    cnt_ref[...] = counts


def _route(logits, bias, counts):
    ne, tm = logits.shape
    epg = ne // N_EXPERT_GROUPS
    lanes = lambda a: jnp.concatenate([a] * (tm // LANES), axis=1)
    scores = jax.nn.sigmoid(logits)
    sel = scores + lanes(bias)
    eidx = lax.broadcasted_iota(jnp.int32, (ne, tm), 0).astype(F32)
    best = None
    for g in range(N_EXPERT_GROUPS):
        gidx = lax.broadcasted_iota(jnp.int32, (epg, tm), 0).astype(F32) + float(g * epg)
        m1, i1, m2, i2 = _top2(sel[g * epg:(g + 1) * epg], gidx, float(ne))
        cand = (m1 + m2, i1, i2)
        if best is None:
            best = cand
        else:
            upd = cand[0] > best[0]
            best = tuple(jnp.where(upd, new, old) for new, old in zip(cand, best))
    _, e1, e2 = best
    hit1 = eidx == e1
    hit2 = eidx == e2
    w1 = jnp.sum(jnp.where(hit1, scores, 0.0), axis=0, keepdims=True)
    w2 = jnp.sum(jnp.where(hit2, scores, 0.0), axis=0, keepdims=True)
    wsum = w1 + w2

    onehot = jnp.where(hit1 | hit2, 1.0, 0.0)
    earlier = (lax.broadcasted_iota(jnp.int32, (tm, tm), 0) < lax.broadcasted_iota(jnp.int32, (tm, tm), 1))
    before = jnp.dot(onehot.astype(BF16), jnp.where(earlier, 1.0, 0.0).astype(BF16), preferred_element_type=F32)
    before = before + lanes(counts)
    r1 = jnp.sum(jnp.where(hit1, before, 0.0), axis=0, keepdims=True)
    r2 = jnp.sum(jnp.where(hit2, before, 0.0), axis=0, keepdims=True)

    row = lax.broadcasted_iota(jnp.int32, (ROUTE_ROWS, tm), 0)
    out = jnp.zeros((ROUTE_ROWS, tm), F32)
    for c, val in enumerate((e1, e2, w1 / wsum, w2 / wsum, r1, r2)):
        out = jnp.where(row == c, val, out)
    return out, counts + jnp.sum(onehot, axis=1, keepdims=True)


def _mix_out(conv, attn, x, wo, ln_g, ln_b, wr_t, br, cnt_in, l, alpha, tm, total_rows, row_block0, prev=None):
    m, d = x.shape
    cdim = conv.shape[1]
    ne = wr_t.shape[0]
    rowi = lambda i: (i, 0)
    rowo = lambda i: (i + row_block0, 0)
    const = lambda i: (0, 0)
    lay = lambda i: (l, 0, 0)
    in_specs = [pl.BlockSpec((tm, cdim), rowi), pl.BlockSpec((tm, d - cdim), rowi), pl.BlockSpec((tm, d), rowi),
                pl.BlockSpec((None, d, d), lay, pipeline_mode=pl.Buffered(1)),
                pl.BlockSpec((None, 1, d), lay), pl.BlockSpec((None, 1, d), lay),
                pl.BlockSpec((ne, d), const), pl.BlockSpec((ne, LANES), const),
                pl.BlockSpec((ne, LANES), const)]
    args = [conv, attn, x, wo, ln_g, ln_b, wr_t, br, cnt_in]
    aliases = {}
    if prev is not None:
        in_specs += [pl.BlockSpec(memory_space=pl.ANY)] * len(prev)
        aliases = {len(args) + k: k for k in range(len(prev))}
        args += list(prev)

    def body(*refs):
        _mix_out_kernel(*refs[:9], *refs[-7:], alpha=alpha)

    return pl.pallas_call(
        body,
        grid=(m // tm,),
        in_specs=in_specs,
        out_specs=[pl.BlockSpec((tm, d), rowo), pl.BlockSpec((tm, d // 2), rowo),
                   pl.BlockSpec((ROUTE_ROWS, tm), lambda i: (0, i + row_block0)),
                   pl.BlockSpec((ne, LANES), const)],
        out_shape=[jax.ShapeDtypeStruct((total_rows, d), F32),
                   jax.ShapeDtypeStruct((total_rows, d // 2), jnp.uint32),
                   jax.ShapeDtypeStruct((ROUTE_ROWS, total_rows), F32),
                   jax.ShapeDtypeStruct((ne, LANES), F32)],
        scratch_shapes=[pltpu.VMEM((ne, LANES), F32), pltpu.VMEM((d, d), BF16), pltpu.VMEM((ne, d), BF16)],
        input_output_aliases=aliases,
        compiler_params=_params("arbitrary"),
        name="mix_out_sample" if prev is not None else "mix_out_prompt",
    )(*args)


def _scatter_rows_kernel(slot_ref, x_ref, o_ref, sem, *, n_tok):
    ct = x_ref.shape[0]
    c = pl.program_id(0)
    last = pl.num_programs(0) - 1
    tail = n_tok - (pl.cdiv(n_tok, ct) - 1) * ct

    def run(count):
        for j in range(count):
            for k in range(TOP_K):
                pltpu.make_async_copy(x_ref.at[pl.ds(j, 1)],
                                      o_ref.at[pl.ds(slot_ref[TOP_K * j + k], 1)], sem).start()
        for _ in range(TOP_K):
            pltpu.make_async_copy(x_ref.at[pl.ds(0, count)], x_ref.at[pl.ds(0, count)], sem).wait()

    if tail == ct:
        run(ct)
    else:
        pl.when(c < last)(lambda: run(ct))
        pl.when(c == last)(lambda: run(tail))


def _scatter_rows(x1, slot_flat, n_slots):
    total, d = x1.shape
    ct = SLOT_BLOCK // TOP_K
    return pl.pallas_call(
        functools.partial(_scatter_rows_kernel, n_tok=total),
        grid=(pl.cdiv(total, ct),),
        in_specs=[pl.BlockSpec((SLOT_BLOCK,), lambda c: (c,), memory_space=pltpu.SMEM),
                  pl.BlockSpec((ct, d), lambda c: (c, 0))],
        out_specs=pl.BlockSpec(memory_space=pl.ANY),
        out_shape=jax.ShapeDtypeStruct((n_slots, d), x1.dtype),
        scratch_shapes=[pltpu.SemaphoreType.DMA(())],
        compiler_params=_params("arbitrary"),
        name="scatter_rows",
    )(slot_flat, x1)


def _expert_kernel(te_ref, nv_ref, nu_ref, x_ref, wg_hbm, wu_hbm, wd_hbm, o_ref,
                   xb_ref, wgu_ref, h_ref, gu_buf, dn_buf, sems, *, l):
    nf, _, fc = h_ref.shape
    nc, _, oc = dn_buf.shape
    steps = nf + nc
    i = pl.program_id(0)
    s = pl.program_id(1)
    n_used = nu_ref[0]

    def copies(pos, e):
        if pos < nf:
            return [pltpu.make_async_copy(w.at[l, e, :, pos * fc:(pos + 1) * fc], gu_buf.at[pos, k], sems.at[pos])
                    for k, w in enumerate((wg_hbm, wu_hbm))]
        c = pos - nf
        return [pltpu.make_async_copy(wd_hbm.at[l, e, :, c * oc:(c + 1) * oc], dn_buf.at[c], sems.at[pos])]

    def start(pos, tile):
        @pl.when(tile < n_used)
        def _():
            for cp in copies(pos, te_ref[tile]):
                cp.start()

    def step_body(pos):
        if pos == 0:
            @pl.when(i == 0)
            def _():
                for first in range(WEIGHT_LOOKAHEAD):
                    start(first, i)

        ahead = pos + WEIGHT_LOOKAHEAD
        start(ahead % steps, i + ahead // steps)
        for cp in copies(pos, 0):
            cp.wait()

        if pos == 0:
            rid = lax.broadcasted_iota(jnp.int32, (x_ref.shape[0], 1), 0)
            lo, hi = _unpack_bf16_pairs(jnp.where(rid < nv_ref[i], x_ref[...], jnp.uint32(0)))
            xb_ref[:, :x_ref.shape[1]] = lo
            xb_ref[:, x_ref.shape[1]:] = hi
        if pos < nf:
            wgu_ref[:, :fc] = gu_buf[pos, 0].astype(BF16)
            wgu_ref[:, fc:] = gu_buf[pos, 1].astype(BF16)
            gu = jnp.dot(xb_ref[...], wgu_ref[...], preferred_element_type=F32)
            h_ref[pos] = (jax.nn.silu(gu[:, :fc]) * gu[:, fc:]).astype(BF16)
        else:
            c = pos - nf
            y = jnp.dot(h_ref[0], dn_buf[c, 0:fc, :].astype(BF16), preferred_element_type=F32)
            for f in range(1, nf):
                y = y + jnp.dot(h_ref[f], dn_buf[c, f * fc:(f + 1) * fc, :].astype(BF16),
                                preferred_element_type=F32)
            o_ref[...] = y

    @pl.when(i < n_used)
    def _():
        for pos in range(steps):
            pl.when(s == pos)(functools.partial(step_body, pos))


def _expert_ffn(xs, w_gate, w_up, w_down, tile_e, tile_nv, n_used, l):
    rows = xs.shape[0]
    d, dff = w_gate.shape[2:]
    nf = dff // FF_CHUNK
    nc = d // OUT_CHUNK
    steps = nf + nc
    assert steps > WEIGHT_LOOKAHEAD
    nt = rows // EXPERT_TILE

    def item(i, nu):
        return jnp.minimum(i, jnp.maximum(nu[0] - 1, 0))

    def col(i, s, nu):
        return jnp.maximum(jnp.where(i < nu[0], s, steps - 1) - nf, 0)

    xmap = lambda i, s, te, nv, nu: (item(i, nu), 0)
    omap = lambda i, s, te, nv, nu: (item(i, nu), col(i, s, nu))
    hbm = pl.BlockSpec(memory_space=pl.ANY)
    return pl.pallas_call(
        functools.partial(_expert_kernel, l=l),
        grid_spec=pltpu.PrefetchScalarGridSpec(
            num_scalar_prefetch=3,
            grid=(nt, steps),
            in_specs=[pl.BlockSpec((EXPERT_TILE, xs.shape[1]), xmap), hbm, hbm, hbm],
            out_specs=pl.BlockSpec((EXPERT_TILE, OUT_CHUNK), omap),
            scratch_shapes=[pltpu.VMEM((EXPERT_TILE, d), BF16),
                            pltpu.VMEM((d, 2 * FF_CHUNK), BF16),
                            pltpu.VMEM((nf, EXPERT_TILE, FF_CHUNK), BF16),
                            pltpu.VMEM((nf, 2, d, FF_CHUNK), F32),
                            pltpu.VMEM((nc, dff, OUT_CHUNK), F32),
                            pltpu.SemaphoreType.DMA((steps,))]),
        out_shape=jax.ShapeDtypeStruct((rows, d), F32),
        compiler_params=_params("arbitrary", "arbitrary"),
        name="expert_ffn",
    )(tile_e, tile_nv, n_used, xs, w_gate, w_up, w_down)


def _combine_kernel(slot_cur, slot_nxt, x1_ref, r_ref, g_ref, b_ref, y_hbm, o_ref, ybuf, sems, *, alpha):
    i = pl.program_id(0)
    ct = x1_ref.shape[0]

    def fetch(slot_ref, buf):
        def issue(jb, carry):
            j0 = pl.multiple_of(jb * ISSUE_UNROLL, ISSUE_UNROLL)
            for u in range(ISSUE_UNROLL):
                for k in range(TOP_K):
                    pltpu.make_async_copy(y_hbm.at[pl.ds(slot_ref[TOP_K * (j0 + u) + k], 1)],
                                          ybuf.at[buf, k, pl.ds(j0 + u, 1)], sems.at[buf]).start()
            return carry

        lax.fori_loop(0, ct // ISSUE_UNROLL, issue, 0)

    @pl.when(i == 0)
    def _():
        fetch(slot_cur, 0)

    @pl.when(i + 1 < pl.num_programs(0))
    def _():
        fetch(slot_nxt, (i + 1) % 2)

    buf = i % 2
    for k in range(TOP_K):
        pltpu.make_async_copy(ybuf.at[buf, k], ybuf.at[buf, k], sems.at[buf]).wait()
    moe = ybuf[buf, 0] * r_ref[0] + ybuf[buf, 1] * r_ref[1]
    o_ref[...] = _layer_norm(alpha * x1_ref[...] + moe, g_ref[...], b_ref[...])


def _combine(x1, yb, slot_flat, gates, ln_g, ln_b, l, alpha, tm, row0, m):
    d = x1.shape[1]
    nsteps = m // tm
    rb0 = row0 // tm
    sb0 = row0 * TOP_K // SLOT_BLOCK
    sper = pl.cdiv(tm * TOP_K, SLOT_BLOCK)
    rowo = lambda i: (i + rb0, 0)
    lay = lambda i: (l, 0, 0)
    return pl.pallas_call(
        functools.partial(_combine_kernel, alpha=alpha),
        grid=(nsteps,),
        in_specs=[pl.BlockSpec((SLOT_BLOCK,), lambda i: (sb0 + i * sper,), memory_space=pltpu.SMEM),
                  pl.BlockSpec((SLOT_BLOCK,), lambda i: (sb0 + jnp.minimum(i + 1, nsteps - 1) * sper,),
                               memory_space=pltpu.SMEM),
                  pl.BlockSpec((tm, d), rowo),
                  pl.BlockSpec((TOP_K, tm, 1), lambda i: (0, i + rb0, 0)),
                  pl.BlockSpec((None, 1, d), lay), pl.BlockSpec((None, 1, d), lay),
                  pl.BlockSpec(memory_space=pl.ANY)],
        out_specs=pl.BlockSpec((tm, d), lambda i: (i, 0)),
        out_shape=jax.ShapeDtypeStruct((m, d), F32),
        scratch_shapes=[pltpu.VMEM((2, TOP_K, tm, d), F32), pltpu.SemaphoreType.DMA((2,))],
        compiler_params=_params("arbitrary"),
        name="combine",
    )(slot_flat, slot_flat, x1, gates, ln_g, ln_b, yb)


def _rope_tables(pos):
    half = ROT_DIM // 2
    inv = ROPE_THETA ** (-jnp.arange(half, dtype=F32) * 2.0 / ROT_DIM)
    ang = pos.astype(F32)[:, None] * inv[None, :]
    cos, sin = jnp.cos(ang), jnp.sin(ang)
    n = pos.shape[0]
    rest = HEAD_DIM - ROT_DIM
    cs = jnp.concatenate([cos, cos, jnp.ones((n, rest), F32)], axis=1)
    sa = jnp.concatenate([-sin, jnp.zeros((n, half + rest), F32)], axis=1)
    sb = jnp.concatenate([jnp.zeros((n, half), F32), sin, jnp.zeros((n, rest), F32)], axis=1)
    rep = LANES // HEAD_DIM
    return tuple(jnp.tile(t, (1, rep)) for t in (cs, sa, sb))


def kernel(x_prompt, x_sample, state_conv, cache_k, cache_v, w_in, conv_w, attn_sinks, w_o,
           ln1_g, ln1_b, w_router, b_router, w_gate, w_up, w_down, ln2_g, ln2_b):
    batch, seq, d = x_prompt.shape
    nseq, dec_seq, _ = x_sample.shape
    depth = w_in.shape[0]
    cdim = conv_w.shape[2]
    adim = d - cdim
    kvd = N_KV_HEADS * HEAD_DIM
    n_heads = adim // HEAD_DIM
    ne = w_router.shape[1]
    mp, ms = batch * seq, nseq * dec_seq
    total = mp + ms
    assign = total * TOP_K
    alpha = (2 * depth) ** 0.25

    sinks = attn_sinks[:, jnp.asarray(_paired_head_order(n_heads))]
    wr_t = w_router.T
    br = jnp.broadcast_to(b_router[:, None], (ne, LANES))

    tabs_p = _rope_tables(jnp.arange(seq))
    tabs_s = _rope_tables(jnp.tile(PAST_LEN + jnp.arange(dec_seq), nseq))
    cache_k2 = cache_k.reshape(depth, nseq, WINDOW, kvd)
    cache_v2 = cache_v.reshape(depth, nseq, WINDOW, kvd)
    ln1g, ln1b = ln1_g.reshape(depth, 1, d), ln1_b.reshape(depth, 1, d)
    ln2g, ln2b = ln2_g.reshape(depth, 1, d), ln2_b.reshape(depth, 1, d)

    n_items = pl.cdiv(assign, EXPERT_TILE) + ne
    slot_len = pl.cdiv(assign, SLOT_BLOCK) * SLOT_BLOCK
    items = jnp.arange(n_items, dtype=jnp.int32)

    xp = x_prompt.reshape(mp, d)
    xs_ = x_sample.reshape(ms, d)
    outs = [[] for _ in range(6)]
    for l in range(depth):
        conv_p, cst_p = _conv_proj(xp, w_in, conv_w, l, PROJ_ROWS, cdim, seq=seq, batch=batch)
        q_p, k_p, v_p, kw_p, vw_p = _qkv_proj(xp, w_in, l, tabs_p, PROJ_ROWS, cdim, adim, kvd, seq=seq, batch=batch)
        attn_p = _attn_prompt(sinks, q_p, k_p, v_p, l, batch, seq)

        st = state_conv[l]
        st1 = jnp.repeat(st[:, 1], dec_seq, axis=0)
        st2 = jnp.stack([st[:, 0], st[:, 1]] + [st[:, 1]] * (dec_seq - 2), axis=1).reshape(ms, cdim)
        conv_s, inner_s = _conv_proj(xs_, w_in, conv_w, l, ms, cdim, state=(st1, st2), dec_seq=dec_seq)
        q_s, kn_s, vn_s = _qkv_proj(xs_, w_in, l, tabs_s, ms, cdim, adim, kvd)
        attn_s, kw_s, vw_s = _attn_sample(sinks, q_s, kn_s, vn_s, cache_k2, cache_v2, l, dec_seq)

        zeros_cnt = jnp.zeros((ne, LANES), F32)
        *bufs, cnt = _mix_out(conv_p, attn_p, xp, w_o, ln1g, ln1b, wr_t, br, zeros_cnt, l, alpha, MIX_ROWS, total, 0)
        x1, x1p, route, cnt = _mix_out(conv_s, attn_s, xs_, w_o, ln1g, ln1b, wr_t, br, cnt,
                                       l, alpha, ms, total, mp // ms, prev=bufs)

        e = route[0:TOP_K].astype(jnp.int32)
        rank = route[4:4 + TOP_K].astype(jnp.int32)
        gates = route[2:2 + TOP_K][:, :, None]
        counts = cnt[:, 0].astype(jnp.int32)
        tiles_e = (counts + EXPERT_TILE - 1) // EXPERT_TILE
        tile_end = jnp.cumsum(tiles_e)
        tile_start = tile_end - tiles_e
        onehot = e[:, :, None] == jnp.arange(ne)[None, None, :]
        pos = jnp.sum(jnp.where(onehot, (tile_start * EXPERT_TILE)[None, None, :], 0), axis=-1) + rank
        slot_flat = jnp.pad(pos.T.reshape(-1), (0, slot_len - assign))
        n_used = tile_end[-1:].astype(jnp.int32)
        item_e = jnp.minimum(jnp.sum(tile_end[None, :] <= items[:, None], axis=1), ne - 1).astype(jnp.int32)
        item_nv = jnp.clip(counts[item_e] - (items - tile_start[item_e]) * EXPERT_TILE, 0, EXPERT_TILE)

        xsorted = _scatter_rows(x1p, slot_flat, n_items * EXPERT_TILE)
        yb = _expert_ffn(xsorted, w_gate, w_up, w_down, item_e, item_nv.astype(jnp.int32), n_used, l)

        xp = _combine(x1, yb, slot_flat, gates, ln2g, ln2b, l, alpha, COMBINE_ROWS, 0, mp)
        xs_ = _combine(x1, yb, slot_flat, gates, ln2g, ln2b, l, alpha, ms, mp, ms)

        inner3 = inner_s.reshape(nseq, dec_seq, cdim)
        for lst, val in zip(outs, (cst_p, kw_p.reshape(batch, WINDOW, N_KV_HEADS, HEAD_DIM),
                                   vw_p.reshape(batch, WINDOW, N_KV_HEADS, HEAD_DIM),
                                   inner3[:, dec_seq - 2:],
                                   kw_s.reshape(nseq, WINDOW, N_KV_HEADS, HEAD_DIM),
                                   vw_s.reshape(nseq, WINDOW, N_KV_HEADS, HEAD_DIM))):
            lst.append(val)

    return (xp.reshape(batch, seq, d), xs_.reshape(nseq, dec_seq, d), *[jnp.stack(o) for o in outs])
```

```python
import functools
import math

import jax
import jax.numpy as jnp
from jax import lax
from jax.experimental import pallas as pl
from jax.experimental.pallas import tpu as pltpu

F32 = jnp.float32
BF16 = jnp.bfloat16

HEAD_DIM = 64
N_KV_HEADS = 4
ROT_DIM = HEAD_DIM // 4
ROPE_THETA = 500000.0
WINDOW = 128
PAST_LEN = 16384
N_EXPERT_GROUPS = 4
TOP_K = 2
LN_EPS = 1e-5
QK_SCALE = HEAD_DIM ** -0.5
assert math.frexp(QK_SCALE)[0] == 0.5

LANES = 128
VMEM_LIMIT = 56 * 1024 * 1024

EXPERT_TILE = 640
FF_CHUNK = 512
OUT_CHUNK = 1024
WEIGHT_LOOKAHEAD = 3
CONV_GROUP = 512
ROUTE_ROWS = 8
SAMPLE_SEQS_PER_STEP = 4
SLOT_BLOCK = 1024
PROJ_ROWS = 512
MIX_ROWS = 256
COMBINE_ROWS = 512


def _params(*sem):
    return pltpu.CompilerParams(dimension_semantics=sem, vmem_limit_bytes=VMEM_LIMIT)


def _rope_chunk(xc, cs, sa, sb):
    return xc * cs + pltpu.roll(xc, LANES - ROT_DIM // 2, 1) * sa + pltpu.roll(xc, ROT_DIM // 2, 1) * sb


def _conv(inner, gate, w_ref, prev1, prev2):
    return gate * (w_ref[0:1, :] * prev2 + w_ref[1:2, :] * prev1 + w_ref[2:3, :] * inner)


def _paired_head_order(n_heads):
    group = n_heads // N_KV_HEADS
    return [(2 * pair + half) * group + j
            for pair in range(N_KV_HEADS // 2) for j in range(group) for half in range(2)]


def _pair_q_chunks(chunks):
    order = _paired_head_order(2 * len(chunks))
    lo = lax.broadcasted_iota(jnp.int32, chunks[0].shape, 1) < HEAD_DIM
    out = []
    for c in range(len(chunks)):
        halves = []
        for half, head in enumerate(order[2 * c:2 * c + 2]):
            src = chunks[head // 2]
            halves.append(src if head % 2 == half else pltpu.roll(src, HEAD_DIM, 1))
        out.append(jnp.where(lo, halves[0], halves[1]))
    return out


def _conv_proj_kernel(x_ref, wc_ref, wb_ref, wu_ref, cw_ref, *refs, tiles_per_seq, dec_seq):
    if dec_seq is None:
        conv_o, cst_o, wbf, carry = refs
    else:
        st1, st2, conv_o, inner_o, wbf = refs
    i = pl.program_id(1)

    @pl.when(i == 0)
    def _():
        for k, w in enumerate((wc_ref, wb_ref, wu_ref)):
            wbf[k] = w[...].astype(BF16)

    xb = x_ref[...].astype(BF16)
    c, gate, u = (jnp.dot(xb, wbf[k], preferred_element_type=F32) for k in range(3))
    inner = c * u
    tm, cw = inner.shape
    row = lax.broadcasted_iota(jnp.int32, (tm, cw), 0)
    if dec_seq is None:
        @pl.when(i % tiles_per_seq == 0)
        def _():
            carry[...] = jnp.zeros_like(carry)

        c1, c2 = carry[7:8, :], carry[6:7, :]
        prev1 = jnp.where(row == 0, c1, pltpu.roll(inner, 1, 0))
        prev2 = jnp.where(row == 0, c2, jnp.where(row == 1, c1, pltpu.roll(inner, 2, 0)))
        carry[...] = inner[tm - 8:, :]
        cst_o[...] = inner[tm - 2:, :]
    else:
        t = row % dec_seq
        prev1 = jnp.where(t == 0, st1[...], pltpu.roll(inner, 1, 0))
        prev2 = jnp.where(t < 2, st2[...], pltpu.roll(inner, 2, 0))
        inner_o[...] = inner
    conv_o[...] = _conv(inner, gate, cw_ref, prev1, prev2).astype(BF16)


def _conv_proj(x, w_in, conv_w, l, tm, cdim, seq=None, batch=None, state=None, dec_seq=None):
    m, k = x.shape
    cw = CONV_GROUP
    groups = cdim // cw
    sec = lambda s: pl.BlockSpec((None, k, cw), lambda g, i: (l, 0, s * groups + g))
    blk = pl.BlockSpec((tm, cw), lambda g, i: (i, g))
    in_specs = [pl.BlockSpec((tm, k), lambda g, i: (i, 0)), sec(0), sec(1), sec(2),
                pl.BlockSpec((None, 3, cw), lambda g, i: (l, 0, g))]
    scratch = [pltpu.VMEM((3, k, cw), BF16)]
    if state is None:
        tps = seq // tm
        args = ()
        out_specs = [blk, pl.BlockSpec((None, 2, cw), lambda g, i: (i // tps, 0, g))]
        out_shape = [jax.ShapeDtypeStruct((m, cdim), BF16), jax.ShapeDtypeStruct((batch, 2, cdim), F32)]
        scratch.append(pltpu.VMEM((8, cw), F32))
    else:
        tps = None
        args = state
        in_specs += [blk, blk]
        out_specs = [blk, blk]
        out_shape = [jax.ShapeDtypeStruct((m, cdim), BF16), jax.ShapeDtypeStruct((m, cdim), F32)]
    return pl.pallas_call(
        functools.partial(_conv_proj_kernel, tiles_per_seq=tps, dec_seq=dec_seq),
        grid=(groups, m // tm),
        in_specs=in_specs, out_specs=out_specs, out_shape=out_shape, scratch_shapes=scratch,
        compiler_params=_params("arbitrary", "arbitrary"),
        name="conv_proj_prompt" if state is None else "conv_proj_sample",
    )(x, w_in, w_in, w_in, conv_w, *args)


def _qkv_proj_kernel(x_ref, w_ref, cs_ref, sa_ref, sb_ref, q_o, k_o, v_o, *refs, adim, q_scale):
    wbf = refs[-1]

    @pl.when(pl.program_id(0) == 0)
    def _():
        wbf[...] = w_ref[...].astype(BF16)

    h = jnp.dot(x_ref[...].astype(BF16), wbf[...], preferred_element_type=F32)
    tm = h.shape[0]
    kvd = (h.shape[1] - adim) // 2
    cs, sa, sb = cs_ref[...], sa_ref[...], sb_ref[...]
    roped = [_rope_chunk(h[:, c * LANES:(c + 1) * LANES], cs, sa, sb) for c in range((adim + kvd) // LANES)]
    for c, qc in enumerate(_pair_q_chunks(roped[:adim // LANES])):
        q_o[:, c * LANES:(c + 1) * LANES] = (qc * q_scale).astype(q_o.dtype)
    for c, kc in enumerate(roped[adim // LANES:]):
        k_o[:, c * LANES:(c + 1) * LANES] = kc.astype(k_o.dtype)
    v = h[:, adim + kvd:]
    v_o[...] = v.astype(v_o.dtype)
    if len(refs) > 1:
        kw_o, vw_o = refs[:2]
        for c, kc in enumerate(roped[adim // LANES:]):
            kw_o[:, c * LANES:(c + 1) * LANES] = kc[tm - WINDOW:, :]
        vw_o[...] = v[tm - WINDOW:, :]


def _qkv_proj(x, w_in, l, tabs, tm, cdim, adim, kvd, seq=None, batch=None):
    m, k = x.shape
    n = adim + 2 * kvd
    row = lambda i: (i, 0)
    prompt = seq is not None
    tps = seq // tm if prompt else 1
    tab = pl.BlockSpec((tm, LANES), lambda i: (i % tps, 0))
    dt = BF16 if prompt else F32
    out_specs = [pl.BlockSpec((tm, adim), row), pl.BlockSpec((tm, kvd), row), pl.BlockSpec((tm, kvd), row)]
    out_shape = [jax.ShapeDtypeStruct((m, adim), dt), jax.ShapeDtypeStruct((m, kvd), dt),
                 jax.ShapeDtypeStruct((m, kvd), dt)]
    if prompt:
        win = pl.BlockSpec((None, WINDOW, kvd), lambda i: (i // tps, 0, 0))
        out_specs += [win, win]
        out_shape += [jax.ShapeDtypeStruct((batch, WINDOW, kvd), F32)] * 2
    return pl.pallas_call(
        functools.partial(_qkv_proj_kernel, adim=adim, q_scale=QK_SCALE if prompt else 1.0),
        grid=(m // tm,),
        in_specs=[pl.BlockSpec((tm, k), row),
                  pl.BlockSpec((None, k, n), lambda i: (l, 0, 3 * cdim // n), pipeline_mode=pl.Buffered(1)),
                  tab, tab, tab],
        out_specs=out_specs, out_shape=out_shape,
        scratch_shapes=[pltpu.VMEM((k, n), BF16)],
        compiler_params=_params("arbitrary"),
        name="qkv_proj_prompt" if prompt else "qkv_proj_sample",
    )(x, w_in, *tabs)


def _sink_softmax(s, mask, sink_col):
    s = jnp.where(mask, s, -jnp.inf)
    m = jnp.maximum(jnp.max(s, axis=-1, keepdims=True), sink_col)
    p = jnp.exp(s - m)
    den = jnp.sum(p, axis=-1, keepdims=True) + jnp.exp(sink_col - m)
    return p * (1.0 / den)


def _attn_prompt_kernel(sink_ref, q_ref, kp_ref, kc_ref, vp_ref, vc_ref, o_ref, bias_ref, *, l):
    n = pl.program_id(1)
    w = q_ref.shape[0]
    chunks = q_ref.shape[1] // LANES
    pairs = kc_ref.shape[1] // LANES
    cpp = chunks // pairs
    lo = lax.broadcasted_iota(jnp.int32, (w, LANES), 1) < HEAD_DIM
    qi = lax.broadcasted_iota(jnp.int32, (w, 2 * w), 0)
    sj = lax.broadcasted_iota(jnp.int32, (w, 2 * w), 1)
    mask = (sj <= qi + w) & (sj > qi) & ((n > 0) | (sj >= w))
    bias_ref[...] = jnp.where(mask, 0.0, -jnp.inf)
    zero = jnp.zeros((w, LANES), BF16)
    for p in range(pairs):
        ksl = slice(p * LANES, (p + 1) * LANES)
        kk = jnp.concatenate([kp_ref[:, ksl], kc_ref[:, ksl]], axis=0)
        vv = jnp.concatenate([vp_ref[:, ksl], vc_ref[:, ksl]], axis=0)
        for j in range(cpp):
            c = p * cpp + j
            qc = q_ref[:, c * LANES:(c + 1) * LANES]
            halves = []
            for half, qh in enumerate((jnp.where(lo, qc, zero), jnp.where(lo, zero, qc))):
                sink = sink_ref[l, 2 * c + half]
                s = lax.dot_general(qh, kk, (((1,), (1,)), ((), ())), preferred_element_type=F32) + bias_ref[...]
                m = jnp.maximum(jnp.max(s, axis=-1, keepdims=True), sink)
                pexp = jnp.exp(s - m)
                den = jnp.sum(pexp, axis=-1, keepdims=True) + jnp.exp(sink - m)
                pr = (pexp * (1.0 / den)).astype(BF16)
                halves.append(jnp.dot(pr, vv, preferred_element_type=F32))
            o_ref[:, c * LANES:(c + 1) * LANES] = jnp.where(lo, halves[0], halves[1]).astype(BF16)


def _attn_prompt(sinks, q, k, v, l, batch, seq):
    m, adim = q.shape
    kvd = k.shape[1]
    nb = seq // WINDOW
    cur = lambda b, n: (b * nb + n, 0)
    prev = lambda b, n: (b * nb + jnp.maximum(n - 1, 0), 0)
    return pl.pallas_call(
        functools.partial(_attn_prompt_kernel, l=l),
        grid=(batch, nb),
        in_specs=[pl.BlockSpec(memory_space=pltpu.SMEM),
                  pl.BlockSpec((WINDOW, adim), cur),
                  pl.BlockSpec((WINDOW, kvd), prev), pl.BlockSpec((WINDOW, kvd), cur),
                  pl.BlockSpec((WINDOW, kvd), prev), pl.BlockSpec((WINDOW, kvd), cur)],
        out_specs=pl.BlockSpec((WINDOW, adim), cur),
        out_shape=jax.ShapeDtypeStruct((m, adim), BF16),
        scratch_shapes=[pltpu.VMEM((WINDOW, 2 * WINDOW), F32)],
        compiler_params=_params("arbitrary", "arbitrary"),
        name="attn_prompt",
    )(sinks, q, k, k, v, v)


def _attn_sample_kernel(sink_ref, q_ref, kn_ref, vn_ref, ck_ref, cv_ref,
                        o_ref, kw_ref, vw_ref, kk_ref, vv_ref, qs_ref, oacc_ref, *, l, t):
    nsq, w, kvd = ck_ref.shape
    tot = kk_ref.shape[1]
    chunks = q_ref.shape[1] // LANES
    pairs = kvd // LANES
    cpp = chunks // pairs
    rows = 2 * cpp * t
    lo = lax.broadcasted_iota(jnp.int32, (t, LANES), 1) < HEAD_DIM
    qi = lax.broadcasted_iota(jnp.int32, (rows, tot), 0) % t
    sj = lax.broadcasted_iota(jnp.int32, (rows, tot), 1)
    mask = (sj <= qi + w) & (sj > qi)
    rowid = lax.broadcasted_iota(jnp.int32, (rows, 1), 0) // t
    pad = jnp.zeros((tot - w - t, kvd), F32)
    for g in range(nsq):
        tok = slice(g * t, (g + 1) * t)
        kk_ref[g, 0:w, :] = ck_ref[g]
        vv_ref[g, 0:w, :] = cv_ref[g]
        kk_ref[g, w:w + t, :] = kn_ref[tok, :]
        vv_ref[g, w:w + t, :] = vn_ref[tok, :]
        kk_ref[g, w + t:, :] = pad
        vv_ref[g, w + t:, :] = pad
        kw_ref[g] = pltpu.roll(kk_ref[g], tot - t, 0)[0:w, :]
        vw_ref[g] = pltpu.roll(vv_ref[g], tot - t, 0)[0:w, :]
        for p in range(pairs):
            ksl = slice(p * LANES, (p + 1) * LANES)
            sink_col = jnp.zeros((rows, 1), F32)
            for j in range(cpp):
                c = p * cpp + j
                qc = q_ref[tok, c * LANES:(c + 1) * LANES]
                qs_ref[g, (2 * j) * t:(2 * j + 1) * t, :] = jnp.where(lo, qc, 0.0)
                qs_ref[g, (2 * j + 1) * t:(2 * j + 2) * t, :] = jnp.where(lo, 0.0, qc)
                sink_col = jnp.where(rowid == 2 * j, sink_ref[l, 2 * c], sink_col)
                sink_col = jnp.where(rowid == 2 * j + 1, sink_ref[l, 2 * c + 1], sink_col)
            s = lax.dot_general(qs_ref[g].astype(BF16), kk_ref[g, :, ksl].astype(BF16),
                                (((1,), (1,)), ((), ())), preferred_element_type=F32) * QK_SCALE
            pr = _sink_softmax(s, mask, sink_col)
            qs_ref[g] = jnp.dot(pr.astype(BF16), vv_ref[g, :, ksl].astype(BF16), preferred_element_type=F32)
            for j in range(cpp):
                c = p * cpp + j
                oacc_ref[tok, c * LANES:(c + 1) * LANES] = jnp.where(
                    lo, qs_ref[g, (2 * j) * t:(2 * j + 1) * t, :], qs_ref[g, (2 * j + 1) * t:(2 * j + 2) * t, :])
    o_ref[...] = oacc_ref[...].astype(BF16)


def _attn_sample(sinks, q, kn, vn, cache_k, cache_v, l, t):
    m, adim = q.shape
    nseq, w, kvd = cache_k.shape[1:]
    nsq = SAMPLE_SEQS_PER_STEP
    tot = -(-(w + t) // 8) * 8
    rows = 2 * (adim // kvd) * t
    row = lambda b: (b, 0)
    seq3 = lambda b: (b, 0, 0)
    cache = lambda b: (l, b, 0, 0)
    return pl.pallas_call(
        functools.partial(_attn_sample_kernel, l=l, t=t),
        grid=(nseq // nsq,),
        in_specs=[pl.BlockSpec(memory_space=pltpu.SMEM),
                  pl.BlockSpec((nsq * t, adim), row),
                  pl.BlockSpec((nsq * t, kvd), row), pl.BlockSpec((nsq * t, kvd), row),
                  pl.BlockSpec((None, nsq, w, kvd), cache), pl.BlockSpec((None, nsq, w, kvd), cache)],
        out_specs=[pl.BlockSpec((nsq * t, adim), row),
                   pl.BlockSpec((nsq, w, kvd), seq3), pl.BlockSpec((nsq, w, kvd), seq3)],
        out_shape=[jax.ShapeDtypeStruct((m, adim), BF16),
                   jax.ShapeDtypeStruct((nseq, w, kvd), F32),
                   jax.ShapeDtypeStruct((nseq, w, kvd), F32)],
        scratch_shapes=[pltpu.VMEM((nsq, tot, kvd), F32), pltpu.VMEM((nsq, tot, kvd), F32),
                        pltpu.VMEM((nsq, rows, LANES), F32), pltpu.VMEM((nsq * t, adim), F32)],
        compiler_params=_params("arbitrary"),
        name="attn_sample",
    )(sinks, q, kn, vn, cache_k, cache_v)


def _layer_norm(z, g, b):
    mu = jnp.mean(z, axis=-1, keepdims=True)
    d = z - mu
    var = jnp.mean(d * d, axis=-1, keepdims=True)
    return d * lax.rsqrt(var + LN_EPS) * g + b


def _top2(sg, idx, n):
    m1 = jnp.max(sg, axis=0, keepdims=True)
    i1 = jnp.min(jnp.where(sg == m1, idx, n), axis=0, keepdims=True)
    rest = jnp.where(idx == i1, -jnp.inf, sg)
    m2 = jnp.max(rest, axis=0, keepdims=True)
    i2 = jnp.min(jnp.where(rest == m2, idx, n), axis=0, keepdims=True)
    return m1, i1, m2, i2


def _pack_bf16_pairs(x):
    half = x.shape[1] // 2
    lo = pltpu.bitcast(x[:, :half].astype(BF16).astype(F32), jnp.uint32)
    hi = pltpu.bitcast(x[:, half:].astype(BF16).astype(F32), jnp.uint32)
    return hi | (lo >> 16)


def _unpack_bf16_pairs(w):
    lo = pltpu.bitcast(w << 16, F32).astype(BF16)
    hi = pltpu.bitcast(w & jnp.uint32(0xFFFF0000), F32).astype(BF16)
    return lo, hi


def _mix_out_kernel(conv_ref, attn_ref, x_ref, wo_ref, g_ref, b_ref, wr_ref, br_ref, cnt_in,
                    x1_ref, x1p_ref, route_ref, cnt_ref, carry, wob, wrb, *, alpha):
    cdim = conv_ref.shape[1]

    @pl.when(pl.program_id(0) == 0)
    def _():
        carry[...] = cnt_in[...]
        wrb[...] = wr_ref[...].astype(BF16)
        wob[0:cdim, :] = wo_ref[0:cdim, :].astype(BF16)
        order = _paired_head_order((wo_ref.shape[0] - cdim) // HEAD_DIM)
        for pos, head in enumerate(order):
            wob[cdim + pos * HEAD_DIM:cdim + (pos + 1) * HEAD_DIM, :] = (
                wo_ref[cdim + head * HEAD_DIM:cdim + (head + 1) * HEAD_DIM, :].astype(BF16))

    y = jnp.dot(conv_ref[...], wob[0:cdim, :], preferred_element_type=F32)
    y = y + jnp.dot(attn_ref[...], wob[cdim:, :], preferred_element_type=F32)
    x1 = _layer_norm(alpha * x_ref[...] + y, g_ref[...], b_ref[...])
    x1_ref[...] = x1
    x1p_ref[...] = _pack_bf16_pairs(x1)
    logits = lax.dot_general(wrb[...], x1.astype(BF16), (((1,), (1,)), ((), ())), preferred_element_type=F32)
    route_ref[...], counts = _route(logits, br_ref[...], carry[...])
    carry[...] = counts
    cnt_ref[...] = counts


def _route(logits, bias, counts):
    ne, tm = logits.shape
    epg = ne // N_EXPERT_GROUPS
    lanes = lambda a: jnp.concatenate([a] * (tm // LANES), axis=1)
    scores = jax.nn.sigmoid(logits)
    sel = scores + lanes(bias)
    eidx = lax.broadcasted_iota(jnp.int32, (ne, tm), 0).astype(F32)
    best = None
    for g in range(N_EXPERT_GROUPS):
        gidx = lax.broadcasted_iota(jnp.int32, (epg, tm), 0).astype(F32) + float(g * epg)
        m1, i1, m2, i2 = _top2(sel[g * epg:(g + 1) * epg], gidx, float(ne))
        cand = (m1 + m2, i1, i2)
        if best is None:
            best = cand
        else:
            upd = cand[0] > best[0]
            best = tuple(jnp.where(upd, new, old) for new, old in zip(cand, best))
    _, e1, e2 = best
    hit1 = eidx == e1
    hit2 = eidx == e2
    w1 = jnp.sum(jnp.where(hit1, scores, 0.0), axis=0, keepdims=True)
    w2 = jnp.sum(jnp.where(hit2, scores, 0.0), axis=0, keepdims=True)
    wsum = w1 + w2

    onehot = jnp.where(hit1 | hit2, 1.0, 0.0)
    earlier = (lax.broadcasted_iota(jnp.int32, (tm, tm), 0) < lax.broadcasted_iota(jnp.int32, (tm, tm), 1))
    before = jnp.dot(onehot.astype(BF16), jnp.where(earlier, 1.0, 0.0).astype(BF16), preferred_element_type=F32)
    before = before + lanes(counts)
    r1 = jnp.sum(jnp.where(hit1, before, 0.0), axis=0, keepdims=True)
    r2 = jnp.sum(jnp.where(hit2, before, 0.0), axis=0, keepdims=True)

    row = lax.broadcasted_iota(jnp.int32, (ROUTE_ROWS, tm), 0)
    out = jnp.zeros((ROUTE_ROWS, tm), F32)
    for c, val in enumerate((e1, e2, w1 / wsum, w2 / wsum, r1, r2)):
        out = jnp.where(row == c, val, out)
    return out, counts + jnp.sum(onehot, axis=1, keepdims=True)


def _mix_out(conv, attn, x, wo, ln_g, ln_b, wr_t, br, cnt_in, l, alpha, tm, total_rows, row_block0, prev=None):
    m, d = x.shape
    cdim = conv.shape[1]
    ne = wr_t.shape[0]
    rowi = lambda i: (i, 0)
    rowo = lambda i: (i + row_block0, 0)
    const = lambda i: (0, 0)
    lay = lambda i: (l, 0, 0)
    in_specs = [pl.BlockSpec((tm, cdim), rowi), pl.BlockSpec((tm, d - cdim), rowi), pl.BlockSpec((tm, d), rowi),
                pl.BlockSpec((None, d, d), lay, pipeline_mode=pl.Buffered(1)),
                pl.BlockSpec((None, 1, d), lay), pl.BlockSpec((None, 1, d), lay),
                pl.BlockSpec((ne, d), const), pl.BlockSpec((ne, LANES), const),
                pl.BlockSpec((ne, LANES), const)]
    args = [conv, attn, x, wo, ln_g, ln_b, wr_t, br, cnt_in]
    aliases = {}
    if prev is not None:
        in_specs += [pl.BlockSpec(memory_space=pl.ANY)] * len(prev)
        aliases = {len(args) + k: k for k in range(len(prev))}
        args += list(prev)

    def body(*refs):
        _mix_out_kernel(*refs[:9], *refs[-7:], alpha=alpha)

    return pl.pallas_call(
        body,
        grid=(m // tm,),
        in_specs=in_specs,
        out_specs=[pl.BlockSpec((tm, d), rowo), pl.BlockSpec((tm, d // 2), rowo),
                   pl.BlockSpec((ROUTE_ROWS, tm), lambda i: (0, i + row_block0)),
                   pl.BlockSpec((ne, LANES), const)],
        out_shape=[jax.ShapeDtypeStruct((total_rows, d), F32),
                   jax.ShapeDtypeStruct((total_rows, d // 2), jnp.uint32),
                   jax.ShapeDtypeStruct((ROUTE_ROWS, total_rows), F32),
                   jax.ShapeDtypeStruct((ne, LANES), F32)],
        scratch_shapes=[pltpu.VMEM((ne, LANES), F32), pltpu.VMEM((d, d), BF16), pltpu.VMEM((ne, d), BF16)],
        input_output_aliases=aliases,
        compiler_params=_params("arbitrary"),
        name="mix_out_sample" if prev is not None else "mix_out_prompt",
    )(*args)


def _scatter_rows_kernel(slot_ref, x_ref, o_ref, sem, *, n_tok):
    ct = x_ref.shape[0]
    c = pl.program_id(0)
    last = pl.num_programs(0) - 1
    tail = n_tok - (pl.cdiv(n_tok, ct) - 1) * ct

    def run(count):
        for j in range(count):
            for k in range(TOP_K):
                pltpu.make_async_copy(x_ref.at[pl.ds(j, 1)],
                                      o_ref.at[pl.ds(slot_ref[TOP_K * j + k], 1)], sem).start()
        for _ in range(TOP_K):
            pltpu.make_async_copy(x_ref.at[pl.ds(0, count)], x_ref.at[pl.ds(0, count)], sem).wait()

    if tail == ct:
        run(ct)
    else:
        pl.when(c < last)(lambda: run(ct))
        pl.when(c == last)(lambda: run(tail))


def _scatter_rows(x1, slot_flat, n_slots):
    total, d = x1.shape
    ct = SLOT_BLOCK // TOP_K
    return pl.pallas_call(
        functools.partial(_scatter_rows_kernel, n_tok=total),
        grid=(pl.cdiv(total, ct),),
        in_specs=[pl.BlockSpec((SLOT_BLOCK,), lambda c: (c,), memory_space=pltpu.SMEM),
                  pl.BlockSpec((ct, d), lambda c: (c, 0))],
        out_specs=pl.BlockSpec(memory_space=pl.ANY),
        out_shape=jax.ShapeDtypeStruct((n_slots, d), x1.dtype),
        scratch_shapes=[pltpu.SemaphoreType.DMA(())],
        compiler_params=_params("arbitrary"),
        name="scatter_rows",
    )(slot_flat, x1)


def _expert_kernel(te_ref, nv_ref, nu_ref, x_ref, wg_hbm, wu_hbm, wd_hbm, o_ref,
                   xb_ref, wgu_ref, h_ref, gu_buf, dn_buf, sems, *, l):
    nf, _, fc = h_ref.shape
    nc, _, oc = dn_buf.shape
    steps = nf + nc
    i = pl.program_id(0)
    s = pl.program_id(1)
    n_used = nu_ref[0]

    def copies(pos, e):
        if pos < nf:
            return [pltpu.make_async_copy(w.at[l, e, :, pos * fc:(pos + 1) * fc], gu_buf.at[pos, k], sems.at[pos])
                    for k, w in enumerate((wg_hbm, wu_hbm))]
        c = pos - nf
        return [pltpu.make_async_copy(wd_hbm.at[l, e, :, c * oc:(c + 1) * oc], dn_buf.at[c], sems.at[pos])]

    def start(pos, tile):
        @pl.when(tile < n_used)
        def _():
            for cp in copies(pos, te_ref[tile]):
                cp.start()

    def step_body(pos):
        if pos == 0:
            @pl.when(i == 0)
            def _():
                for first in range(WEIGHT_LOOKAHEAD):
                    start(first, i)

        ahead = pos + WEIGHT_LOOKAHEAD
        start(ahead % steps, i + ahead // steps)
        for cp in copies(pos, 0):
            cp.wait()

        if pos == 0:
            rid = lax.broadcasted_iota(jnp.int32, (x_ref.shape[0], 1), 0)
            lo, hi = _unpack_bf16_pairs(jnp.where(rid < nv_ref[i], x_ref[...], jnp.uint32(0)))
            xb_ref[:, :x_ref.shape[1]] = lo
            xb_ref[:, x_ref.shape[1]:] = hi
        if pos < nf:
            wgu_ref[:, :fc] = gu_buf[pos, 0].astype(BF16)
            wgu_ref[:, fc:] = gu_buf[pos, 1].astype(BF16)
            gu = jnp.dot(xb_ref[...], wgu_ref[...], preferred_element_type=F32)
            h_ref[pos] = (jax.nn.silu(gu[:, :fc]) * gu[:, fc:]).astype(BF16)
        else:
            c = pos - nf
            y = jnp.dot(h_ref[0], dn_buf[c, 0:fc, :].astype(BF16), preferred_element_type=F32)
            for f in range(1, nf):
                y = y + jnp.dot(h_ref[f], dn_buf[c, f * fc:(f + 1) * fc, :].astype(BF16),
                                preferred_element_type=F32)
            o_ref[...] = y

    @pl.when(i < n_used)
    def _():
        for pos in range(steps):
            pl.when(s == pos)(functools.partial(step_body, pos))


def _expert_ffn(xs, w_gate, w_up, w_down, tile_e, tile_nv, n_used, l):
    rows = xs.shape[0]
    d, dff = w_gate.shape[2:]
    nf = dff // FF_CHUNK
    nc = d // OUT_CHUNK
    steps = nf + nc
    assert steps > WEIGHT_LOOKAHEAD
    nt = rows // EXPERT_TILE

    def item(i, nu):
        return jnp.minimum(i, jnp.maximum(nu[0] - 1, 0))

    def col(i, s, nu):
        return jnp.maximum(jnp.where(i < nu[0], s, steps - 1) - nf, 0)

    xmap = lambda i, s, te, nv, nu: (item(i, nu), 0)
    omap = lambda i, s, te, nv, nu: (item(i, nu), col(i, s, nu))
    hbm = pl.BlockSpec(memory_space=pl.ANY)
    return pl.pallas_call(
        functools.partial(_expert_kernel, l=l),
        grid_spec=pltpu.PrefetchScalarGridSpec(
            num_scalar_prefetch=3,
            grid=(nt, steps),
            in_specs=[pl.BlockSpec((EXPERT_TILE, xs.shape[1]), xmap), hbm, hbm, hbm],
            out_specs=pl.BlockSpec((EXPERT_TILE, OUT_CHUNK), omap),
            scratch_shapes=[pltpu.VMEM((EXPERT_TILE, d), BF16),
                            pltpu.VMEM((d, 2 * FF_CHUNK), BF16),
                            pltpu.VMEM((nf, EXPERT_TILE, FF_CHUNK), BF16),
                            pltpu.VMEM((nf, 2, d, FF_CHUNK), F32),
                            pltpu.VMEM((nc, dff, OUT_CHUNK), F32),
                            pltpu.SemaphoreType.DMA((steps,))]),
        out_shape=jax.ShapeDtypeStruct((rows, d), F32),
        compiler_params=_params("arbitrary", "arbitrary"),
        name="expert_ffn",
    )(tile_e, tile_nv, n_used, xs, w_gate, w_up, w_down)


def _combine_kernel(slot_cur, slot_nxt, x1_ref, r_ref, g_ref, b_ref, y_hbm, o_ref, ybuf, sems, *, alpha):
    i = pl.program_id(0)
    ct = x1_ref.shape[0]

    def fetch(slot_ref, buf):
        for j in range(ct):
            for k in range(TOP_K):
                pltpu.make_async_copy(y_hbm.at[pl.ds(slot_ref[TOP_K * j + k], 1)],
                                      ybuf.at[buf, k, pl.ds(j, 1)], sems.at[buf]).start()

    @pl.when(i == 0)
    def _():
        fetch(slot_cur, 0)

    for parity in range(2):
        @pl.when((i + 1 < pl.num_programs(0)) & (i % 2 == parity))
        def _(parity=parity):
            fetch(slot_nxt, 1 - parity)

    buf = i % 2
    for k in range(TOP_K):
        pltpu.make_async_copy(ybuf.at[buf, k], ybuf.at[buf, k], sems.at[buf]).wait()
    moe = ybuf[buf, 0] * r_ref[0] + ybuf[buf, 1] * r_ref[1]
    o_ref[...] = _layer_norm(alpha * x1_ref[...] + moe, g_ref[...], b_ref[...])


def _combine(x1, yb, slot_flat, gates, ln_g, ln_b, l, alpha, tm, row0, m):
    d = x1.shape[1]
    nsteps = m // tm
    rb0 = row0 // tm
    sb0 = row0 * TOP_K // SLOT_BLOCK
    sper = pl.cdiv(tm * TOP_K, SLOT_BLOCK)
    rowo = lambda i: (i + rb0, 0)
    lay = lambda i: (l, 0, 0)
    return pl.pallas_call(
        functools.partial(_combine_kernel, alpha=alpha),
        grid=(nsteps,),
        in_specs=[pl.BlockSpec((SLOT_BLOCK,), lambda i: (sb0 + i * sper,), memory_space=pltpu.SMEM),
                  pl.BlockSpec((SLOT_BLOCK,), lambda i: (sb0 + jnp.minimum(i + 1, nsteps - 1) * sper,),
                               memory_space=pltpu.SMEM),
                  pl.BlockSpec((tm, d), rowo),
                  pl.BlockSpec((TOP_K, tm, 1), lambda i: (0, i + rb0, 0)),
                  pl.BlockSpec((None, 1, d), lay), pl.BlockSpec((None, 1, d), lay),
                  pl.BlockSpec(memory_space=pl.ANY)],
        out_specs=pl.BlockSpec((tm, d), lambda i: (i, 0)),
        out_shape=jax.ShapeDtypeStruct((m, d), F32),
        scratch_shapes=[pltpu.VMEM((2, TOP_K, tm, d), F32), pltpu.SemaphoreType.DMA((2,))],
        compiler_params=_params("arbitrary"),
        name="combine",
    )(slot_flat, slot_flat, x1, gates, ln_g, ln_b, yb)


def _rope_tables(pos):
    half = ROT_DIM // 2
    inv = ROPE_THETA ** (-jnp.arange(half, dtype=F32) * 2.0 / ROT_DIM)
    ang = pos.astype(F32)[:, None] * inv[None, :]
    cos, sin = jnp.cos(ang), jnp.sin(ang)
    n = pos.shape[0]
    rest = HEAD_DIM - ROT_DIM
    cs = jnp.concatenate([cos, cos, jnp.ones((n, rest), F32)], axis=1)
    sa = jnp.concatenate([-sin, jnp.zeros((n, half + rest), F32)], axis=1)
    sb = jnp.concatenate([jnp.zeros((n, half), F32), sin, jnp.zeros((n, rest), F32)], axis=1)
    rep = LANES // HEAD_DIM
    return tuple(jnp.tile(t, (1, rep)) for t in (cs, sa, sb))


def kernel(x_prompt, x_sample, state_conv, cache_k, cache_v, w_in, conv_w, attn_sinks, w_o,
           ln1_g, ln1_b, w_router, b_router, w_gate, w_up, w_down, ln2_g, ln2_b):
    batch, seq, d = x_prompt.shape
    nseq, dec_seq, _ = x_sample.shape
    depth = w_in.shape[0]
    cdim = conv_w.shape[2]
    adim = d - cdim
    kvd = N_KV_HEADS * HEAD_DIM
    n_heads = adim // HEAD_DIM
    ne = w_router.shape[1]
    mp, ms = batch * seq, nseq * dec_seq
    total = mp + ms
    assign = total * TOP_K
    alpha = (2 * depth) ** 0.25

    sinks = attn_sinks[:, jnp.asarray(_paired_head_order(n_heads))]
    wr_t = w_router.T
    br = jnp.broadcast_to(b_router[:, None], (ne, LANES))

    tabs_p = _rope_tables(jnp.arange(seq))
    tabs_s = _rope_tables(jnp.tile(PAST_LEN + jnp.arange(dec_seq), nseq))
    cache_k2 = cache_k.reshape(depth, nseq, WINDOW, kvd)
    cache_v2 = cache_v.reshape(depth, nseq, WINDOW, kvd)
    ln1g, ln1b = ln1_g.reshape(depth, 1, d), ln1_b.reshape(depth, 1, d)
    ln2g, ln2b = ln2_g.reshape(depth, 1, d), ln2_b.reshape(depth, 1, d)

    n_items = pl.cdiv(assign, EXPERT_TILE) + ne
    slot_len = pl.cdiv(assign, SLOT_BLOCK) * SLOT_BLOCK
    items = jnp.arange(n_items, dtype=jnp.int32)

    xp = x_prompt.reshape(mp, d)
    xs_ = x_sample.reshape(ms, d)
    outs = [[] for _ in range(6)]
    for l in range(depth):
        conv_p, cst_p = _conv_proj(xp, w_in, conv_w, l, PROJ_ROWS, cdim, seq=seq, batch=batch)
        q_p, k_p, v_p, kw_p, vw_p = _qkv_proj(xp, w_in, l, tabs_p, PROJ_ROWS, cdim, adim, kvd, seq=seq, batch=batch)
        attn_p = _attn_prompt(sinks, q_p, k_p, v_p, l, batch, seq)

        st = state_conv[l]
        st1 = jnp.repeat(st[:, 1], dec_seq, axis=0)
        st2 = jnp.stack([st[:, 0], st[:, 1]] + [st[:, 1]] * (dec_seq - 2), axis=1).reshape(ms, cdim)
        conv_s, inner_s = _conv_proj(xs_, w_in, conv_w, l, ms, cdim, state=(st1, st2), dec_seq=dec_seq)
        q_s, kn_s, vn_s = _qkv_proj(xs_, w_in, l, tabs_s, ms, cdim, adim, kvd)
        attn_s, kw_s, vw_s = _attn_sample(sinks, q_s, kn_s, vn_s, cache_k2, cache_v2, l, dec_seq)

        zeros_cnt = jnp.zeros((ne, LANES), F32)
        *bufs, cnt = _mix_out(conv_p, attn_p, xp, w_o, ln1g, ln1b, wr_t, br, zeros_cnt, l, alpha, MIX_ROWS, total, 0)
        x1, x1p, route, cnt = _mix_out(conv_s, attn_s, xs_, w_o, ln1g, ln1b, wr_t, br, cnt,
                                       l, alpha, ms, total, mp // ms, prev=bufs)

        e = route[0:TOP_K].astype(jnp.int32)
        rank = route[4:4 + TOP_K].astype(jnp.int32)
        gates = route[2:2 + TOP_K][:, :, None]
        counts = cnt[:, 0].astype(jnp.int32)
        tiles_e = (counts + EXPERT_TILE - 1) // EXPERT_TILE
        tile_end = jnp.cumsum(tiles_e)
        tile_start = tile_end - tiles_e
        onehot = e[:, :, None] == jnp.arange(ne)[None, None, :]
        pos = jnp.sum(jnp.where(onehot, (tile_start * EXPERT_TILE)[None, None, :], 0), axis=-1) + rank
        slot_flat = jnp.pad(pos.T.reshape(-1), (0, slot_len - assign))
        n_used = tile_end[-1:].astype(jnp.int32)
        item_e = jnp.minimum(jnp.sum(tile_end[None, :] <= items[:, None], axis=1), ne - 1).astype(jnp.int32)
        item_nv = jnp.clip(counts[item_e] - (items - tile_start[item_e]) * EXPERT_TILE, 0, EXPERT_TILE)

        xsorted = _scatter_rows(x1p, slot_flat, n_items * EXPERT_TILE)
        yb = _expert_ffn(xsorted, w_gate, w_up, w_down, item_e, item_nv.astype(jnp.int32), n_used, l)

        xp = _combine(x1, yb, slot_flat, gates, ln2g, ln2b, l, alpha, COMBINE_ROWS, 0, mp)
        xs_ = _combine(x1, yb, slot_flat, gates, ln2g, ln2b, l, alpha, ms, mp, ms)

        inner3 = inner_s.reshape(nseq, dec_seq, cdim)
        for lst, val in zip(outs, (cst_p, kw_p.reshape(batch, WINDOW, N_KV_HEADS, HEAD_DIM),
                                   vw_p.reshape(batch, WINDOW, N_KV_HEADS, HEAD_DIM),
                                   inner3[:, dec_seq - 2:],
                                   kw_s.reshape(nseq, WINDOW, N_KV_HEADS, HEAD_DIM),
                                   vw_s.reshape(nseq, WINDOW, N_KV_HEADS, HEAD_DIM))):
            lst.append(val)

    return (xp.reshape(batch, seq, d), xs_.reshape(nseq, dec_seq, d), *[jnp.stack(o) for o in outs])
```

```python
import functools
import math

import jax
import jax.numpy as jnp
from jax import lax
from jax.experimental import pallas as pl
from jax.experimental.pallas import tpu as pltpu

F32 = jnp.float32
BF16 = jnp.bfloat16

HEAD_DIM = 64
N_KV_HEADS = 4
ROT_DIM = HEAD_DIM // 4
ROPE_THETA = 500000.0
WINDOW = 128
PAST_LEN = 16384
N_EXPERT_GROUPS = 4
TOP_K = 2
LN_EPS = 1e-5
QK_SCALE = HEAD_DIM ** -0.5
assert math.frexp(QK_SCALE)[0] == 0.5

LANES = 128
VMEM_LIMIT = 56 * 1024 * 1024

EXPERT_TILE = 640
FF_CHUNK = 512
OUT_CHUNK = 1024
WEIGHT_LOOKAHEAD = 3
CONV_GROUP = 512
ROUTE_ROWS = 8
SAMPLE_SEQS_PER_STEP = 4
SLOT_BLOCK = 1024
PROJ_ROWS = 512
MIX_ROWS = 256
COMBINE_ROWS = 512


def _params(*sem):
    return pltpu.CompilerParams(dimension_semantics=sem, vmem_limit_bytes=VMEM_LIMIT)


def _rope_chunk(xc, cs, sa, sb):
    return xc * cs + pltpu.roll(xc, LANES - ROT_DIM // 2, 1) * sa + pltpu.roll(xc, ROT_DIM // 2, 1) * sb


def _conv(inner, gate, w_ref, prev1, prev2):
    return gate * (w_ref[0:1, :] * prev2 + w_ref[1:2, :] * prev1 + w_ref[2:3, :] * inner)


def _paired_head_order(n_heads):
    group = n_heads // N_KV_HEADS
    return [(2 * pair + half) * group + j
            for pair in range(N_KV_HEADS // 2) for j in range(group) for half in range(2)]


def _pair_q_chunks(chunks):
    order = _paired_head_order(2 * len(chunks))
    lo = lax.broadcasted_iota(jnp.int32, chunks[0].shape, 1) < HEAD_DIM
    out = []
    for c in range(len(chunks)):
        halves = []
        for half, head in enumerate(order[2 * c:2 * c + 2]):
            src = chunks[head // 2]
            halves.append(src if head % 2 == half else pltpu.roll(src, HEAD_DIM, 1))
        out.append(jnp.where(lo, halves[0], halves[1]))
    return out


def _conv_proj_kernel(x_ref, wc_ref, wb_ref, wu_ref, cw_ref, *refs, tiles_per_seq, dec_seq):
    if dec_seq is None:
        conv_o, cst_o, wbf, carry = refs
    else:
        st1, st2, conv_o, inner_o, wbf = refs
    i = pl.program_id(1)

    @pl.when(i == 0)
    def _():
        for k, w in enumerate((wc_ref, wb_ref, wu_ref)):
            wbf[k] = w[...].astype(BF16)

    xb = x_ref[...].astype(BF16)
    c, gate, u = (jnp.dot(xb, wbf[k], preferred_element_type=F32) for k in range(3))
    inner = c * u
    tm, cw = inner.shape
    row = lax.broadcasted_iota(jnp.int32, (tm, cw), 0)
    if dec_seq is None:
        @pl.when(i % tiles_per_seq == 0)
        def _():
            carry[...] = jnp.zeros_like(carry)

        c1, c2 = carry[7:8, :], carry[6:7, :]
        prev1 = jnp.where(row == 0, c1, pltpu.roll(inner, 1, 0))
        prev2 = jnp.where(row == 0, c2, jnp.where(row == 1, c1, pltpu.roll(inner, 2, 0)))
        carry[...] = inner[tm - 8:, :]
        cst_o[...] = inner[tm - 2:, :]
    else:
        t = row % dec_seq
        prev1 = jnp.where(t == 0, st1[...], pltpu.roll(inner, 1, 0))
        prev2 = jnp.where(t < 2, st2[...], pltpu.roll(inner, 2, 0))
        inner_o[...] = inner
    conv_o[...] = _conv(inner, gate, cw_ref, prev1, prev2).astype(BF16)


def _conv_proj(x, w_in, conv_w, l, tm, cdim, seq=None, batch=None, state=None, dec_seq=None):
    m, k = x.shape
    cw = CONV_GROUP
    groups = cdim // cw
    sec = lambda s: pl.BlockSpec((None, k, cw), lambda g, i: (l, 0, s * groups + g))
    blk = pl.BlockSpec((tm, cw), lambda g, i: (i, g))
    in_specs = [pl.BlockSpec((tm, k), lambda g, i: (i, 0)), sec(0), sec(1), sec(2),
                pl.BlockSpec((None, 3, cw), lambda g, i: (l, 0, g))]
    scratch = [pltpu.VMEM((3, k, cw), BF16)]
    if state is None:
        tps = seq // tm
        args = ()
        out_specs = [blk, pl.BlockSpec((None, 2, cw), lambda g, i: (i // tps, 0, g))]
        out_shape = [jax.ShapeDtypeStruct((m, cdim), BF16), jax.ShapeDtypeStruct((batch, 2, cdim), F32)]
        scratch.append(pltpu.VMEM((8, cw), F32))
    else:
        tps = None
        args = state
        in_specs += [blk, blk]
        out_specs = [blk, blk]
        out_shape = [jax.ShapeDtypeStruct((m, cdim), BF16), jax.ShapeDtypeStruct((m, cdim), F32)]
    return pl.pallas_call(
        functools.partial(_conv_proj_kernel, tiles_per_seq=tps, dec_seq=dec_seq),
        grid=(groups, m // tm),
        in_specs=in_specs, out_specs=out_specs, out_shape=out_shape, scratch_shapes=scratch,
        compiler_params=_params("arbitrary", "arbitrary"),
        name="conv_proj_prompt" if state is None else "conv_proj_sample",
    )(x, w_in, w_in, w_in, conv_w, *args)


def _qkv_proj_kernel(x_ref, w_ref, cs_ref, sa_ref, sb_ref, q_o, k_o, v_o, *refs, adim, q_scale):
    wbf = refs[-1]

    @pl.when(pl.program_id(0) == 0)
    def _():
        wbf[...] = w_ref[...].astype(BF16)

    h = jnp.dot(x_ref[...].astype(BF16), wbf[...], preferred_element_type=F32)
    tm = h.shape[0]
    kvd = (h.shape[1] - adim) // 2
    cs, sa, sb = cs_ref[...], sa_ref[...], sb_ref[...]
    roped = [_rope_chunk(h[:, c * LANES:(c + 1) * LANES], cs, sa, sb) for c in range((adim + kvd) // LANES)]
    for c, qc in enumerate(_pair_q_chunks(roped[:adim // LANES])):
        q_o[:, c * LANES:(c + 1) * LANES] = (qc * q_scale).astype(q_o.dtype)
    for c, kc in enumerate(roped[adim // LANES:]):
        k_o[:, c * LANES:(c + 1) * LANES] = kc.astype(k_o.dtype)
    v = h[:, adim + kvd:]
    v_o[...] = v.astype(v_o.dtype)
    if len(refs) > 1:
        kw_o, vw_o = refs[:2]
        for c, kc in enumerate(roped[adim // LANES:]):
            kw_o[:, c * LANES:(c + 1) * LANES] = kc[tm - WINDOW:, :]
        vw_o[...] = v[tm - WINDOW:, :]


def _qkv_proj(x, w_in, l, tabs, tm, cdim, adim, kvd, seq=None, batch=None):
    m, k = x.shape
    n = adim + 2 * kvd
    row = lambda i: (i, 0)
    prompt = seq is not None
    tps = seq // tm if prompt else 1
    tab = pl.BlockSpec((tm, LANES), lambda i: (i % tps, 0))
    dt = BF16 if prompt else F32
    out_specs = [pl.BlockSpec((tm, adim), row), pl.BlockSpec((tm, kvd), row), pl.BlockSpec((tm, kvd), row)]
    out_shape = [jax.ShapeDtypeStruct((m, adim), dt), jax.ShapeDtypeStruct((m, kvd), dt),
                 jax.ShapeDtypeStruct((m, kvd), dt)]
    if prompt:
        win = pl.BlockSpec((None, WINDOW, kvd), lambda i: (i // tps, 0, 0))
        out_specs += [win, win]
        out_shape += [jax.ShapeDtypeStruct((batch, WINDOW, kvd), F32)] * 2
    return pl.pallas_call(
        functools.partial(_qkv_proj_kernel, adim=adim, q_scale=QK_SCALE if prompt else 1.0),
        grid=(m // tm,),
        in_specs=[pl.BlockSpec((tm, k), row),
                  pl.BlockSpec((None, k, n), lambda i: (l, 0, 3 * cdim // n), pipeline_mode=pl.Buffered(1)),
                  tab, tab, tab],
        out_specs=out_specs, out_shape=out_shape,
        scratch_shapes=[pltpu.VMEM((k, n), BF16)],
        compiler_params=_params("arbitrary"),
        name="qkv_proj_prompt" if prompt else "qkv_proj_sample",
    )(x, w_in, *tabs)


def _sink_softmax(s, mask, sink_col):
    s = jnp.where(mask, s, -jnp.inf)
    m = jnp.maximum(jnp.max(s, axis=-1, keepdims=True), sink_col)
    p = jnp.exp(s - m)
    den = jnp.sum(p, axis=-1, keepdims=True) + jnp.exp(sink_col - m)
    return p * (1.0 / den)


def _attn_prompt_kernel(sink_ref, q_ref, kp_ref, kc_ref, vp_ref, vc_ref, o_ref, bias_ref, *, l):
    n = pl.program_id(1)
    w = q_ref.shape[0]
    chunks = q_ref.shape[1] // LANES
    pairs = kc_ref.shape[1] // LANES
    cpp = chunks // pairs
    lo = lax.broadcasted_iota(jnp.int32, (w, LANES), 1) < HEAD_DIM
    qi = lax.broadcasted_iota(jnp.int32, (w, 2 * w), 0)
    sj = lax.broadcasted_iota(jnp.int32, (w, 2 * w), 1)
    mask = (sj <= qi + w) & (sj > qi) & ((n > 0) | (sj >= w))
    bias_ref[...] = jnp.where(mask, 0.0, -jnp.inf)
    zero = jnp.zeros((w, LANES), BF16)
    for p in range(pairs):
        ksl = slice(p * LANES, (p + 1) * LANES)
        kk = jnp.concatenate([kp_ref[:, ksl], kc_ref[:, ksl]], axis=0)
        vv = jnp.concatenate([vp_ref[:, ksl], vc_ref[:, ksl]], axis=0)
        for j in range(cpp):
            c = p * cpp + j
            qc = q_ref[:, c * LANES:(c + 1) * LANES]
            halves = []
            for half, qh in enumerate((jnp.where(lo, qc, zero), jnp.where(lo, zero, qc))):
                sink = sink_ref[l, 2 * c + half]
                s = lax.dot_general(qh, kk, (((1,), (1,)), ((), ())), preferred_element_type=F32) + bias_ref[...]
                m = jnp.maximum(jnp.max(s, axis=-1, keepdims=True), sink)
                pexp = jnp.exp(s - m)
                den = jnp.sum(pexp, axis=-1, keepdims=True) + jnp.exp(sink - m)
                pr = (pexp * (1.0 / den)).astype(BF16)
                halves.append(jnp.dot(pr, vv, preferred_element_type=F32))
            o_ref[:, c * LANES:(c + 1) * LANES] = jnp.where(lo, halves[0], halves[1]).astype(BF16)


def _attn_prompt(sinks, q, k, v, l, batch, seq):
    m, adim = q.shape
    kvd = k.shape[1]
    nb = seq // WINDOW
    cur = lambda b, n: (b * nb + n, 0)
    prev = lambda b, n: (b * nb + jnp.maximum(n - 1, 0), 0)
    return pl.pallas_call(
        functools.partial(_attn_prompt_kernel, l=l),
        grid=(batch, nb),
        in_specs=[pl.BlockSpec(memory_space=pltpu.SMEM),
                  pl.BlockSpec((WINDOW, adim), cur),
                  pl.BlockSpec((WINDOW, kvd), prev), pl.BlockSpec((WINDOW, kvd), cur),
                  pl.BlockSpec((WINDOW, kvd), prev), pl.BlockSpec((WINDOW, kvd), cur)],
        out_specs=pl.BlockSpec((WINDOW, adim), cur),
        out_shape=jax.ShapeDtypeStruct((m, adim), BF16),
        scratch_shapes=[pltpu.VMEM((WINDOW, 2 * WINDOW), F32)],
        compiler_params=_params("arbitrary", "arbitrary"),
        name="attn_prompt",
    )(sinks, q, k, k, v, v)


def _attn_sample_kernel(sink_ref, q_ref, kn_ref, vn_ref, ck_ref, cv_ref,
                        o_ref, kw_ref, vw_ref, kk_ref, vv_ref, qs_ref, oacc_ref, *, l, t):
    nsq, w, kvd = ck_ref.shape
    tot = kk_ref.shape[1]
    chunks = q_ref.shape[1] // LANES
    pairs = kvd // LANES
    cpp = chunks // pairs
    rows = 2 * cpp * t
    lo = lax.broadcasted_iota(jnp.int32, (t, LANES), 1) < HEAD_DIM
    qi = lax.broadcasted_iota(jnp.int32, (rows, tot), 0) % t
    sj = lax.broadcasted_iota(jnp.int32, (rows, tot), 1)
    mask = (sj <= qi + w) & (sj > qi)
    rowid = lax.broadcasted_iota(jnp.int32, (rows, 1), 0) // t
    pad = jnp.zeros((tot - w - t, kvd), F32)
    for g in range(nsq):
        tok = slice(g * t, (g + 1) * t)
        kk_ref[g, 0:w, :] = ck_ref[g]
        vv_ref[g, 0:w, :] = cv_ref[g]
        kk_ref[g, w:w + t, :] = kn_ref[tok, :]
        vv_ref[g, w:w + t, :] = vn_ref[tok, :]
        kk_ref[g, w + t:, :] = pad
        vv_ref[g, w + t:, :] = pad
        kw_ref[g] = pltpu.roll(kk_ref[g], tot - t, 0)[0:w, :]
        vw_ref[g] = pltpu.roll(vv_ref[g], tot - t, 0)[0:w, :]
        for p in range(pairs):
            ksl = slice(p * LANES, (p + 1) * LANES)
            sink_col = jnp.zeros((rows, 1), F32)
            for j in range(cpp):
                c = p * cpp + j
                qc = q_ref[tok, c * LANES:(c + 1) * LANES]
                qs_ref[g, (2 * j) * t:(2 * j + 1) * t, :] = jnp.where(lo, qc, 0.0)
                qs_ref[g, (2 * j + 1) * t:(2 * j + 2) * t, :] = jnp.where(lo, 0.0, qc)
                sink_col = jnp.where(rowid == 2 * j, sink_ref[l, 2 * c], sink_col)
                sink_col = jnp.where(rowid == 2 * j + 1, sink_ref[l, 2 * c + 1], sink_col)
            s = lax.dot_general(qs_ref[g].astype(BF16), kk_ref[g, :, ksl].astype(BF16),
                                (((1,), (1,)), ((), ())), preferred_element_type=F32) * QK_SCALE
            pr = _sink_softmax(s, mask, sink_col)
            qs_ref[g] = jnp.dot(pr.astype(BF16), vv_ref[g, :, ksl].astype(BF16), preferred_element_type=F32)
            for j in range(cpp):
                c = p * cpp + j
                oacc_ref[tok, c * LANES:(c + 1) * LANES] = jnp.where(
                    lo, qs_ref[g, (2 * j) * t:(2 * j + 1) * t, :], qs_ref[g, (2 * j + 1) * t:(2 * j + 2) * t, :])
    o_ref[...] = oacc_ref[...].astype(BF16)


def _attn_sample(sinks, q, kn, vn, cache_k, cache_v, l, t):
    m, adim = q.shape
    nseq, w, kvd = cache_k.shape[1:]
    nsq = SAMPLE_SEQS_PER_STEP
    tot = -(-(w + t) // 8) * 8
    rows = 2 * (adim // kvd) * t
    row = lambda b: (b, 0)
    seq3 = lambda b: (b, 0, 0)
    cache = lambda b: (l, b, 0, 0)
    return pl.pallas_call(
        functools.partial(_attn_sample_kernel, l=l, t=t),
        grid=(nseq // nsq,),
        in_specs=[pl.BlockSpec(memory_space=pltpu.SMEM),
                  pl.BlockSpec((nsq * t, adim), row),
                  pl.BlockSpec((nsq * t, kvd), row), pl.BlockSpec((nsq * t, kvd), row),
                  pl.BlockSpec((None, nsq, w, kvd), cache), pl.BlockSpec((None, nsq, w, kvd), cache)],
        out_specs=[pl.BlockSpec((nsq * t, adim), row),
                   pl.BlockSpec((nsq, w, kvd), seq3), pl.BlockSpec((nsq, w, kvd), seq3)],
        out_shape=[jax.ShapeDtypeStruct((m, adim), BF16),
                   jax.ShapeDtypeStruct((nseq, w, kvd), F32),
                   jax.ShapeDtypeStruct((nseq, w, kvd), F32)],
        scratch_shapes=[pltpu.VMEM((nsq, tot, kvd), F32), pltpu.VMEM((nsq, tot, kvd), F32),
                        pltpu.VMEM((nsq, rows, LANES), F32), pltpu.VMEM((nsq * t, adim), F32)],
        compiler_params=_params("arbitrary"),
        name="attn_sample",
    )(sinks, q, kn, vn, cache_k, cache_v)


def _layer_norm(z, g, b):
    mu = jnp.mean(z, axis=-1, keepdims=True)
    d = z - mu
    var = jnp.mean(d * d, axis=-1, keepdims=True)
    return d * lax.rsqrt(var + LN_EPS) * g + b


def _top2(sg, idx, n):
    m1 = jnp.max(sg, axis=0, keepdims=True)
    i1 = jnp.min(jnp.where(sg == m1, idx, n), axis=0, keepdims=True)
    rest = jnp.where(idx == i1, -jnp.inf, sg)
    m2 = jnp.max(rest, axis=0, keepdims=True)
    i2 = jnp.min(jnp.where(rest == m2, idx, n), axis=0, keepdims=True)
    return m1, i1, m2, i2


def _pack_bf16_pairs(x):
    half = x.shape[1] // 2
    lo = pltpu.bitcast(x[:, :half].astype(BF16).astype(F32), jnp.uint32)
    hi = pltpu.bitcast(x[:, half:].astype(BF16).astype(F32), jnp.uint32)
    return hi | (lo >> 16)


def _unpack_bf16_pairs(w):
    lo = pltpu.bitcast(w << 16, F32).astype(BF16)
    hi = pltpu.bitcast(w & jnp.uint32(0xFFFF0000), F32).astype(BF16)
    return lo, hi


def _mix_out_kernel(conv_ref, attn_ref, x_ref, wo_ref, g_ref, b_ref, wr_ref, br_ref, cnt_in,
                    x1_ref, x1p_ref, route_ref, cnt_ref, carry, wob, wrb, *, alpha):
    cdim = conv_ref.shape[1]

    @pl.when(pl.program_id(0) == 0)
    def _():
        carry[...] = cnt_in[...]
        wrb[...] = wr_ref[...].astype(BF16)
        wob[0:cdim, :] = wo_ref[0:cdim, :].astype(BF16)
        order = _paired_head_order((wo_ref.shape[0] - cdim) // HEAD_DIM)
        for pos, head in enumerate(order):
            wob[cdim + pos * HEAD_DIM:cdim + (pos + 1) * HEAD_DIM, :] = (
                wo_ref[cdim + head * HEAD_DIM:cdim + (head + 1) * HEAD_DIM, :].astype(BF16))

    y = jnp.dot(conv_ref[...], wob[0:cdim, :], preferred_element_type=F32)
    y = y + jnp.dot(attn_ref[...], wob[cdim:, :], preferred_element_type=F32)
    x1 = _layer_norm(alpha * x_ref[...] + y, g_ref[...], b_ref[...])
    x1_ref[...] = x1
    x1p_ref[...] = _pack_bf16_pairs(x1)
    logits = lax.dot_general(wrb[...], x1.astype(BF16), (((1,), (1,)), ((), ())), preferred_element_type=F32)
    route_ref[...], counts = _route(logits, br_ref[...], carry[...])
    carry[...] = counts
    cnt_ref[...] = counts


def _route(logits, bias, counts):
    ne, tm = logits.shape
    epg = ne // N_EXPERT_GROUPS
    lanes = lambda a: jnp.concatenate([a] * (tm // LANES), axis=1)
    scores = jax.nn.sigmoid(logits)
    sel = scores + lanes(bias)
    eidx = lax.broadcasted_iota(jnp.int32, (ne, tm), 0).astype(F32)
    best = None
    for g in range(N_EXPERT_GROUPS):
        gidx = lax.broadcasted_iota(jnp.int32, (epg, tm), 0).astype(F32) + float(g * epg)
        m1, i1, m2, i2 = _top2(sel[g * epg:(g + 1) * epg], gidx, float(ne))
        cand = (m1 + m2, i1, i2)
        if best is None:
            best = cand
        else:
            upd = cand[0] > best[0]
            best = tuple(jnp.where(upd, new, old) for new, old in zip(cand, best))
    _, e1, e2 = best
    hit1 = eidx == e1
    hit2 = eidx == e2
    w1 = jnp.sum(jnp.where(hit1, scores, 0.0), axis=0, keepdims=True)
    w2 = jnp.sum(jnp.where(hit2, scores, 0.0), axis=0, keepdims=True)
    wsum = w1 + w2

    onehot = jnp.where(hit1 | hit2, 1.0, 0.0)
    earlier = (lax.broadcasted_iota(jnp.int32, (tm, tm), 0) < lax.broadcasted_iota(jnp.int32, (tm, tm), 1))
    before = jnp.dot(onehot.astype(BF16), jnp.where(earlier, 1.0, 0.0).astype(BF16), preferred_element_type=F32)
    before = before + lanes(counts)
    r1 = jnp.sum(jnp.where(hit1, before, 0.0), axis=0, keepdims=True)
    r2 = jnp.sum(jnp.where(hit2, before, 0.0), axis=0, keepdims=True)

    row = lax.broadcasted_iota(jnp.int32, (ROUTE_ROWS, tm), 0)
    out = jnp.zeros((ROUTE_ROWS, tm), F32)
    for c, val in enumerate((e1, e2, w1 / wsum, w2 / wsum, r1, r2)):
        out = jnp.where(row == c, val, out)
    return out, counts + jnp.sum(onehot, axis=1, keepdims=True)


def _mix_out(conv, attn, x, wo, ln_g, ln_b, wr_t, br, cnt_in, l, alpha, tm, total_rows, row_block0, prev=None):
    m, d = x.shape
    cdim = conv.shape[1]
    ne = wr_t.shape[0]
    rowi = lambda i: (i, 0)
    rowo = lambda i: (i + row_block0, 0)
    const = lambda i: (0, 0)
    lay = lambda i: (l, 0, 0)
    in_specs = [pl.BlockSpec((tm, cdim), rowi), pl.BlockSpec((tm, d - cdim), rowi), pl.BlockSpec((tm, d), rowi),
                pl.BlockSpec((None, d, d), lay, pipeline_mode=pl.Buffered(1)),
                pl.BlockSpec((None, 1, d), lay), pl.BlockSpec((None, 1, d), lay),
                pl.BlockSpec((ne, d), const), pl.BlockSpec((ne, LANES), const),
                pl.BlockSpec((ne, LANES), const)]
    args = [conv, attn, x, wo, ln_g, ln_b, wr_t, br, cnt_in]
    aliases = {}
    if prev is not None:
        in_specs += [pl.BlockSpec(memory_space=pl.ANY)] * len(prev)
        aliases = {len(args) + k: k for k in range(len(prev))}
        args += list(prev)

    def body(*refs):
        _mix_out_kernel(*refs[:9], *refs[-7:], alpha=alpha)

    return pl.pallas_call(
        body,
        grid=(m // tm,),
        in_specs=in_specs,
        out_specs=[pl.BlockSpec((tm, d), rowo), pl.BlockSpec((tm, d // 2), rowo),
                   pl.BlockSpec((ROUTE_ROWS, tm), lambda i: (0, i + row_block0)),
                   pl.BlockSpec((ne, LANES), const)],
        out_shape=[jax.ShapeDtypeStruct((total_rows, d), F32),
                   jax.ShapeDtypeStruct((total_rows, d // 2), jnp.uint32),
                   jax.ShapeDtypeStruct((ROUTE_ROWS, total_rows), F32),
                   jax.ShapeDtypeStruct((ne, LANES), F32)],
        scratch_shapes=[pltpu.VMEM((ne, LANES), F32), pltpu.VMEM((d, d), BF16), pltpu.VMEM((ne, d), BF16)],
        input_output_aliases=aliases,
        compiler_params=_params("arbitrary"),
        name="mix_out_sample" if prev is not None else "mix_out_prompt",
    )(*args)


def _scatter_rows_kernel(slot_ref, x_ref, o_ref, sem, *, n_tok):
    ct = x_ref.shape[0]
    c = pl.program_id(0)
    last = pl.num_programs(0) - 1
    tail = n_tok - (pl.cdiv(n_tok, ct) - 1) * ct

    def run(count):
        for j in range(count):
            for k in range(TOP_K):
                pltpu.make_async_copy(x_ref.at[pl.ds(j, 1)],
                                      o_ref.at[pl.ds(slot_ref[TOP_K * j + k], 1)], sem).start(priority=k % 2)
        for _ in range(TOP_K):
            pltpu.make_async_copy(x_ref.at[pl.ds(0, count)], x_ref.at[pl.ds(0, count)], sem).wait()

    if tail == ct:
        run(ct)
    else:
        pl.when(c < last)(lambda: run(ct))
        pl.when(c == last)(lambda: run(tail))


def _scatter_rows(x1, slot_flat, n_slots):
    total, d = x1.shape
    ct = SLOT_BLOCK // TOP_K
    return pl.pallas_call(
        functools.partial(_scatter_rows_kernel, n_tok=total),
        grid=(pl.cdiv(total, ct),),
        in_specs=[pl.BlockSpec((SLOT_BLOCK,), lambda c: (c,), memory_space=pltpu.SMEM),
                  pl.BlockSpec((ct, d), lambda c: (c, 0))],
        out_specs=pl.BlockSpec(memory_space=pl.ANY),
        out_shape=jax.ShapeDtypeStruct((n_slots, d), x1.dtype),
        scratch_shapes=[pltpu.SemaphoreType.DMA(())],
        compiler_params=_params("arbitrary"),
        name="scatter_rows",
    )(slot_flat, x1)


def _expert_kernel(te_ref, nv_ref, nu_ref, x_ref, wg_hbm, wu_hbm, wd_hbm, o_ref,
                   xb_ref, wgu_ref, h_ref, gu_buf, dn_buf, sems, *, l):
    nf, _, fc = h_ref.shape
    nc, _, oc = dn_buf.shape
    steps = nf + nc
    i = pl.program_id(0)
    s = pl.program_id(1)
    n_used = nu_ref[0]

    def copies(pos, e):
        if pos < nf:
            return [pltpu.make_async_copy(w.at[l, e, :, pos * fc:(pos + 1) * fc], gu_buf.at[pos, k], sems.at[pos])
                    for k, w in enumerate((wg_hbm, wu_hbm))]
        c = pos - nf
        return [pltpu.make_async_copy(wd_hbm.at[l, e, :, c * oc:(c + 1) * oc], dn_buf.at[c], sems.at[pos])]

    def start(pos, tile):
        @pl.when(tile < n_used)
        def _():
            for k, cp in enumerate(copies(pos, te_ref[tile])):
                cp.start(priority=(pos + k) % 2)

    def step_body(pos):
        if pos == 0:
            @pl.when(i == 0)
            def _():
                for first in range(WEIGHT_LOOKAHEAD):
                    start(first, i)

        ahead = pos + WEIGHT_LOOKAHEAD
        start(ahead % steps, i + ahead // steps)
        for cp in copies(pos, 0):
            cp.wait()

        if pos == 0:
            rid = lax.broadcasted_iota(jnp.int32, (x_ref.shape[0], 1), 0)
            lo, hi = _unpack_bf16_pairs(jnp.where(rid < nv_ref[i], x_ref[...], jnp.uint32(0)))
            xb_ref[:, :x_ref.shape[1]] = lo
            xb_ref[:, x_ref.shape[1]:] = hi
        if pos < nf:
            wgu_ref[:, :fc] = gu_buf[pos, 0].astype(BF16)
            wgu_ref[:, fc:] = gu_buf[pos, 1].astype(BF16)
            gu = jnp.dot(xb_ref[...], wgu_ref[...], preferred_element_type=F32)
            h_ref[pos] = (jax.nn.silu(gu[:, :fc]) * gu[:, fc:]).astype(BF16)
        else:
            c = pos - nf
            y = jnp.dot(h_ref[0], dn_buf[c, 0:fc, :].astype(BF16), preferred_element_type=F32)
            for f in range(1, nf):
                y = y + jnp.dot(h_ref[f], dn_buf[c, f * fc:(f + 1) * fc, :].astype(BF16),
                                preferred_element_type=F32)
            o_ref[...] = y

    @pl.when(i < n_used)
    def _():
        for pos in range(steps):
            pl.when(s == pos)(functools.partial(step_body, pos))


def _expert_ffn(xs, w_gate, w_up, w_down, tile_e, tile_nv, n_used, l):
    rows = xs.shape[0]
    d, dff = w_gate.shape[2:]
    nf = dff // FF_CHUNK
    nc = d // OUT_CHUNK
    steps = nf + nc
    assert steps > WEIGHT_LOOKAHEAD
    nt = rows // EXPERT_TILE

    def item(i, nu):
        return jnp.minimum(i, jnp.maximum(nu[0] - 1, 0))

    def col(i, s, nu):
        return jnp.maximum(jnp.where(i < nu[0], s, steps - 1) - nf, 0)

    xmap = lambda i, s, te, nv, nu: (item(i, nu), 0)
    omap = lambda i, s, te, nv, nu: (item(i, nu), col(i, s, nu))
    hbm = pl.BlockSpec(memory_space=pl.ANY)
    return pl.pallas_call(
        functools.partial(_expert_kernel, l=l),
        grid_spec=pltpu.PrefetchScalarGridSpec(
            num_scalar_prefetch=3,
            grid=(nt, steps),
            in_specs=[pl.BlockSpec((EXPERT_TILE, xs.shape[1]), xmap), hbm, hbm, hbm],
            out_specs=pl.BlockSpec((EXPERT_TILE, OUT_CHUNK), omap),
            scratch_shapes=[pltpu.VMEM((EXPERT_TILE, d), BF16),
                            pltpu.VMEM((d, 2 * FF_CHUNK), BF16),
                            pltpu.VMEM((nf, EXPERT_TILE, FF_CHUNK), BF16),
                            pltpu.VMEM((nf, 2, d, FF_CHUNK), F32),
                            pltpu.VMEM((nc, dff, OUT_CHUNK), F32),
                            pltpu.SemaphoreType.DMA((steps,))]),
        out_shape=jax.ShapeDtypeStruct((rows, d), F32),
        compiler_params=_params("arbitrary", "arbitrary"),
        name="expert_ffn",
    )(tile_e, tile_nv, n_used, xs, w_gate, w_up, w_down)


def _combine_kernel(slot_cur, slot_nxt, x1_ref, r_ref, g_ref, b_ref, y_hbm, o_ref, ybuf, sems, *, alpha):
    i = pl.program_id(0)
    ct = x1_ref.shape[0]

    def fetch(slot_ref, buf):
        for j in range(ct):
            for k in range(TOP_K):
                pltpu.make_async_copy(y_hbm.at[pl.ds(slot_ref[TOP_K * j + k], 1)],
                                      ybuf.at[buf, k, pl.ds(j, 1)], sems.at[buf]).start()

    @pl.when(i == 0)
    def _():
        fetch(slot_cur, 0)

    for parity in range(2):
        @pl.when((i + 1 < pl.num_programs(0)) & (i % 2 == parity))
        def _(parity=parity):
            fetch(slot_nxt, 1 - parity)

    buf = i % 2
    for k in range(TOP_K):
        pltpu.make_async_copy(ybuf.at[buf, k], ybuf.at[buf, k], sems.at[buf]).wait()
    moe = ybuf[buf, 0] * r_ref[0] + ybuf[buf, 1] * r_ref[1]
    o_ref[...] = _layer_norm(alpha * x1_ref[...] + moe, g_ref[...], b_ref[...])


def _combine(x1, yb, slot_flat, gates, ln_g, ln_b, l, alpha, tm, row0, m):
    d = x1.shape[1]
    nsteps = m // tm
    rb0 = row0 // tm
    sb0 = row0 * TOP_K // SLOT_BLOCK
    sper = pl.cdiv(tm * TOP_K, SLOT_BLOCK)
    rowo = lambda i: (i + rb0, 0)
    lay = lambda i: (l, 0, 0)
    return pl.pallas_call(
        functools.partial(_combine_kernel, alpha=alpha),
        grid=(nsteps,),
        in_specs=[pl.BlockSpec((SLOT_BLOCK,), lambda i: (sb0 + i * sper,), memory_space=pltpu.SMEM),
                  pl.BlockSpec((SLOT_BLOCK,), lambda i: (sb0 + jnp.minimum(i + 1, nsteps - 1) * sper,),
                               memory_space=pltpu.SMEM),
                  pl.BlockSpec((tm, d), rowo),
                  pl.BlockSpec((TOP_K, tm, 1), lambda i: (0, i + rb0, 0)),
                  pl.BlockSpec((None, 1, d), lay), pl.BlockSpec((None, 1, d), lay),
                  pl.BlockSpec(memory_space=pl.ANY)],
        out_specs=pl.BlockSpec((tm, d), lambda i: (i, 0)),
        out_shape=jax.ShapeDtypeStruct((m, d), F32),
        scratch_shapes=[pltpu.VMEM((2, TOP_K, tm, d), F32), pltpu.SemaphoreType.DMA((2,))],
        compiler_params=_params("arbitrary"),
        name="combine",
    )(slot_flat, slot_flat, x1, gates, ln_g, ln_b, yb)


def _rope_tables(pos):
    half = ROT_DIM // 2
    inv = ROPE_THETA ** (-jnp.arange(half, dtype=F32) * 2.0 / ROT_DIM)
    ang = pos.astype(F32)[:, None] * inv[None, :]
    cos, sin = jnp.cos(ang), jnp.sin(ang)
    n = pos.shape[0]
    rest = HEAD_DIM - ROT_DIM
    cs = jnp.concatenate([cos, cos, jnp.ones((n, rest), F32)], axis=1)
    sa = jnp.concatenate([-sin, jnp.zeros((n, half + rest), F32)], axis=1)
    sb = jnp.concatenate([jnp.zeros((n, half), F32), sin, jnp.zeros((n, rest), F32)], axis=1)
    rep = LANES // HEAD_DIM
    return tuple(jnp.tile(t, (1, rep)) for t in (cs, sa, sb))


def kernel(x_prompt, x_sample, state_conv, cache_k, cache_v, w_in, conv_w, attn_sinks, w_o,
           ln1_g, ln1_b, w_router, b_router, w_gate, w_up, w_down, ln2_g, ln2_b):
    batch, seq, d = x_prompt.shape
    nseq, dec_seq, _ = x_sample.shape
    depth = w_in.shape[0]
    cdim = conv_w.shape[2]
    adim = d - cdim
    kvd = N_KV_HEADS * HEAD_DIM
    n_heads = adim // HEAD_DIM
    ne = w_router.shape[1]
    mp, ms = batch * seq, nseq * dec_seq
    total = mp + ms
    assign = total * TOP_K
    alpha = (2 * depth) ** 0.25

    sinks = attn_sinks[:, jnp.asarray(_paired_head_order(n_heads))]
    wr_t = w_router.T
    br = jnp.broadcast_to(b_router[:, None], (ne, LANES))

    tabs_p = _rope_tables(jnp.arange(seq))
    tabs_s = _rope_tables(jnp.tile(PAST_LEN + jnp.arange(dec_seq), nseq))
    cache_k2 = cache_k.reshape(depth, nseq, WINDOW, kvd)
    cache_v2 = cache_v.reshape(depth, nseq, WINDOW, kvd)
    ln1g, ln1b = ln1_g.reshape(depth, 1, d), ln1_b.reshape(depth, 1, d)
    ln2g, ln2b = ln2_g.reshape(depth, 1, d), ln2_b.reshape(depth, 1, d)

    n_items = pl.cdiv(assign, EXPERT_TILE) + ne
    slot_len = pl.cdiv(assign, SLOT_BLOCK) * SLOT_BLOCK
    items = jnp.arange(n_items, dtype=jnp.int32)

    xp = x_prompt.reshape(mp, d)
    xs_ = x_sample.reshape(ms, d)
    outs = [[] for _ in range(6)]
    for l in range(depth):
        conv_p, cst_p = _conv_proj(xp, w_in, conv_w, l, PROJ_ROWS, cdim, seq=seq, batch=batch)
        q_p, k_p, v_p, kw_p, vw_p = _qkv_proj(xp, w_in, l, tabs_p, PROJ_ROWS, cdim, adim, kvd, seq=seq, batch=batch)
        attn_p = _attn_prompt(sinks, q_p, k_p, v_p, l, batch, seq)

        st = state_conv[l]
        st1 = jnp.repeat(st[:, 1], dec_seq, axis=0)
        st2 = jnp.stack([st[:, 0], st[:, 1]] + [st[:, 1]] * (dec_seq - 2), axis=1).reshape(ms, cdim)
        conv_s, inner_s = _conv_proj(xs_, w_in, conv_w, l, ms, cdim, state=(st1, st2), dec_seq=dec_seq)
        q_s, kn_s, vn_s = _qkv_proj(xs_, w_in, l, tabs_s, ms, cdim, adim, kvd)
        attn_s, kw_s, vw_s = _attn_sample(sinks, q_s, kn_s, vn_s, cache_k2, cache_v2, l, dec_seq)

        zeros_cnt = jnp.zeros((ne, LANES), F32)
        *bufs, cnt = _mix_out(conv_p, attn_p, xp, w_o, ln1g, ln1b, wr_t, br, zeros_cnt, l, alpha, MIX_ROWS, total, 0)
        x1, x1p, route, cnt = _mix_out(conv_s, attn_s, xs_, w_o, ln1g, ln1b, wr_t, br, cnt,
                                       l, alpha, ms, total, mp // ms, prev=bufs)

        e = route[0:TOP_K].astype(jnp.int32)
        rank = route[4:4 + TOP_K].astype(jnp.int32)
        gates = route[2:2 + TOP_K][:, :, None]
        counts = cnt[:, 0].astype(jnp.int32)
        tiles_e = (counts + EXPERT_TILE - 1) // EXPERT_TILE
        tile_end = jnp.cumsum(tiles_e)
        tile_start = tile_end - tiles_e
        onehot = e[:, :, None] == jnp.arange(ne)[None, None, :]
        pos = jnp.sum(jnp.where(onehot, (tile_start * EXPERT_TILE)[None, None, :], 0), axis=-1) + rank
        slot_flat = jnp.pad(pos.T.reshape(-1), (0, slot_len - assign))
        n_used = tile_end[-1:].astype(jnp.int32)
        item_e = jnp.minimum(jnp.sum(tile_end[None, :] <= items[:, None], axis=1), ne - 1).astype(jnp.int32)
        item_nv = jnp.clip(counts[item_e] - (items - tile_start[item_e]) * EXPERT_TILE, 0, EXPERT_TILE)

        xsorted = _scatter_rows(x1p, slot_flat, n_items * EXPERT_TILE)
        yb = _expert_ffn(xsorted, w_gate, w_up, w_down, item_e, item_nv.astype(jnp.int32), n_used, l)

        xp = _combine(x1, yb, slot_flat, gates, ln2g, ln2b, l, alpha, COMBINE_ROWS, 0, mp)
        xs_ = _combine(x1, yb, slot_flat, gates, ln2g, ln2b, l, alpha, ms, mp, ms)

        inner3 = inner_s.reshape(nseq, dec_seq, cdim)
        for lst, val in zip(outs, (cst_p, kw_p.reshape(batch, WINDOW, N_KV_HEADS, HEAD_DIM),
                                   vw_p.reshape(batch, WINDOW, N_KV_HEADS, HEAD_DIM),
                                   inner3[:, dec_seq - 2:],
                                   kw_s.reshape(nseq, WINDOW, N_KV_HEADS, HEAD_DIM),
                                   vw_s.reshape(nseq, WINDOW, N_KV_HEADS, HEAD_DIM))):
            lst.append(val)

    return (xp.reshape(batch, seq, d), xs_.reshape(nseq, dec_seq, d), *[jnp.stack(o) for o in outs])
```

```python
import functools
import math

import jax
import jax.numpy as jnp
from jax import lax
from jax.experimental import pallas as pl
from jax.experimental.pallas import tpu as pltpu

F32 = jnp.float32
BF16 = jnp.bfloat16

HEAD_DIM = 64
N_KV_HEADS = 4
ROT_DIM = HEAD_DIM // 4
ROPE_THETA = 500000.0
WINDOW = 128
PAST_LEN = 16384
N_EXPERT_GROUPS = 4
TOP_K = 2
LN_EPS = 1e-5
QK_SCALE = HEAD_DIM ** -0.5
assert math.frexp(QK_SCALE)[0] == 0.5

LANES = 128
VMEM_LIMIT = 56 * 1024 * 1024

EXPERT_TILE = 640
FF_CHUNK = 512
OUT_CHUNK = 1024
WEIGHT_LOOKAHEAD = 3
CONV_GROUP = 512
ROUTE_ROWS = 8
SAMPLE_SEQS_PER_STEP = 4
SLOT_BLOCK = 1024
PROJ_ROWS = 512
MIX_ROWS = 256
COMBINE_ROWS = 512


def _params(*sem):
    return pltpu.CompilerParams(dimension_semantics=sem, vmem_limit_bytes=VMEM_LIMIT)


def _rope_chunk(xc, cs, sa, sb):
    return xc * cs + pltpu.roll(xc, LANES - ROT_DIM // 2, 1) * sa + pltpu.roll(xc, ROT_DIM // 2, 1) * sb


def _conv(inner, gate, w_ref, prev1, prev2):
    return gate * (w_ref[0:1, :] * prev2 + w_ref[1:2, :] * prev1 + w_ref[2:3, :] * inner)


def _paired_head_order(n_heads):
    group = n_heads // N_KV_HEADS
    return [(2 * pair + half) * group + j
            for pair in range(N_KV_HEADS // 2) for j in range(group) for half in range(2)]


def _pair_q_chunks(chunks):
    order = _paired_head_order(2 * len(chunks))
    lo = lax.broadcasted_iota(jnp.int32, chunks[0].shape, 1) < HEAD_DIM
    out = []
    for c in range(len(chunks)):
        halves = []
        for half, head in enumerate(order[2 * c:2 * c + 2]):
            src = chunks[head // 2]
            halves.append(src if head % 2 == half else pltpu.roll(src, HEAD_DIM, 1))
        out.append(jnp.where(lo, halves[0], halves[1]))
    return out


def _conv_proj_kernel(x_ref, wc_ref, wb_ref, wu_ref, cw_ref, *refs, tiles_per_seq, dec_seq):
    if dec_seq is None:
        conv_o, cst_o, wbf, carry = refs
    else:
        st1, st2, conv_o, inner_o, wbf = refs
    i = pl.program_id(1)

    @pl.when(i == 0)
    def _():
        for k, w in enumerate((wc_ref, wb_ref, wu_ref)):
            wbf[k] = w[...].astype(BF16)

    xb = x_ref[...].astype(BF16)
    c, gate, u = (jnp.dot(xb, wbf[k], preferred_element_type=F32) for k in range(3))
    inner = c * u
    tm, cw = inner.shape
    row = lax.broadcasted_iota(jnp.int32, (tm, cw), 0)
    if dec_seq is None:
        @pl.when(i % tiles_per_seq == 0)
        def _():
            carry[...] = jnp.zeros_like(carry)

        c1, c2 = carry[7:8, :], carry[6:7, :]
        prev1 = jnp.where(row == 0, c1, pltpu.roll(inner, 1, 0))
        prev2 = jnp.where(row == 0, c2, jnp.where(row == 1, c1, pltpu.roll(inner, 2, 0)))
        carry[...] = inner[tm - 8:, :]
        cst_o[...] = inner[tm - 2:, :]
    else:
        t = row % dec_seq
        prev1 = jnp.where(t == 0, st1[...], pltpu.roll(inner, 1, 0))
        prev2 = jnp.where(t < 2, st2[...], pltpu.roll(inner, 2, 0))
        inner_o[...] = inner
    conv_o[...] = _conv(inner, gate, cw_ref, prev1, prev2).astype(BF16)


def _conv_proj(x, w_in, conv_w, l, tm, cdim, seq=None, batch=None, state=None, dec_seq=None):
    m, k = x.shape
    cw = CONV_GROUP
    groups = cdim // cw
    sec = lambda s: pl.BlockSpec((None, k, cw), lambda g, i: (l, 0, s * groups + g))
    blk = pl.BlockSpec((tm, cw), lambda g, i: (i, g))
    in_specs = [pl.BlockSpec((tm, k), lambda g, i: (i, 0)), sec(0), sec(1), sec(2),
                pl.BlockSpec((None, 3, cw), lambda g, i: (l, 0, g))]
    scratch = [pltpu.VMEM((3, k, cw), BF16)]
    if state is None:
        tps = seq // tm
        args = ()
        out_specs = [blk, pl.BlockSpec((None, 2, cw), lambda g, i: (i // tps, 0, g))]
        out_shape = [jax.ShapeDtypeStruct((m, cdim), BF16), jax.ShapeDtypeStruct((batch, 2, cdim), F32)]
        scratch.append(pltpu.VMEM((8, cw), F32))
    else:
        tps = None
        args = state
        in_specs += [blk, blk]
        out_specs = [blk, blk]
        out_shape = [jax.ShapeDtypeStruct((m, cdim), BF16), jax.ShapeDtypeStruct((m, cdim), F32)]
    return pl.pallas_call(
        functools.partial(_conv_proj_kernel, tiles_per_seq=tps, dec_seq=dec_seq),
        grid=(groups, m // tm),
        in_specs=in_specs, out_specs=out_specs, out_shape=out_shape, scratch_shapes=scratch,
        compiler_params=_params("arbitrary", "arbitrary"),
        name="conv_proj_prompt" if state is None else "conv_proj_sample",
    )(x, w_in, w_in, w_in, conv_w, *args)


def _qkv_proj_kernel(x_ref, w_ref, cs_ref, sa_ref, sb_ref, q_o, k_o, v_o, *refs, adim, q_scale):
    wbf = refs[-1]

    @pl.when(pl.program_id(0) == 0)
    def _():
        wbf[...] = w_ref[...].astype(BF16)

    h = jnp.dot(x_ref[...].astype(BF16), wbf[...], preferred_element_type=F32)
    tm = h.shape[0]
    kvd = (h.shape[1] - adim) // 2
    cs, sa, sb = cs_ref[...], sa_ref[...], sb_ref[...]
    roped = [_rope_chunk(h[:, c * LANES:(c + 1) * LANES], cs, sa, sb) for c in range((adim + kvd) // LANES)]
    for c, qc in enumerate(_pair_q_chunks(roped[:adim // LANES])):
        q_o[:, c * LANES:(c + 1) * LANES] = (qc * q_scale).astype(q_o.dtype)
    for c, kc in enumerate(roped[adim // LANES:]):
        k_o[:, c * LANES:(c + 1) * LANES] = kc.astype(k_o.dtype)
    v = h[:, adim + kvd:]
    v_o[...] = v.astype(v_o.dtype)
    if len(refs) > 1:
        kw_o, vw_o = refs[:2]
        for c, kc in enumerate(roped[adim // LANES:]):
            kw_o[:, c * LANES:(c + 1) * LANES] = kc[tm - WINDOW:, :]
        vw_o[...] = v[tm - WINDOW:, :]


def _qkv_proj(x, w_in, l, tabs, tm, cdim, adim, kvd, seq=None, batch=None):
    m, k = x.shape
    n = adim + 2 * kvd
    row = lambda i: (i, 0)
    prompt = seq is not None
    tps = seq // tm if prompt else 1
    tab = pl.BlockSpec((tm, LANES), lambda i: (i % tps, 0))
    dt = BF16 if prompt else F32
    out_specs = [pl.BlockSpec((tm, adim), row), pl.BlockSpec((tm, kvd), row), pl.BlockSpec((tm, kvd), row)]
    out_shape = [jax.ShapeDtypeStruct((m, adim), dt), jax.ShapeDtypeStruct((m, kvd), dt),
                 jax.ShapeDtypeStruct((m, kvd), dt)]
    if prompt:
        win = pl.BlockSpec((None, WINDOW, kvd), lambda i: (i // tps, 0, 0))
        out_specs += [win, win]
        out_shape += [jax.ShapeDtypeStruct((batch, WINDOW, kvd), F32)] * 2
    return pl.pallas_call(
        functools.partial(_qkv_proj_kernel, adim=adim, q_scale=QK_SCALE if prompt else 1.0),
        grid=(m // tm,),
        in_specs=[pl.BlockSpec((tm, k), row),
                  pl.BlockSpec((None, k, n), lambda i: (l, 0, 3 * cdim // n), pipeline_mode=pl.Buffered(1)),
                  tab, tab, tab],
        out_specs=out_specs, out_shape=out_shape,
        scratch_shapes=[pltpu.VMEM((k, n), BF16)],
        compiler_params=_params("arbitrary"),
        name="qkv_proj_prompt" if prompt else "qkv_proj_sample",
    )(x, w_in, *tabs)


def _sink_softmax(s, mask, sink_col):
    s = jnp.where(mask, s, -jnp.inf)
    m = jnp.maximum(jnp.max(s, axis=-1, keepdims=True), sink_col)
    p = jnp.exp(s - m)
    den = jnp.sum(p, axis=-1, keepdims=True) + jnp.exp(sink_col - m)
    return p * (1.0 / den)


def _attn_prompt_kernel(sink_ref, q_ref, kp_ref, kc_ref, vp_ref, vc_ref, o_ref, bias_ref, *, l):
    n = pl.program_id(1)
    w = q_ref.shape[0]
    chunks = q_ref.shape[1] // LANES
    pairs = kc_ref.shape[1] // LANES
    cpp = chunks // pairs
    lo = lax.broadcasted_iota(jnp.int32, (w, LANES), 1) < HEAD_DIM
    qi = lax.broadcasted_iota(jnp.int32, (w, 2 * w), 0)
    sj = lax.broadcasted_iota(jnp.int32, (w, 2 * w), 1)
    mask = (sj <= qi + w) & (sj > qi) & ((n > 0) | (sj >= w))
    bias_ref[...] = jnp.where(mask, 0.0, -jnp.inf)
    zero = jnp.zeros((w, LANES), BF16)
    for p in range(pairs):
        ksl = slice(p * LANES, (p + 1) * LANES)
        kk = jnp.concatenate([kp_ref[:, ksl], kc_ref[:, ksl]], axis=0)
        vv = jnp.concatenate([vp_ref[:, ksl], vc_ref[:, ksl]], axis=0)
        for j in range(cpp):
            c = p * cpp + j
            qc = q_ref[:, c * LANES:(c + 1) * LANES]
            halves = []
            for half, qh in enumerate((jnp.where(lo, qc, zero), jnp.where(lo, zero, qc))):
                sink = sink_ref[l, 2 * c + half]
                s = lax.dot_general(qh, kk, (((1,), (1,)), ((), ())), preferred_element_type=F32) + bias_ref[...]
                m = jnp.maximum(jnp.max(s, axis=-1, keepdims=True), sink)
                pexp = jnp.exp(s - m)
                den = jnp.sum(pexp, axis=-1, keepdims=True) + jnp.exp(sink - m)
                pr = (pexp * (1.0 / den)).astype(BF16)
                halves.append(jnp.dot(pr, vv, preferred_element_type=F32))
            o_ref[:, c * LANES:(c + 1) * LANES] = jnp.where(lo, halves[0], halves[1]).astype(BF16)


def _attn_prompt(sinks, q, k, v, l, batch, seq):
    m, adim = q.shape
    kvd = k.shape[1]
    nb = seq // WINDOW
    cur = lambda b, n: (b * nb + n, 0)
    prev = lambda b, n: (b * nb + jnp.maximum(n - 1, 0), 0)
    return pl.pallas_call(
        functools.partial(_attn_prompt_kernel, l=l),
        grid=(batch, nb),
        in_specs=[pl.BlockSpec(memory_space=pltpu.SMEM),
                  pl.BlockSpec((WINDOW, adim), cur),
                  pl.BlockSpec((WINDOW, kvd), prev), pl.BlockSpec((WINDOW, kvd), cur),
                  pl.BlockSpec((WINDOW, kvd), prev), pl.BlockSpec((WINDOW, kvd), cur)],
        out_specs=pl.BlockSpec((WINDOW, adim), cur),
        out_shape=jax.ShapeDtypeStruct((m, adim), BF16),
        scratch_shapes=[pltpu.VMEM((WINDOW, 2 * WINDOW), F32)],
        compiler_params=_params("arbitrary", "arbitrary"),
        name="attn_prompt",
    )(sinks, q, k, k, v, v)


def _attn_sample_kernel(sink_ref, q_ref, kn_ref, vn_ref, ck_ref, cv_ref,
                        o_ref, kw_ref, vw_ref, kk_ref, vv_ref, qs_ref, oacc_ref, *, l, t):
    nsq, w, kvd = ck_ref.shape
    tot = kk_ref.shape[1]
    chunks = q_ref.shape[1] // LANES
    pairs = kvd // LANES
    cpp = chunks // pairs
    rows = 2 * cpp * t
    lo = lax.broadcasted_iota(jnp.int32, (t, LANES), 1) < HEAD_DIM
    qi = lax.broadcasted_iota(jnp.int32, (rows, tot), 0) % t
    sj = lax.broadcasted_iota(jnp.int32, (rows, tot), 1)
    mask = (sj <= qi + w) & (sj > qi)
    rowid = lax.broadcasted_iota(jnp.int32, (rows, 1), 0) // t
    pad = jnp.zeros((tot - w - t, kvd), F32)
    for g in range(nsq):
        tok = slice(g * t, (g + 1) * t)
        kk_ref[g, 0:w, :] = ck_ref[g]
        vv_ref[g, 0:w, :] = cv_ref[g]
        kk_ref[g, w:w + t, :] = kn_ref[tok, :]
        vv_ref[g, w:w + t, :] = vn_ref[tok, :]
        kk_ref[g, w + t:, :] = pad
        vv_ref[g, w + t:, :] = pad
        kw_ref[g] = pltpu.roll(kk_ref[g], tot - t, 0)[0:w, :]
        vw_ref[g] = pltpu.roll(vv_ref[g], tot - t, 0)[0:w, :]
        for p in range(pairs):
            ksl = slice(p * LANES, (p + 1) * LANES)
            sink_col = jnp.zeros((rows, 1), F32)
            for j in range(cpp):
                c = p * cpp + j
                qc = q_ref[tok, c * LANES:(c + 1) * LANES]
                qs_ref[g, (2 * j) * t:(2 * j + 1) * t, :] = jnp.where(lo, qc, 0.0)
                qs_ref[g, (2 * j + 1) * t:(2 * j + 2) * t, :] = jnp.where(lo, 0.0, qc)
                sink_col = jnp.where(rowid == 2 * j, sink_ref[l, 2 * c], sink_col)
                sink_col = jnp.where(rowid == 2 * j + 1, sink_ref[l, 2 * c + 1], sink_col)
            s = lax.dot_general(qs_ref[g].astype(BF16), kk_ref[g, :, ksl].astype(BF16),
                                (((1,), (1,)), ((), ())), preferred_element_type=F32) * QK_SCALE
            pr = _sink_softmax(s, mask, sink_col)
            qs_ref[g] = jnp.dot(pr.astype(BF16), vv_ref[g, :, ksl].astype(BF16), preferred_element_type=F32)
            for j in range(cpp):
                c = p * cpp + j
                oacc_ref[tok, c * LANES:(c + 1) * LANES] = jnp.where(
                    lo, qs_ref[g, (2 * j) * t:(2 * j + 1) * t, :], qs_ref[g, (2 * j + 1) * t:(2 * j + 2) * t, :])
    o_ref[...] = oacc_ref[...].astype(BF16)


def _attn_sample(sinks, q, kn, vn, cache_k, cache_v, l, t):
    m, adim = q.shape
    nseq, w, kvd = cache_k.shape[1:]
    nsq = SAMPLE_SEQS_PER_STEP
    tot = -(-(w + t) // 8) * 8
    rows = 2 * (adim // kvd) * t
    row = lambda b: (b, 0)
    seq3 = lambda b: (b, 0, 0)
    cache = lambda b: (l, b, 0, 0)
    return pl.pallas_call(
        functools.partial(_attn_sample_kernel, l=l, t=t),
        grid=(nseq // nsq,),
        in_specs=[pl.BlockSpec(memory_space=pltpu.SMEM),
                  pl.BlockSpec((nsq * t, adim), row),
                  pl.BlockSpec((nsq * t, kvd), row), pl.BlockSpec((nsq * t, kvd), row),
                  pl.BlockSpec((None, nsq, w, kvd), cache), pl.BlockSpec((None, nsq, w, kvd), cache)],
        out_specs=[pl.BlockSpec((nsq * t, adim), row),
                   pl.BlockSpec((nsq, w, kvd), seq3), pl.BlockSpec((nsq, w, kvd), seq3)],
        out_shape=[jax.ShapeDtypeStruct((m, adim), BF16),
                   jax.ShapeDtypeStruct((nseq, w, kvd), F32),
                   jax.ShapeDtypeStruct((nseq, w, kvd), F32)],
        scratch_shapes=[pltpu.VMEM((nsq, tot, kvd), F32), pltpu.VMEM((nsq, tot, kvd), F32),
                        pltpu.VMEM((nsq, rows, LANES), F32), pltpu.VMEM((nsq * t, adim), F32)],
        compiler_params=_params("arbitrary"),
        name="attn_sample",
    )(sinks, q, kn, vn, cache_k, cache_v)


def _layer_norm(z, g, b):
    mu = jnp.mean(z, axis=-1, keepdims=True)
    d = z - mu
    var = jnp.mean(d * d, axis=-1, keepdims=True)
    return d * lax.rsqrt(var + LN_EPS) * g + b


def _top2(sg, idx, n):
    m1 = jnp.max(sg, axis=0, keepdims=True)
    i1 = jnp.min(jnp.where(sg == m1, idx, n), axis=0, keepdims=True)
    rest = jnp.where(idx == i1, -jnp.inf, sg)
    m2 = jnp.max(rest, axis=0, keepdims=True)
    i2 = jnp.min(jnp.where(rest == m2, idx, n), axis=0, keepdims=True)
    return m1, i1, m2, i2


def _pack_bf16_pairs(x):
    half = x.shape[1] // 2
    lo = pltpu.bitcast(x[:, :half].astype(BF16).astype(F32), jnp.uint32)
    hi = pltpu.bitcast(x[:, half:].astype(BF16).astype(F32), jnp.uint32)
    return hi | (lo >> 16)


def _unpack_bf16_pairs(w):
    lo = pltpu.bitcast(w << 16, F32).astype(BF16)
    hi = pltpu.bitcast(w & jnp.uint32(0xFFFF0000), F32).astype(BF16)
    return lo, hi


def _mix_out_kernel(conv_ref, attn_ref, x_ref, wo_ref, g_ref, b_ref, wr_ref, br_ref, cnt_in,
                    x1_ref, x1p_ref, route_ref, cnt_ref, carry, wob, wrb, y_scr, *, alpha):
    cdim = conv_ref.shape[1]
    i = pl.program_id(0)

    @pl.when(i == 0)
    def _():
        y_scr[1] = jnp.zeros(y_scr.shape[1:], F32)
        carry[...] = cnt_in[...]
        wrb[...] = wr_ref[...].astype(BF16)
        wob[0:cdim, :] = wo_ref[0:cdim, :].astype(BF16)
        order = _paired_head_order((wo_ref.shape[0] - cdim) // HEAD_DIM)
        for pos, head in enumerate(order):
            wob[cdim + pos * HEAD_DIM:cdim + (pos + 1) * HEAD_DIM, :] = (
                wo_ref[cdim + head * HEAD_DIM:cdim + (head + 1) * HEAD_DIM, :].astype(BF16))

    y_prev = y_scr[(i + 1) % 2]
    y = jnp.dot(conv_ref[...], wob[0:cdim, :], preferred_element_type=F32)
    y_scr[i % 2] = y + jnp.dot(attn_ref[...], wob[cdim:, :], preferred_element_type=F32)

    x1 = _layer_norm(alpha * x_ref[...] + y_prev, g_ref[...], b_ref[...])
    x1_ref[...] = x1
    x1p_ref[...] = _pack_bf16_pairs(x1)
    logits = lax.dot_general(wrb[...], x1.astype(BF16), (((1,), (1,)), ((), ())), preferred_element_type=F32)
    route_ref[...], counts = _route(logits, br_ref[...], carry[...])
    counts = jnp.where(i > 0, counts, carry[...])
    carry[...] = counts
    cnt_ref[...] = counts


def _route(logits, bias, counts):
    ne, tm = logits.shape
    epg = ne // N_EXPERT_GROUPS
    lanes = lambda a: jnp.concatenate([a] * (tm // LANES), axis=1)
    scores = jax.nn.sigmoid(logits)
    sel = scores + lanes(bias)
    eidx = lax.broadcasted_iota(jnp.int32, (ne, tm), 0).astype(F32)
    best = None
    for g in range(N_EXPERT_GROUPS):
        gidx = lax.broadcasted_iota(jnp.int32, (epg, tm), 0).astype(F32) + float(g * epg)
        m1, i1, m2, i2 = _top2(sel[g * epg:(g + 1) * epg], gidx, float(ne))
        cand = (m1 + m2, i1, i2)
        if best is None:
            best = cand
        else:
            upd = cand[0] > best[0]
            best = tuple(jnp.where(upd, new, old) for new, old in zip(cand, best))
    _, e1, e2 = best
    hit1 = eidx == e1
    hit2 = eidx == e2
    w1 = jnp.sum(jnp.where(hit1, scores, 0.0), axis=0, keepdims=True)
    w2 = jnp.sum(jnp.where(hit2, scores, 0.0), axis=0, keepdims=True)
    wsum = w1 + w2

    onehot = jnp.where(hit1 | hit2, 1.0, 0.0)
    earlier = (lax.broadcasted_iota(jnp.int32, (tm, tm), 0) < lax.broadcasted_iota(jnp.int32, (tm, tm), 1))
    before = jnp.dot(onehot.astype(BF16), jnp.where(earlier, 1.0, 0.0).astype(BF16), preferred_element_type=F32)
    before = before + lanes(counts)
    r1 = jnp.sum(jnp.where(hit1, before, 0.0), axis=0, keepdims=True)
    r2 = jnp.sum(jnp.where(hit2, before, 0.0), axis=0, keepdims=True)

    row = lax.broadcasted_iota(jnp.int32, (ROUTE_ROWS, tm), 0)
    out = jnp.zeros((ROUTE_ROWS, tm), F32)
    for c, val in enumerate((e1, e2, w1 / wsum, w2 / wsum, r1, r2)):
        out = jnp.where(row == c, val, out)
    return out, counts + jnp.sum(onehot, axis=1, keepdims=True)


def _mix_out(conv, attn, x, wo, ln_g, ln_b, wr_t, br, cnt_in, l, alpha, tm, total_rows, row_block0, prev=None):
    m, d = x.shape
    cdim = conv.shape[1]
    ne = wr_t.shape[0]
    n = m // tm
    rowi = lambda i: (jnp.minimum(i, n - 1), 0)
    rowp = lambda i: (jnp.maximum(i - 1, 0), 0)
    rowo = lambda i: (jnp.maximum(i - 1, 0) + row_block0, 0)
    const = lambda i: (0, 0)
    lay = lambda i: (l, 0, 0)
    in_specs = [pl.BlockSpec((tm, cdim), rowi), pl.BlockSpec((tm, d - cdim), rowi), pl.BlockSpec((tm, d), rowp),
                pl.BlockSpec((None, d, d), lay, pipeline_mode=pl.Buffered(1)),
                pl.BlockSpec((None, 1, d), lay), pl.BlockSpec((None, 1, d), lay),
                pl.BlockSpec((ne, d), const), pl.BlockSpec((ne, LANES), const),
                pl.BlockSpec((ne, LANES), const)]
    args = [conv, attn, x, wo, ln_g, ln_b, wr_t, br, cnt_in]
    aliases = {}
    if prev is not None:
        in_specs += [pl.BlockSpec(memory_space=pl.ANY)] * len(prev)
        aliases = {len(args) + k: k for k in range(len(prev))}
        args += list(prev)

    def body(*refs):
        _mix_out_kernel(*refs[:9], *refs[-8:], alpha=alpha)

    return pl.pallas_call(
        body,
        grid=(n + 1,),
        in_specs=in_specs,
        out_specs=[pl.BlockSpec((tm, d), rowo), pl.BlockSpec((tm, d // 2), rowo),
                   pl.BlockSpec((ROUTE_ROWS, tm), lambda i: (0, jnp.maximum(i - 1, 0) + row_block0)),
                   pl.BlockSpec((ne, LANES), const)],
        out_shape=[jax.ShapeDtypeStruct((total_rows, d), F32),
                   jax.ShapeDtypeStruct((total_rows, d // 2), jnp.uint32),
                   jax.ShapeDtypeStruct((ROUTE_ROWS, total_rows), F32),
                   jax.ShapeDtypeStruct((ne, LANES), F32)],
        scratch_shapes=[pltpu.VMEM((ne, LANES), F32), pltpu.VMEM((d, d), BF16), pltpu.VMEM((ne, d), BF16),
                        pltpu.VMEM((2, tm, d), F32)],
        input_output_aliases=aliases,
        compiler_params=_params("arbitrary"),
        name="mix_out_sample" if prev is not None else "mix_out_prompt",
    )(*args)


def _scatter_rows_kernel(slot_ref, x_ref, o_ref, sem, *, n_tok):
    ct = x_ref.shape[0]
    c = pl.program_id(0)
    last = pl.num_programs(0) - 1
    tail = n_tok - (pl.cdiv(n_tok, ct) - 1) * ct

    def run(count):
        for j in range(count):
            for k in range(TOP_K):
                pltpu.make_async_copy(x_ref.at[pl.ds(j, 1)],
                                      o_ref.at[pl.ds(slot_ref[TOP_K * j + k], 1)], sem).start(priority=k % 2)
        for _ in range(TOP_K):
            pltpu.make_async_copy(x_ref.at[pl.ds(0, count)], x_ref.at[pl.ds(0, count)], sem).wait()

    if tail == ct:
        run(ct)
    else:
        pl.when(c < last)(lambda: run(ct))
        pl.when(c == last)(lambda: run(tail))


def _scatter_rows(x1, slot_flat, n_slots):
    total, d = x1.shape
    ct = SLOT_BLOCK // TOP_K
    return pl.pallas_call(
        functools.partial(_scatter_rows_kernel, n_tok=total),
        grid=(pl.cdiv(total, ct),),
        in_specs=[pl.BlockSpec((SLOT_BLOCK,), lambda c: (c,), memory_space=pltpu.SMEM),
                  pl.BlockSpec((ct, d), lambda c: (c, 0))],
        out_specs=pl.BlockSpec(memory_space=pl.ANY),
        out_shape=jax.ShapeDtypeStruct((n_slots, d), x1.dtype),
        scratch_shapes=[pltpu.SemaphoreType.DMA(())],
        compiler_params=_params("arbitrary"),
        name="scatter_rows",
    )(slot_flat, x1)


def _expert_kernel(te_ref, nv_ref, nu_ref, x_ref, wg_hbm, wu_hbm, wd_hbm, o_ref,
                   xb_ref, wgu_ref, h_ref, gu_buf, dn_buf, sems, *, l):
    nf, _, fc = h_ref.shape
    nc, _, oc = dn_buf.shape
    steps = nf + nc
    i = pl.program_id(0)
    s = pl.program_id(1)
    n_used = nu_ref[0]

    def copies(pos, e):
        if pos < nf:
            return [pltpu.make_async_copy(w.at[l, e, :, pos * fc:(pos + 1) * fc], gu_buf.at[pos, k], sems.at[pos])
                    for k, w in enumerate((wg_hbm, wu_hbm))]
        c = pos - nf
        return [pltpu.make_async_copy(wd_hbm.at[l, e, :, c * oc:(c + 1) * oc], dn_buf.at[c], sems.at[pos])]

    def start(pos, tile):
        @pl.when(tile < n_used)
        def _():
            for k, cp in enumerate(copies(pos, te_ref[tile])):
                cp.start(priority=(pos + k) % 2)

    def step_body(pos):
        if pos == 0:
            @pl.when(i == 0)
            def _():
                for first in range(WEIGHT_LOOKAHEAD):
                    start(first, i)

        ahead = pos + WEIGHT_LOOKAHEAD
        start(ahead % steps, i + ahead // steps)
        for cp in copies(pos, 0):
            cp.wait()

        if pos == 0:
            rid = lax.broadcasted_iota(jnp.int32, (x_ref.shape[0], 1), 0)
            lo, hi = _unpack_bf16_pairs(jnp.where(rid < nv_ref[i], x_ref[...], jnp.uint32(0)))
            xb_ref[:, :x_ref.shape[1]] = lo
            xb_ref[:, x_ref.shape[1]:] = hi
        if pos < nf:
            wgu_ref[:, :fc] = gu_buf[pos, 0].astype(BF16)
            wgu_ref[:, fc:] = gu_buf[pos, 1].astype(BF16)
            gu = jnp.dot(xb_ref[...], wgu_ref[...], preferred_element_type=F32)
            h_ref[pos] = (jax.nn.silu(gu[:, :fc]) * gu[:, fc:]).astype(BF16)
        else:
            c = pos - nf
            y = jnp.dot(h_ref[0], dn_buf[c, 0:fc, :].astype(BF16), preferred_element_type=F32)
            for f in range(1, nf):
                y = y + jnp.dot(h_ref[f], dn_buf[c, f * fc:(f + 1) * fc, :].astype(BF16),
                                preferred_element_type=F32)
            o_ref[...] = y

    @pl.when(i < n_used)
    def _():
        for pos in range(steps):
            pl.when(s == pos)(functools.partial(step_body, pos))


def _expert_ffn(xs, w_gate, w_up, w_down, tile_e, tile_nv, n_used, l):
    rows = xs.shape[0]
    d, dff = w_gate.shape[2:]
    nf = dff // FF_CHUNK
    nc = d // OUT_CHUNK
    steps = nf + nc
    assert steps > WEIGHT_LOOKAHEAD
    nt = rows // EXPERT_TILE

    def item(i, nu):
        return jnp.minimum(i, jnp.maximum(nu[0] - 1, 0))

    def col(i, s, nu):
        return jnp.maximum(jnp.where(i < nu[0], s, steps - 1) - nf, 0)

    xmap = lambda i, s, te, nv, nu: (item(i, nu), 0)
    omap = lambda i, s, te, nv, nu: (item(i, nu), col(i, s, nu))
    hbm = pl.BlockSpec(memory_space=pl.ANY)
    return pl.pallas_call(
        functools.partial(_expert_kernel, l=l),
        grid_spec=pltpu.PrefetchScalarGridSpec(
            num_scalar_prefetch=3,
            grid=(nt, steps),
            in_specs=[pl.BlockSpec((EXPERT_TILE, xs.shape[1]), xmap), hbm, hbm, hbm],
            out_specs=pl.BlockSpec((EXPERT_TILE, OUT_CHUNK), omap),
            scratch_shapes=[pltpu.VMEM((EXPERT_TILE, d), BF16),
                            pltpu.VMEM((d, 2 * FF_CHUNK), BF16),
                            pltpu.VMEM((nf, EXPERT_TILE, FF_CHUNK), BF16),
                            pltpu.VMEM((nf, 2, d, FF_CHUNK), F32),
                            pltpu.VMEM((nc, dff, OUT_CHUNK), F32),
                            pltpu.SemaphoreType.DMA((steps,))]),
        out_shape=jax.ShapeDtypeStruct((rows, d), F32),
        compiler_params=_params("arbitrary", "arbitrary"),
        name="expert_ffn",
    )(tile_e, tile_nv, n_used, xs, w_gate, w_up, w_down)


def _combine_kernel(slot_cur, slot_nxt, x1_ref, r_ref, g_ref, b_ref, y_hbm, o_ref, ybuf, sems, *, alpha):
    i = pl.program_id(0)
    ct = x1_ref.shape[0]

    def fetch(slot_ref, buf):
        for j in range(ct):
            for k in range(TOP_K):
                pltpu.make_async_copy(y_hbm.at[pl.ds(slot_ref[TOP_K * j + k], 1)],
                                      ybuf.at[buf, k, pl.ds(j, 1)], sems.at[buf]).start()

    @pl.when(i == 0)
    def _():
        fetch(slot_cur, 0)

    for parity in range(2):
        @pl.when((i + 1 < pl.num_programs(0)) & (i % 2 == parity))
        def _(parity=parity):
            fetch(slot_nxt, 1 - parity)

    buf = i % 2
    for k in range(TOP_K):
        pltpu.make_async_copy(ybuf.at[buf, k], ybuf.at[buf, k], sems.at[buf]).wait()
    moe = ybuf[buf, 0] * r_ref[0] + ybuf[buf, 1] * r_ref[1]
    o_ref[...] = _layer_norm(alpha * x1_ref[...] + moe, g_ref[...], b_ref[...])


def _combine(x1, yb, slot_flat, gates, ln_g, ln_b, l, alpha, tm, row0, m):
    d = x1.shape[1]
    nsteps = m // tm
    rb0 = row0 // tm
    sb0 = row0 * TOP_K // SLOT_BLOCK
    sper = pl.cdiv(tm * TOP_K, SLOT_BLOCK)
    rowo = lambda i: (i + rb0, 0)
    lay = lambda i: (l, 0, 0)
    return pl.pallas_call(
        functools.partial(_combine_kernel, alpha=alpha),
        grid=(nsteps,),
        in_specs=[pl.BlockSpec((SLOT_BLOCK,), lambda i: (sb0 + i * sper,), memory_space=pltpu.SMEM),
                  pl.BlockSpec((SLOT_BLOCK,), lambda i: (sb0 + jnp.minimum(i + 1, nsteps - 1) * sper,),
                               memory_space=pltpu.SMEM),
                  pl.BlockSpec((tm, d), rowo),
                  pl.BlockSpec((TOP_K, tm, 1), lambda i: (0, i + rb0, 0)),
                  pl.BlockSpec((None, 1, d), lay), pl.BlockSpec((None, 1, d), lay),
                  pl.BlockSpec(memory_space=pl.ANY)],
        out_specs=pl.BlockSpec((tm, d), lambda i: (i, 0)),
        out_shape=jax.ShapeDtypeStruct((m, d), F32),
        scratch_shapes=[pltpu.VMEM((2, TOP_K, tm, d), F32), pltpu.SemaphoreType.DMA((2,))],
        compiler_params=_params("arbitrary"),
        name="combine",
    )(slot_flat, slot_flat, x1, gates, ln_g, ln_b, yb)


def _rope_tables(pos):
    half = ROT_DIM // 2
    inv = ROPE_THETA ** (-jnp.arange(half, dtype=F32) * 2.0 / ROT_DIM)
    ang = pos.astype(F32)[:, None] * inv[None, :]
    cos, sin = jnp.cos(ang), jnp.sin(ang)
    n = pos.shape[0]
    rest = HEAD_DIM - ROT_DIM
    cs = jnp.concatenate([cos, cos, jnp.ones((n, rest), F32)], axis=1)
    sa = jnp.concatenate([-sin, jnp.zeros((n, half + rest), F32)], axis=1)
    sb = jnp.concatenate([jnp.zeros((n, half), F32), sin, jnp.zeros((n, rest), F32)], axis=1)
    rep = LANES // HEAD_DIM
    return tuple(jnp.tile(t, (1, rep)) for t in (cs, sa, sb))


def kernel(x_prompt, x_sample, state_conv, cache_k, cache_v, w_in, conv_w, attn_sinks, w_o,
           ln1_g, ln1_b, w_router, b_router, w_gate, w_up, w_down, ln2_g, ln2_b):
    batch, seq, d = x_prompt.shape
    nseq, dec_seq, _ = x_sample.shape
    depth = w_in.shape[0]
    cdim = conv_w.shape[2]
    adim = d - cdim
    kvd = N_KV_HEADS * HEAD_DIM
    n_heads = adim // HEAD_DIM
    ne = w_router.shape[1]
    mp, ms = batch * seq, nseq * dec_seq
    total = mp + ms
    assign = total * TOP_K
    alpha = (2 * depth) ** 0.25

    sinks = attn_sinks[:, jnp.asarray(_paired_head_order(n_heads))]
    wr_t = w_router.T
    br = jnp.broadcast_to(b_router[:, None], (ne, LANES))

    tabs_p = _rope_tables(jnp.arange(seq))
    tabs_s = _rope_tables(jnp.tile(PAST_LEN + jnp.arange(dec_seq), nseq))
    cache_k2 = cache_k.reshape(depth, nseq, WINDOW, kvd)
    cache_v2 = cache_v.reshape(depth, nseq, WINDOW, kvd)
    ln1g, ln1b = ln1_g.reshape(depth, 1, d), ln1_b.reshape(depth, 1, d)
    ln2g, ln2b = ln2_g.reshape(depth, 1, d), ln2_b.reshape(depth, 1, d)

    n_items = pl.cdiv(assign, EXPERT_TILE) + ne
    slot_len = pl.cdiv(assign, SLOT_BLOCK) * SLOT_BLOCK
    items = jnp.arange(n_items, dtype=jnp.int32)

    xp = x_prompt.reshape(mp, d)
    xs_ = x_sample.reshape(ms, d)
    outs = [[] for _ in range(6)]
    for l in range(depth):
        conv_p, cst_p = _conv_proj(xp, w_in, conv_w, l, PROJ_ROWS, cdim, seq=seq, batch=batch)
        q_p, k_p, v_p, kw_p, vw_p = _qkv_proj(xp, w_in, l, tabs_p, PROJ_ROWS, cdim, adim, kvd, seq=seq, batch=batch)
        attn_p = _attn_prompt(sinks, q_p, k_p, v_p, l, batch, seq)

        st = state_conv[l]
        st1 = jnp.repeat(st[:, 1], dec_seq, axis=0)
        st2 = jnp.stack([st[:, 0], st[:, 1]] + [st[:, 1]] * (dec_seq - 2), axis=1).reshape(ms, cdim)
        conv_s, inner_s = _conv_proj(xs_, w_in, conv_w, l, ms, cdim, state=(st1, st2), dec_seq=dec_seq)
        q_s, kn_s, vn_s = _qkv_proj(xs_, w_in, l, tabs_s, ms, cdim, adim, kvd)
        attn_s, kw_s, vw_s = _attn_sample(sinks, q_s, kn_s, vn_s, cache_k2, cache_v2, l, dec_seq)

        zeros_cnt = jnp.zeros((ne, LANES), F32)
        *bufs, cnt = _mix_out(conv_p, attn_p, xp, w_o, ln1g, ln1b, wr_t, br, zeros_cnt, l, alpha, MIX_ROWS, total, 0)
        x1, x1p, route, cnt = _mix_out(conv_s, attn_s, xs_, w_o, ln1g, ln1b, wr_t, br, cnt,
                                       l, alpha, ms, total, mp // ms, prev=bufs)

        e = route[0:TOP_K].astype(jnp.int32)
        rank = route[4:4 + TOP_K].astype(jnp.int32)
        gates = route[2:2 + TOP_K][:, :, None]
        counts = cnt[:, 0].astype(jnp.int32)
        tiles_e = (counts + EXPERT_TILE - 1) // EXPERT_TILE
        tile_end = jnp.cumsum(tiles_e)
        tile_start = tile_end - tiles_e
        onehot = e[:, :, None] == jnp.arange(ne)[None, None, :]
        pos = jnp.sum(jnp.where(onehot, (tile_start * EXPERT_TILE)[None, None, :], 0), axis=-1) + rank
        slot_flat = jnp.pad(pos.T.reshape(-1), (0, slot_len - assign))
        n_used = tile_end[-1:].astype(jnp.int32)
        item_e = jnp.minimum(jnp.sum(tile_end[None, :] <= items[:, None], axis=1), ne - 1).astype(jnp.int32)
        item_nv = jnp.clip(counts[item_e] - (items - tile_start[item_e]) * EXPERT_TILE, 0, EXPERT_TILE)

        xsorted = _scatter_rows(x1p, slot_flat, n_items * EXPERT_TILE)
        yb = _expert_ffn(xsorted, w_gate, w_up, w_down, item_e, item_nv.astype(jnp.int32), n_used, l)

        xp = _combine(x1, yb, slot_flat, gates, ln2g, ln2b, l, alpha, COMBINE_ROWS, 0, mp)
        xs_ = _combine(x1, yb, slot_flat, gates, ln2g, ln2b, l, alpha, ms, mp, ms)

        inner3 = inner_s.reshape(nseq, dec_seq, cdim)
        for lst, val in zip(outs, (cst_p, kw_p.reshape(batch, WINDOW, N_KV_HEADS, HEAD_DIM),
                                   vw_p.reshape(batch, WINDOW, N_KV_HEADS, HEAD_DIM),
                                   inner3[:, dec_seq - 2:],
                                   kw_s.reshape(nseq, WINDOW, N_KV_HEADS, HEAD_DIM),
                                   vw_s.reshape(nseq, WINDOW, N_KV_HEADS, HEAD_DIM))):
            lst.append(val)

    return (xp.reshape(batch, seq, d), xs_.reshape(nseq, dec_seq, d), *[jnp.stack(o) for o in outs])
```

```python
import functools
import math

import jax
import jax.numpy as jnp
from jax import lax
from jax.experimental import pallas as pl
from jax.experimental.pallas import tpu as pltpu

F32 = jnp.float32
BF16 = jnp.bfloat16

HEAD_DIM = 64
N_KV_HEADS = 4
ROT_DIM = HEAD_DIM // 4
ROPE_THETA = 500000.0
WINDOW = 128
PAST_LEN = 16384
N_EXPERT_GROUPS = 4
TOP_K = 2
LN_EPS = 1e-5
QK_SCALE = HEAD_DIM ** -0.5
assert math.frexp(QK_SCALE)[0] == 0.5

LANES = 128
VMEM_LIMIT = 56 * 1024 * 1024

EXPERT_TILE = 640
FF_CHUNK = 512
OUT_CHUNK = 1024
WEIGHT_LOOKAHEAD = 3
CONV_GROUP = 512
ROUTE_ROWS = 8
SAMPLE_SEQS_PER_STEP = 8
SLOT_BLOCK = 1024
PROJ_ROWS = 512
MIX_ROWS = 256
COMBINE_ROWS = 512


def _params(*sem):
    return pltpu.CompilerParams(dimension_semantics=sem, vmem_limit_bytes=VMEM_LIMIT)


def _rope_chunk(xc, cs, sa, sb):
    return xc * cs + pltpu.roll(xc, LANES - ROT_DIM // 2, 1) * sa + pltpu.roll(xc, ROT_DIM // 2, 1) * sb


def _conv(inner, gate, w_ref, prev1, prev2):
    return gate * (w_ref[0:1, :] * prev2 + w_ref[1:2, :] * prev1 + w_ref[2:3, :] * inner)


def _paired_head_order(n_heads):
    group = n_heads // N_KV_HEADS
    return [(2 * pair + half) * group + j
            for pair in range(N_KV_HEADS // 2) for j in range(group) for half in range(2)]


def _pair_q_chunks(chunks):
    order = _paired_head_order(2 * len(chunks))
    lo = lax.broadcasted_iota(jnp.int32, chunks[0].shape, 1) < HEAD_DIM
    out = []
    for c in range(len(chunks)):
        halves = []
        for half, head in enumerate(order[2 * c:2 * c + 2]):
            src = chunks[head // 2]
            halves.append(src if head % 2 == half else pltpu.roll(src, HEAD_DIM, 1))
        out.append(jnp.where(lo, halves[0], halves[1]))
    return out


def _conv_proj_kernel(x_ref, wc_ref, wb_ref, wu_ref, cw_ref, *refs, tiles_per_seq, dec_seq):
    if dec_seq is None:
        conv_o, cst_o, wbf, carry = refs
    else:
        st1, st2, conv_o, inner_o, wbf = refs
    i = pl.program_id(1)

    @pl.when(i == 0)
    def _():
        for k, w in enumerate((wc_ref, wb_ref, wu_ref)):
            wbf[k] = w[...].astype(BF16)

    xb = x_ref[...].astype(BF16)
    c, gate, u = (jnp.dot(xb, wbf[k], preferred_element_type=F32) for k in range(3))
    inner = c * u
    tm, cw = inner.shape
    row = lax.broadcasted_iota(jnp.int32, (tm, cw), 0)
    if dec_seq is None:
        @pl.when(i % tiles_per_seq == 0)
        def _():
            carry[...] = jnp.zeros_like(carry)

        c1, c2 = carry[7:8, :], carry[6:7, :]
        prev1 = jnp.where(row == 0, c1, pltpu.roll(inner, 1, 0))
        prev2 = jnp.where(row == 0, c2, jnp.where(row == 1, c1, pltpu.roll(inner, 2, 0)))
        carry[...] = inner[tm - 8:, :]
        cst_o[...] = inner[tm - 2:, :]
    else:
        t = row % dec_seq
        prev1 = jnp.where(t == 0, st1[...], pltpu.roll(inner, 1, 0))
        prev2 = jnp.where(t < 2, st2[...], pltpu.roll(inner, 2, 0))
        inner_o[...] = inner
    conv_o[...] = _conv(inner, gate, cw_ref, prev1, prev2).astype(BF16)


def _conv_proj(x, w_in, conv_w, l, tm, cdim, seq=None, batch=None, state=None, dec_seq=None):
    m, k = x.shape
    cw = CONV_GROUP
    groups = cdim // cw
    sec = lambda s: pl.BlockSpec((None, k, cw), lambda g, i: (l, 0, s * groups + g))
    blk = pl.BlockSpec((tm, cw), lambda g, i: (i, g))
    in_specs = [pl.BlockSpec((tm, k), lambda g, i: (i, 0)), sec(0), sec(1), sec(2),
                pl.BlockSpec((None, 3, cw), lambda g, i: (l, 0, g))]
    scratch = [pltpu.VMEM((3, k, cw), BF16)]
    if state is None:
        tps = seq // tm
        args = ()
        out_specs = [blk, pl.BlockSpec((None, 2, cw), lambda g, i: (i // tps, 0, g))]
        out_shape = [jax.ShapeDtypeStruct((m, cdim), BF16), jax.ShapeDtypeStruct((batch, 2, cdim), F32)]
        scratch.append(pltpu.VMEM((8, cw), F32))
    else:
        tps = None
        args = state
        in_specs += [blk, blk]
        out_specs = [blk, blk]
        out_shape = [jax.ShapeDtypeStruct((m, cdim), BF16), jax.ShapeDtypeStruct((m, cdim), F32)]
    return pl.pallas_call(
        functools.partial(_conv_proj_kernel, tiles_per_seq=tps, dec_seq=dec_seq),
        grid=(groups, m // tm),
        in_specs=in_specs, out_specs=out_specs, out_shape=out_shape, scratch_shapes=scratch,
        compiler_params=_params("arbitrary", "arbitrary"),
        name="conv_proj_prompt" if state is None else "conv_proj_sample",
    )(x, w_in, w_in, w_in, conv_w, *args)


def _qkv_proj_kernel(x_ref, w_ref, cs_ref, sa_ref, sb_ref, q_o, k_o, v_o, *refs, adim, q_scale):
    wbf = refs[-1]

    @pl.when(pl.program_id(0) == 0)
    def _():
        wbf[...] = w_ref[...].astype(BF16)

    h = jnp.dot(x_ref[...].astype(BF16), wbf[...], preferred_element_type=F32)
    tm = h.shape[0]
    kvd = (h.shape[1] - adim) // 2
    cs, sa, sb = cs_ref[...], sa_ref[...], sb_ref[...]
    roped = [_rope_chunk(h[:, c * LANES:(c + 1) * LANES], cs, sa, sb) for c in range((adim + kvd) // LANES)]
    for c, qc in enumerate(_pair_q_chunks(roped[:adim // LANES])):
        q_o[:, c * LANES:(c + 1) * LANES] = (qc * q_scale).astype(q_o.dtype)
    for c, kc in enumerate(roped[adim // LANES:]):
        k_o[:, c * LANES:(c + 1) * LANES] = kc.astype(k_o.dtype)
    v = h[:, adim + kvd:]
    v_o[...] = v.astype(v_o.dtype)
    if len(refs) > 1:
        kw_o, vw_o = refs[:2]
        for c, kc in enumerate(roped[adim // LANES:]):
            kw_o[:, c * LANES:(c + 1) * LANES] = kc[tm - WINDOW:, :]
        vw_o[...] = v[tm - WINDOW:, :]


def _qkv_proj(x, w_in, l, tabs, tm, cdim, adim, kvd, seq=None, batch=None):
    m, k = x.shape
    n = adim + 2 * kvd
    row = lambda i: (i, 0)
    prompt = seq is not None
    tps = seq // tm if prompt else 1
    tab = pl.BlockSpec((tm, LANES), lambda i: (i % tps, 0))
    dt = BF16 if prompt else F32
    out_specs = [pl.BlockSpec((tm, adim), row), pl.BlockSpec((tm, kvd), row), pl.BlockSpec((tm, kvd), row)]
    out_shape = [jax.ShapeDtypeStruct((m, adim), dt), jax.ShapeDtypeStruct((m, kvd), dt),
                 jax.ShapeDtypeStruct((m, kvd), dt)]
    if prompt:
        win = pl.BlockSpec((None, WINDOW, kvd), lambda i: (i // tps, 0, 0))
        out_specs += [win, win]
        out_shape += [jax.ShapeDtypeStruct((batch, WINDOW, kvd), F32)] * 2
    return pl.pallas_call(
        functools.partial(_qkv_proj_kernel, adim=adim, q_scale=QK_SCALE if prompt else 1.0),
        grid=(m // tm,),
        in_specs=[pl.BlockSpec((tm, k), row),
                  pl.BlockSpec((None, k, n), lambda i: (l, 0, 3 * cdim // n), pipeline_mode=pl.Buffered(1)),
                  tab, tab, tab],
        out_specs=out_specs, out_shape=out_shape,
        scratch_shapes=[pltpu.VMEM((k, n), BF16)],
        compiler_params=_params("arbitrary"),
        name="qkv_proj_prompt" if prompt else "qkv_proj_sample",
    )(x, w_in, *tabs)


def _sink_softmax(s, mask, sink_col):
    s = jnp.where(mask, s, -jnp.inf)
    m = jnp.maximum(jnp.max(s, axis=-1, keepdims=True), sink_col)
    p = jnp.exp(s - m)
    den = jnp.sum(p, axis=-1, keepdims=True) + jnp.exp(sink_col - m)
    return p * (1.0 / den)


def _attn_prompt_kernel(sink_ref, q_ref, kp_ref, kc_ref, vp_ref, vc_ref, o_ref, bias_ref, *, l):
    n = pl.program_id(1)
    w = q_ref.shape[0]
    chunks = q_ref.shape[1] // LANES
    pairs = kc_ref.shape[1] // LANES
    cpp = chunks // pairs
    lo = lax.broadcasted_iota(jnp.int32, (w, LANES), 1) < HEAD_DIM
    qi = lax.broadcasted_iota(jnp.int32, (w, 2 * w), 0)
    sj = lax.broadcasted_iota(jnp.int32, (w, 2 * w), 1)
    mask = (sj <= qi + w) & (sj > qi) & ((n > 0) | (sj >= w))
    bias_ref[...] = jnp.where(mask, 0.0, -jnp.inf)
    zero = jnp.zeros((w, LANES), BF16)
    for p in range(pairs):
        ksl = slice(p * LANES, (p + 1) * LANES)
        kk = jnp.concatenate([kp_ref[:, ksl], kc_ref[:, ksl]], axis=0)
        vv = jnp.concatenate([vp_ref[:, ksl], vc_ref[:, ksl]], axis=0)
        for j in range(cpp):
            c = p * cpp + j
            qc = q_ref[:, c * LANES:(c + 1) * LANES]
            halves = []
            for half, qh in enumerate((jnp.where(lo, qc, zero), jnp.where(lo, zero, qc))):
                sink = sink_ref[l, 2 * c + half]
                s = lax.dot_general(qh, kk, (((1,), (1,)), ((), ())), preferred_element_type=F32) + bias_ref[...]
                m = jnp.maximum(jnp.max(s, axis=-1, keepdims=True), sink)
                pexp = jnp.exp(s - m)
                den = jnp.sum(pexp, axis=-1, keepdims=True) + jnp.exp(sink - m)
                pr = (pexp * (1.0 / den)).astype(BF16)
                halves.append(jnp.dot(pr, vv, preferred_element_type=F32))
            o_ref[:, c * LANES:(c + 1) * LANES] = jnp.where(lo, halves[0], halves[1]).astype(BF16)


def _attn_prompt(sinks, q, k, v, l, batch, seq):
    m, adim = q.shape
    kvd = k.shape[1]
    nb = seq // WINDOW
    cur = lambda b, n: (b * nb + n, 0)
    prev = lambda b, n: (b * nb + jnp.maximum(n - 1, 0), 0)
    return pl.pallas_call(
        functools.partial(_attn_prompt_kernel, l=l),
        grid=(batch, nb),
        in_specs=[pl.BlockSpec(memory_space=pltpu.SMEM),
                  pl.BlockSpec((WINDOW, adim), cur),
                  pl.BlockSpec((WINDOW, kvd), prev), pl.BlockSpec((WINDOW, kvd), cur),
                  pl.BlockSpec((WINDOW, kvd), prev), pl.BlockSpec((WINDOW, kvd), cur)],
        out_specs=pl.BlockSpec((WINDOW, adim), cur),
        out_shape=jax.ShapeDtypeStruct((m, adim), BF16),
        scratch_shapes=[pltpu.VMEM((WINDOW, 2 * WINDOW), F32)],
        compiler_params=_params("arbitrary", "arbitrary"),
        name="attn_prompt",
    )(sinks, q, k, k, v, v)


def _attn_sample_kernel(sink_ref, q_ref, kn_ref, vn_ref, ck_ref, cv_ref,
                        o_ref, kw_ref, vw_ref, kk_ref, vv_ref, qs_ref, oacc_ref, *, l, t):
    nsq, w, kvd = ck_ref.shape
    tot = kk_ref.shape[1]
    chunks = q_ref.shape[1] // LANES
    pairs = kvd // LANES
    cpp = chunks // pairs
    rows = 2 * cpp * t
    lo = lax.broadcasted_iota(jnp.int32, (t, LANES), 1) < HEAD_DIM
    qi = lax.broadcasted_iota(jnp.int32, (rows, tot), 0) % t
    sj = lax.broadcasted_iota(jnp.int32, (rows, tot), 1)
    mask = (sj <= qi + w) & (sj > qi)
    rowid = lax.broadcasted_iota(jnp.int32, (rows, 1), 0) // t
    pad = jnp.zeros((tot - w - t, kvd), F32)
    for g in range(nsq):
        tok = slice(g * t, (g + 1) * t)
        kk_ref[g, 0:w, :] = ck_ref[g]
        vv_ref[g, 0:w, :] = cv_ref[g]
        kk_ref[g, w:w + t, :] = kn_ref[tok, :]
        vv_ref[g, w:w + t, :] = vn_ref[tok, :]
        kk_ref[g, w + t:, :] = pad
        vv_ref[g, w + t:, :] = pad
        kw_ref[g] = pltpu.roll(kk_ref[g], tot - t, 0)[0:w, :]
        vw_ref[g] = pltpu.roll(vv_ref[g], tot - t, 0)[0:w, :]
        for p in range(pairs):
            ksl = slice(p * LANES, (p + 1) * LANES)
            sink_col = jnp.zeros((rows, 1), F32)
            for j in range(cpp):
                c = p * cpp + j
                qc = q_ref[tok, c * LANES:(c + 1) * LANES]
                qs_ref[g, (2 * j) * t:(2 * j + 1) * t, :] = jnp.where(lo, qc, 0.0)
                qs_ref[g, (2 * j + 1) * t:(2 * j + 2) * t, :] = jnp.where(lo, 0.0, qc)
                sink_col = jnp.where(rowid == 2 * j, sink_ref[l, 2 * c], sink_col)
                sink_col = jnp.where(rowid == 2 * j + 1, sink_ref[l, 2 * c + 1], sink_col)
            s = lax.dot_general(qs_ref[g].astype(BF16), kk_ref[g, :, ksl].astype(BF16),
                                (((1,), (1,)), ((), ())), preferred_element_type=F32) * QK_SCALE
            pr = _sink_softmax(s, mask, sink_col)
            qs_ref[g] = jnp.dot(pr.astype(BF16), vv_ref[g, :, ksl].astype(BF16), preferred_element_type=F32)
            for j in range(cpp):
                c = p * cpp + j
                oacc_ref[tok, c * LANES:(c + 1) * LANES] = jnp.where(
                    lo, qs_ref[g, (2 * j) * t:(2 * j + 1) * t, :], qs_ref[g, (2 * j + 1) * t:(2 * j + 2) * t, :])
    o_ref[...] = oacc_ref[...].astype(BF16)


def _attn_sample(sinks, q, kn, vn, cache_k, cache_v, l, t):
    m, adim = q.shape
    nseq, w, kvd = cache_k.shape[1:]
    nsq = SAMPLE_SEQS_PER_STEP
    tot = -(-(w + t) // 8) * 8
    rows = 2 * (adim // kvd) * t
    row = lambda b: (b, 0)
    seq3 = lambda b: (b, 0, 0)
    cache = lambda b: (l, b, 0, 0)
    return pl.pallas_call(
        functools.partial(_attn_sample_kernel, l=l, t=t),
        grid=(nseq // nsq,),
        in_specs=[pl.BlockSpec(memory_space=pltpu.SMEM),
                  pl.BlockSpec((nsq * t, adim), row),
                  pl.BlockSpec((nsq * t, kvd), row), pl.BlockSpec((nsq * t, kvd), row),
                  pl.BlockSpec((None, nsq, w, kvd), cache), pl.BlockSpec((None, nsq, w, kvd), cache)],
        out_specs=[pl.BlockSpec((nsq * t, adim), row),
                   pl.BlockSpec((nsq, w, kvd), seq3), pl.BlockSpec((nsq, w, kvd), seq3)],
        out_shape=[jax.ShapeDtypeStruct((m, adim), BF16),
                   jax.ShapeDtypeStruct((nseq, w, kvd), F32),
                   jax.ShapeDtypeStruct((nseq, w, kvd), F32)],
        scratch_shapes=[pltpu.VMEM((nsq, tot, kvd), F32), pltpu.VMEM((nsq, tot, kvd), F32),
                        pltpu.VMEM((nsq, rows, LANES), F32), pltpu.VMEM((nsq * t, adim), F32)],
        compiler_params=_params("arbitrary"),
        name="attn_sample",
    )(sinks, q, kn, vn, cache_k, cache_v)


def _layer_norm(z, g, b):
    mu = jnp.mean(z, axis=-1, keepdims=True)
    d = z - mu
    var = jnp.mean(d * d, axis=-1, keepdims=True)
    return d * lax.rsqrt(var + LN_EPS) * g + b


def _top2(sg, idx, n):
    m1 = jnp.max(sg, axis=0, keepdims=True)
    i1 = jnp.min(jnp.where(sg == m1, idx, n), axis=0, keepdims=True)
    rest = jnp.where(idx == i1, -jnp.inf, sg)
    m2 = jnp.max(rest, axis=0, keepdims=True)
    i2 = jnp.min(jnp.where(rest == m2, idx, n), axis=0, keepdims=True)
    return m1, i1, m2, i2


def _pack_bf16_pairs(x):
    half = x.shape[1] // 2
    lo = pltpu.bitcast(x[:, :half].astype(BF16).astype(F32), jnp.uint32)
    hi = pltpu.bitcast(x[:, half:].astype(BF16).astype(F32), jnp.uint32)
    return hi | (lo >> 16)


def _unpack_bf16_pairs(w):
    lo = pltpu.bitcast(w << 16, F32).astype(BF16)
    hi = pltpu.bitcast(w & jnp.uint32(0xFFFF0000), F32).astype(BF16)
    return lo, hi


def _mix_out_kernel(conv_ref, attn_ref, x_ref, wo_ref, g_ref, b_ref, wr_ref, br_ref, cnt_in,
                    x1_ref, x1p_ref, route_ref, cnt_ref, carry, wob, wrb, y_scr, *, alpha):
    cdim = conv_ref.shape[1]
    i = pl.program_id(0)

    @pl.when(i == 0)
    def _():
        y_scr[1] = jnp.zeros(y_scr.shape[1:], F32)
        carry[...] = cnt_in[...]
        wrb[...] = wr_ref[...].astype(BF16)
        wob[0:cdim, :] = wo_ref[0:cdim, :].astype(BF16)
        order = _paired_head_order((wo_ref.shape[0] - cdim) // HEAD_DIM)
        for pos, head in enumerate(order):
            wob[cdim + pos * HEAD_DIM:cdim + (pos + 1) * HEAD_DIM, :] = (
                wo_ref[cdim + head * HEAD_DIM:cdim + (head + 1) * HEAD_DIM, :].astype(BF16))

    y_prev = y_scr[(i + 1) % 2]
    y = jnp.dot(conv_ref[...], wob[0:cdim, :], preferred_element_type=F32)
    y_scr[i % 2] = y + jnp.dot(attn_ref[...], wob[cdim:, :], preferred_element_type=F32)

    x1 = _layer_norm(alpha * x_ref[...] + y_prev, g_ref[...], b_ref[...])
    x1_ref[...] = x1
    x1p_ref[...] = _pack_bf16_pairs(x1)
    logits = lax.dot_general(wrb[...], x1.astype(BF16), (((1,), (1,)), ((), ())), preferred_element_type=F32)
    route_ref[...], counts = _route(logits, br_ref[...], carry[...])
    counts = jnp.where(i > 0, counts, carry[...])
    carry[...] = counts
    cnt_ref[...] = counts


def _route(logits, bias, counts):
    ne, tm = logits.shape
    epg = ne // N_EXPERT_GROUPS
    lanes = lambda a: jnp.concatenate([a] * (tm // LANES), axis=1)
    scores = jax.nn.sigmoid(logits)
    sel = scores + lanes(bias)
    eidx = lax.broadcasted_iota(jnp.int32, (ne, tm), 0).astype(F32)
    best = None
    for g in range(N_EXPERT_GROUPS):
        gidx = lax.broadcasted_iota(jnp.int32, (epg, tm), 0).astype(F32) + float(g * epg)
        m1, i1, m2, i2 = _top2(sel[g * epg:(g + 1) * epg], gidx, float(ne))
        cand = (m1 + m2, i1, i2)
        if best is None:
            best = cand
        else:
            upd = cand[0] > best[0]
            best = tuple(jnp.where(upd, new, old) for new, old in zip(cand, best))
    _, e1, e2 = best
    hit1 = eidx == e1
    hit2 = eidx == e2
    w1 = jnp.sum(jnp.where(hit1, scores, 0.0), axis=0, keepdims=True)
    w2 = jnp.sum(jnp.where(hit2, scores, 0.0), axis=0, keepdims=True)
    wsum = w1 + w2

    onehot = jnp.where(hit1 | hit2, 1.0, 0.0)
    earlier = (lax.broadcasted_iota(jnp.int32, (tm, tm), 0) < lax.broadcasted_iota(jnp.int32, (tm, tm), 1))
    before = jnp.dot(onehot.astype(BF16), jnp.where(earlier, 1.0, 0.0).astype(BF16), preferred_element_type=F32)
    before = before + lanes(counts)
    r1 = jnp.sum(jnp.where(hit1, before, 0.0), axis=0, keepdims=True)
    r2 = jnp.sum(jnp.where(hit2, before, 0.0), axis=0, keepdims=True)

    row = lax.broadcasted_iota(jnp.int32, (ROUTE_ROWS, tm), 0)
    out = jnp.zeros((ROUTE_ROWS, tm), F32)
    for c, val in enumerate((e1, e2, w1 / wsum, w2 / wsum, r1, r2)):
        out = jnp.where(row == c, val, out)
    return out, counts + jnp.sum(onehot, axis=1, keepdims=True)


def _mix_out(conv, attn, x, wo, ln_g, ln_b, wr_t, br, cnt_in, l, alpha, tm, total_rows, row_block0, prev=None):
    m, d = x.shape
    cdim = conv.shape[1]
    ne = wr_t.shape[0]
    n = m // tm
    rowi = lambda i: (jnp.minimum(i, n - 1), 0)
    rowp = lambda i: (jnp.maximum(i - 1, 0), 0)
    rowo = lambda i: (jnp.maximum(i - 1, 0) + row_block0, 0)
    const = lambda i: (0, 0)
    lay = lambda i: (l, 0, 0)
    in_specs = [pl.BlockSpec((tm, cdim), rowi), pl.BlockSpec((tm, d - cdim), rowi), pl.BlockSpec((tm, d), rowp),
                pl.BlockSpec((None, d, d), lay, pipeline_mode=pl.Buffered(1)),
                pl.BlockSpec((None, 1, d), lay), pl.BlockSpec((None, 1, d), lay),
                pl.BlockSpec((ne, d), const), pl.BlockSpec((ne, LANES), const),
                pl.BlockSpec((ne, LANES), const)]
    args = [conv, attn, x, wo, ln_g, ln_b, wr_t, br, cnt_in]
    aliases = {}
    if prev is not None:
        in_specs += [pl.BlockSpec(memory_space=pl.ANY)] * len(prev)
        aliases = {len(args) + k: k for k in range(len(prev))}
        args += list(prev)

    def body(*refs):
        _mix_out_kernel(*refs[:9], *refs[-8:], alpha=alpha)

    return pl.pallas_call(
        body,
        grid=(n + 1,),
        in_specs=in_specs,
        out_specs=[pl.BlockSpec((tm, d), rowo), pl.BlockSpec((tm, d // 2), rowo),
                   pl.BlockSpec((ROUTE_ROWS, tm), lambda i: (0, jnp.maximum(i - 1, 0) + row_block0)),
                   pl.BlockSpec((ne, LANES), const)],
        out_shape=[jax.ShapeDtypeStruct((total_rows, d), F32),
                   jax.ShapeDtypeStruct((total_rows, d // 2), jnp.uint32),
                   jax.ShapeDtypeStruct((ROUTE_ROWS, total_rows), F32),
                   jax.ShapeDtypeStruct((ne, LANES), F32)],
        scratch_shapes=[pltpu.VMEM((ne, LANES), F32), pltpu.VMEM((d, d), BF16), pltpu.VMEM((ne, d), BF16),
                        pltpu.VMEM((2, tm, d), F32)],
        input_output_aliases=aliases,
        compiler_params=_params("arbitrary"),
        name="mix_out_sample" if prev is not None else "mix_out_prompt",
    )(*args)


def _scatter_rows_kernel(slot_ref, x_ref, o_ref, sem, *, n_tok):
    ct = x_ref.shape[0]
    c = pl.program_id(0)
    last = pl.num_programs(0) - 1
    tail = n_tok - (pl.cdiv(n_tok, ct) - 1) * ct

    def run(count):
        for j in range(count):
            for k in range(TOP_K):
                pltpu.make_async_copy(x_ref.at[pl.ds(j, 1)],
                                      o_ref.at[pl.ds(slot_ref[TOP_K * j + k], 1)], sem).start(priority=k % 2)
        for _ in range(TOP_K):
            pltpu.make_async_copy(x_ref.at[pl.ds(0, count)], x_ref.at[pl.ds(0, count)], sem).wait()

    if tail == ct:
        run(ct)
    else:
        pl.when(c < last)(lambda: run(ct))
        pl.when(c == last)(lambda: run(tail))


def _scatter_rows(x1, slot_flat, n_slots):
    total, d = x1.shape
    ct = SLOT_BLOCK // TOP_K
    return pl.pallas_call(
        functools.partial(_scatter_rows_kernel, n_tok=total),
        grid=(pl.cdiv(total, ct),),
        in_specs=[pl.BlockSpec((SLOT_BLOCK,), lambda c: (c,), memory_space=pltpu.SMEM),
                  pl.BlockSpec((ct, d), lambda c: (c, 0))],
        out_specs=pl.BlockSpec(memory_space=pl.ANY),
        out_shape=jax.ShapeDtypeStruct((n_slots, d), x1.dtype),
        scratch_shapes=[pltpu.SemaphoreType.DMA(())],
        compiler_params=_params("arbitrary"),
        name="scatter_rows",
    )(slot_flat, x1)


def _expert_kernel(te_ref, nv_ref, nu_ref, x_ref, wg_hbm, wu_hbm, wd_hbm, o_ref,
                   xb_ref, wgu_ref, h_ref, gu_buf, dn_buf, sems, *, l):
    nf, _, fc = h_ref.shape
    nc, _, oc = dn_buf.shape
    steps = nf + nc
    i = pl.program_id(0)
    s = pl.program_id(1)
    n_used = nu_ref[0]

    def copies(pos, e):
        if pos < nf:
            return [pltpu.make_async_copy(w.at[l, e, :, pos * fc:(pos + 1) * fc], gu_buf.at[pos, k], sems.at[pos])
                    for k, w in enumerate((wg_hbm, wu_hbm))]
        c = pos - nf
        return [pltpu.make_async_copy(wd_hbm.at[l, e, :, c * oc:(c + 1) * oc], dn_buf.at[c], sems.at[pos])]

    def start(pos, tile):
        @pl.when(tile < n_used)
        def _():
            for k, cp in enumerate(copies(pos, te_ref[tile])):
                cp.start(priority=(pos + k) % 2)

    def step_body(pos):
        if pos == 0:
            @pl.when(i == 0)
            def _():
                for first in range(WEIGHT_LOOKAHEAD):
                    start(first, i)

        ahead = pos + WEIGHT_LOOKAHEAD
        start(ahead % steps, i + ahead // steps)
        for cp in copies(pos, 0):
            cp.wait()

        if pos == 0:
            rid = lax.broadcasted_iota(jnp.int32, (x_ref.shape[0], 1), 0)
            lo, hi = _unpack_bf16_pairs(jnp.where(rid < nv_ref[i], x_ref[...], jnp.uint32(0)))
            xb_ref[:, :x_ref.shape[1]] = lo
            xb_ref[:, x_ref.shape[1]:] = hi
        if pos < nf:
            wgu_ref[:, :fc] = gu_buf[pos, 0].astype(BF16)
            wgu_ref[:, fc:] = gu_buf[pos, 1].astype(BF16)
            gu = jnp.dot(xb_ref[...], wgu_ref[...], preferred_element_type=F32)
            h_ref[pos] = (jax.nn.silu(gu[:, :fc]) * gu[:, fc:]).astype(BF16)
        else:
            c = pos - nf
            y = jnp.dot(h_ref[0], dn_buf[c, 0:fc, :].astype(BF16), preferred_element_type=F32)
            for f in range(1, nf):
                y = y + jnp.dot(h_ref[f], dn_buf[c, f * fc:(f + 1) * fc, :].astype(BF16),
                                preferred_element_type=F32)
            o_ref[...] = y

    @pl.when(i < n_used)
    def _():
        for pos in range(steps):
            pl.when(s == pos)(functools.partial(step_body, pos))


def _expert_ffn(xs, w_gate, w_up, w_down, tile_e, tile_nv, n_used, l):
    rows = xs.shape[0]
    d, dff = w_gate.shape[2:]
    nf = dff // FF_CHUNK
    nc = d // OUT_CHUNK
    steps = nf + nc
    assert steps > WEIGHT_LOOKAHEAD
    nt = rows // EXPERT_TILE

    def item(i, nu):
        return jnp.minimum(i, jnp.maximum(nu[0] - 1, 0))

    def col(i, s, nu):
        return jnp.maximum(jnp.where(i < nu[0], s, steps - 1) - nf, 0)

    xmap = lambda i, s, te, nv, nu: (item(i, nu), 0)
    omap = lambda i, s, te, nv, nu: (item(i, nu), col(i, s, nu))
    hbm = pl.BlockSpec(memory_space=pl.ANY)
    return pl.pallas_call(
        functools.partial(_expert_kernel, l=l),
        grid_spec=pltpu.PrefetchScalarGridSpec(
            num_scalar_prefetch=3,
            grid=(nt, steps),
            in_specs=[pl.BlockSpec((EXPERT_TILE, xs.shape[1]), xmap), hbm, hbm, hbm],
            out_specs=pl.BlockSpec((EXPERT_TILE, OUT_CHUNK), omap),
            scratch_shapes=[pltpu.VMEM((EXPERT_TILE, d), BF16),
                            pltpu.VMEM((d, 2 * FF_CHUNK), BF16),
                            pltpu.VMEM((nf, EXPERT_TILE, FF_CHUNK), BF16),
                            pltpu.VMEM((nf, 2, d, FF_CHUNK), F32),
                            pltpu.VMEM((nc, dff, OUT_CHUNK), F32),
                            pltpu.SemaphoreType.DMA((steps,))]),
        out_shape=jax.ShapeDtypeStruct((rows, d), F32),
        compiler_params=_params("arbitrary", "arbitrary"),
        name="expert_ffn",
    )(tile_e, tile_nv, n_used, xs, w_gate, w_up, w_down)


def _combine_kernel(slot_cur, slot_nxt, x1_ref, r_ref, g_ref, b_ref, y_hbm, o_ref, ybuf, sems, *, alpha):
    i = pl.program_id(0)
    ct = x1_ref.shape[0]

    def fetch(slot_ref, buf):
        for j in range(ct):
            for k in range(TOP_K):
                pltpu.make_async_copy(y_hbm.at[pl.ds(slot_ref[TOP_K * j + k], 1)],
                                      ybuf.at[buf, k, pl.ds(j, 1)], sems.at[buf]).start(priority=k % 2)

    @pl.when(i == 0)
    def _():
        fetch(slot_cur, 0)

    for parity in range(2):
        @pl.when((i + 1 < pl.num_programs(0)) & (i % 2 == parity))
        def _(parity=parity):
            fetch(slot_nxt, 1 - parity)

    buf = i % 2
    for k in range(TOP_K):
        pltpu.make_async_copy(ybuf.at[buf, k], ybuf.at[buf, k], sems.at[buf]).wait()
    moe = ybuf[buf, 0] * r_ref[0] + ybuf[buf, 1] * r_ref[1]
    o_ref[...] = _layer_norm(alpha * x1_ref[...] + moe, g_ref[...], b_ref[...])


def _combine(x1, yb, slot_flat, gates, ln_g, ln_b, l, alpha, tm, row0, m):
    d = x1.shape[1]
    nsteps = m // tm
    rb0 = row0 // tm
    sb0 = row0 * TOP_K // SLOT_BLOCK
    sper = pl.cdiv(tm * TOP_K, SLOT_BLOCK)
    rowo = lambda i: (i + rb0, 0)
    lay = lambda i: (l, 0, 0)
    return pl.pallas_call(
        functools.partial(_combine_kernel, alpha=alpha),
        grid=(nsteps,),
        in_specs=[pl.BlockSpec((SLOT_BLOCK,), lambda i: (sb0 + i * sper,), memory_space=pltpu.SMEM),
                  pl.BlockSpec((SLOT_BLOCK,), lambda i: (sb0 + jnp.minimum(i + 1, nsteps - 1) * sper,),
                               memory_space=pltpu.SMEM),
                  pl.BlockSpec((tm, d), rowo),
                  pl.BlockSpec((TOP_K, tm, 1), lambda i: (0, i + rb0, 0)),
                  pl.BlockSpec((None, 1, d), lay), pl.BlockSpec((None, 1, d), lay),
                  pl.BlockSpec(memory_space=pl.ANY)],
        out_specs=pl.BlockSpec((tm, d), lambda i: (i, 0)),
        out_shape=jax.ShapeDtypeStruct((m, d), F32),
        scratch_shapes=[pltpu.VMEM((2, TOP_K, tm, d), F32), pltpu.SemaphoreType.DMA((2,))],
        compiler_params=_params("arbitrary"),
        name="combine",
    )(slot_flat, slot_flat, x1, gates, ln_g, ln_b, yb)


def _rope_tables(pos):
    half = ROT_DIM // 2
    inv = ROPE_THETA ** (-jnp.arange(half, dtype=F32) * 2.0 / ROT_DIM)
    ang = pos.astype(F32)[:, None] * inv[None, :]
    cos, sin = jnp.cos(ang), jnp.sin(ang)
    n = pos.shape[0]
    rest = HEAD_DIM - ROT_DIM
    cs = jnp.concatenate([cos, cos, jnp.ones((n, rest), F32)], axis=1)
    sa = jnp.concatenate([-sin, jnp.zeros((n, half + rest), F32)], axis=1)
    sb = jnp.concatenate([jnp.zeros((n, half), F32), sin, jnp.zeros((n, rest), F32)], axis=1)
    rep = LANES // HEAD_DIM
    return tuple(jnp.tile(t, (1, rep)) for t in (cs, sa, sb))


def kernel(x_prompt, x_sample, state_conv, cache_k, cache_v, w_in, conv_w, attn_sinks, w_o,
           ln1_g, ln1_b, w_router, b_router, w_gate, w_up, w_down, ln2_g, ln2_b):
    batch, seq, d = x_prompt.shape
    nseq, dec_seq, _ = x_sample.shape
    depth = w_in.shape[0]
    cdim = conv_w.shape[2]
    adim = d - cdim
    kvd = N_KV_HEADS * HEAD_DIM
    n_heads = adim // HEAD_DIM
    ne = w_router.shape[1]
    mp, ms = batch * seq, nseq * dec_seq
    total = mp + ms
    assign = total * TOP_K
    alpha = (2 * depth) ** 0.25

    sinks = attn_sinks[:, jnp.asarray(_paired_head_order(n_heads))]
    wr_t = w_router.T
    br = jnp.broadcast_to(b_router[:, None], (ne, LANES))

    tabs_p = _rope_tables(jnp.arange(seq))
    tabs_s = _rope_tables(jnp.tile(PAST_LEN + jnp.arange(dec_seq), nseq))
    cache_k2 = cache_k.reshape(depth, nseq, WINDOW, kvd)
    cache_v2 = cache_v.reshape(depth, nseq, WINDOW, kvd)
    ln1g, ln1b = ln1_g.reshape(depth, 1, d), ln1_b.reshape(depth, 1, d)
    ln2g, ln2b = ln2_g.reshape(depth, 1, d), ln2_b.reshape(depth, 1, d)

    n_items = pl.cdiv(assign, EXPERT_TILE) + ne
    slot_len = pl.cdiv(assign, SLOT_BLOCK) * SLOT_BLOCK
    items = jnp.arange(n_items, dtype=jnp.int32)

    xp = x_prompt.reshape(mp, d)
    xs_ = x_sample.reshape(ms, d)
    outs = [[] for _ in range(6)]
    for l in range(depth):
        conv_p, cst_p = _conv_proj(xp, w_in, conv_w, l, PROJ_ROWS, cdim, seq=seq, batch=batch)
        q_p, k_p, v_p, kw_p, vw_p = _qkv_proj(xp, w_in, l, tabs_p, PROJ_ROWS, cdim, adim, kvd, seq=seq, batch=batch)
        attn_p = _attn_prompt(sinks, q_p, k_p, v_p, l, batch, seq)

        st = state_conv[l]
        st1 = jnp.repeat(st[:, 1], dec_seq, axis=0)
        st2 = jnp.stack([st[:, 0], st[:, 1]] + [st[:, 1]] * (dec_seq - 2), axis=1).reshape(ms, cdim)
        conv_s, inner_s = _conv_proj(xs_, w_in, conv_w, l, ms, cdim, state=(st1, st2), dec_seq=dec_seq)
        q_s, kn_s, vn_s = _qkv_proj(xs_, w_in, l, tabs_s, ms, cdim, adim, kvd)
        attn_s, kw_s, vw_s = _attn_sample(sinks, q_s, kn_s, vn_s, cache_k2, cache_v2, l, dec_seq)

        zeros_cnt = jnp.zeros((ne, LANES), F32)
        *bufs, cnt = _mix_out(conv_p, attn_p, xp, w_o, ln1g, ln1b, wr_t, br, zeros_cnt, l, alpha, MIX_ROWS, total, 0)
        x1, x1p, route, cnt = _mix_out(conv_s, attn_s, xs_, w_o, ln1g, ln1b, wr_t, br, cnt,
                                       l, alpha, ms, total, mp // ms, prev=bufs)

        e = route[0:TOP_K].astype(jnp.int32)
        rank = route[4:4 + TOP_K].astype(jnp.int32)
        gates = route[2:2 + TOP_K][:, :, None]
        counts = cnt[:, 0].astype(jnp.int32)
        tiles_e = (counts + EXPERT_TILE - 1) // EXPERT_TILE
        tile_end = jnp.cumsum(tiles_e)
        tile_start = tile_end - tiles_e
        onehot = e[:, :, None] == jnp.arange(ne)[None, None, :]
        pos = jnp.sum(jnp.where(onehot, (tile_start * EXPERT_TILE)[None, None, :], 0), axis=-1) + rank
        slot_flat = jnp.pad(pos.T.reshape(-1), (0, slot_len - assign))
        n_used = tile_end[-1:].astype(jnp.int32)
        item_e = jnp.minimum(jnp.sum(tile_end[None, :] <= items[:, None], axis=1), ne - 1).astype(jnp.int32)
        item_nv = jnp.clip(counts[item_e] - (items - tile_start[item_e]) * EXPERT_TILE, 0, EXPERT_TILE)

        xsorted = _scatter_rows(x1p, slot_flat, n_items * EXPERT_TILE)
        yb = _expert_ffn(xsorted, w_gate, w_up, w_down, item_e, item_nv.astype(jnp.int32), n_used, l)

        xp = _combine(x1, yb, slot_flat, gates, ln2g, ln2b, l, alpha, COMBINE_ROWS, 0, mp)
        xs_ = _combine(x1, yb, slot_flat, gates, ln2g, ln2b, l, alpha, ms, mp, ms)

        inner3 = inner_s.reshape(nseq, dec_seq, cdim)
        for lst, val in zip(outs, (cst_p, kw_p.reshape(batch, WINDOW, N_KV_HEADS, HEAD_DIM),
                                   vw_p.reshape(batch, WINDOW, N_KV_HEADS, HEAD_DIM),
                                   inner3[:, dec_seq - 2:],
                                   kw_s.reshape(nseq, WINDOW, N_KV_HEADS, HEAD_DIM),
                                   vw_s.reshape(nseq, WINDOW, N_KV_HEADS, HEAD_DIM))):
            lst.append(val)

    return (xp.reshape(batch, seq, d), xs_.reshape(nseq, dec_seq, d), *[jnp.stack(o) for o in outs])
```

```python
import functools
import math

import jax
import jax.numpy as jnp
from jax import lax
from jax.experimental import pallas as pl
from jax.experimental.pallas import tpu as pltpu

F32 = jnp.float32
BF16 = jnp.bfloat16

HEAD_DIM = 64
N_KV_HEADS = 4
ROT_DIM = HEAD_DIM // 4
ROPE_THETA = 500000.0
WINDOW = 128
PAST_LEN = 16384
N_EXPERT_GROUPS = 4
TOP_K = 2
LN_EPS = 1e-5
QK_SCALE = HEAD_DIM ** -0.5
assert math.frexp(QK_SCALE)[0] == 0.5

LANES = 128
VMEM_LIMIT = 56 * 1024 * 1024

EXPERT_TILE = 576
FF_CHUNK = 512
OUT_CHUNK = 1024
WEIGHT_LOOKAHEAD = 3
CONV_GROUP = 512
ROUTE_ROWS = 8
SAMPLE_SEQS_PER_STEP = 4
SLOT_BLOCK = 1024
PROJ_ROWS = 512
MIX_ROWS = 256
COMBINE_ROWS = 512


def _params(*sem):
    return pltpu.CompilerParams(dimension_semantics=sem, vmem_limit_bytes=VMEM_LIMIT)


def _rope_chunk(xc, cs, sa, sb):
    return xc * cs + pltpu.roll(xc, LANES - ROT_DIM // 2, 1) * sa + pltpu.roll(xc, ROT_DIM // 2, 1) * sb


def _conv(inner, gate, w_ref, prev1, prev2):
    return gate * (w_ref[0:1, :] * prev2 + w_ref[1:2, :] * prev1 + w_ref[2:3, :] * inner)


def _paired_head_order(n_heads):
    group = n_heads // N_KV_HEADS
    return [(2 * pair + half) * group + j
            for pair in range(N_KV_HEADS // 2) for j in range(group) for half in range(2)]


def _pair_q_chunks(chunks):
    order = _paired_head_order(2 * len(chunks))
    lo = lax.broadcasted_iota(jnp.int32, chunks[0].shape, 1) < HEAD_DIM
    out = []
    for c in range(len(chunks)):
        halves = []
        for half, head in enumerate(order[2 * c:2 * c + 2]):
            src = chunks[head // 2]
            halves.append(src if head % 2 == half else pltpu.roll(src, HEAD_DIM, 1))
        out.append(jnp.where(lo, halves[0], halves[1]))
    return out


def _conv_proj_kernel(x_ref, wc_ref, wb_ref, wu_ref, cw_ref, *refs, tiles_per_seq, dec_seq):
    if dec_seq is None:
        conv_o, cst_o, wbf, carry = refs
    else:
        st1, st2, conv_o, inner_o, wbf = refs
    i = pl.program_id(1)

    @pl.when(i == 0)
    def _():
        for k, w in enumerate((wc_ref, wb_ref, wu_ref)):
            wbf[k] = w[...].astype(BF16)

    xb = x_ref[...].astype(BF16)
    c, gate, u = (jnp.dot(xb, wbf[k], preferred_element_type=F32) for k in range(3))
    inner = c * u
    tm, cw = inner.shape
    row = lax.broadcasted_iota(jnp.int32, (tm, cw), 0)
    if dec_seq is None:
        @pl.when(i % tiles_per_seq == 0)
        def _():
            carry[...] = jnp.zeros_like(carry)

        c1, c2 = carry[7:8, :], carry[6:7, :]
        prev1 = jnp.where(row == 0, c1, pltpu.roll(inner, 1, 0))
        prev2 = jnp.where(row == 0, c2, jnp.where(row == 1, c1, pltpu.roll(inner, 2, 0)))
        carry[...] = inner[tm - 8:, :]
        cst_o[...] = inner[tm - 2:, :]
    else:
        t = row % dec_seq
        prev1 = jnp.where(t == 0, st1[...], pltpu.roll(inner, 1, 0))
        prev2 = jnp.where(t < 2, st2[...], pltpu.roll(inner, 2, 0))
        inner_o[...] = inner
    conv_o[...] = _conv(inner, gate, cw_ref, prev1, prev2).astype(BF16)


def _conv_proj(x, w_in, conv_w, l, tm, cdim, seq=None, batch=None, state=None, dec_seq=None):
    m, k = x.shape
    cw = CONV_GROUP
    groups = cdim // cw
    sec = lambda s: pl.BlockSpec((None, k, cw), lambda g, i: (l, 0, s * groups + g))
    blk = pl.BlockSpec((tm, cw), lambda g, i: (i, g))
    in_specs = [pl.BlockSpec((tm, k), lambda g, i: (i, 0)), sec(0), sec(1), sec(2),
                pl.BlockSpec((None, 3, cw), lambda g, i: (l, 0, g))]
    scratch = [pltpu.VMEM((3, k, cw), BF16)]
    if state is None:
        tps = seq // tm
        args = ()
        out_specs = [blk, pl.BlockSpec((None, 2, cw), lambda g, i: (i // tps, 0, g))]
        out_shape = [jax.ShapeDtypeStruct((m, cdim), BF16), jax.ShapeDtypeStruct((batch, 2, cdim), F32)]
        scratch.append(pltpu.VMEM((8, cw), F32))
    else:
        tps = None
        args = state
        in_specs += [blk, blk]
        out_specs = [blk, blk]
        out_shape = [jax.ShapeDtypeStruct((m, cdim), BF16), jax.ShapeDtypeStruct((m, cdim), F32)]
    return pl.pallas_call(
        functools.partial(_conv_proj_kernel, tiles_per_seq=tps, dec_seq=dec_seq),
        grid=(groups, m // tm),
        in_specs=in_specs, out_specs=out_specs, out_shape=out_shape, scratch_shapes=scratch,
        compiler_params=_params("arbitrary", "arbitrary"),
        name="conv_proj_prompt" if state is None else "conv_proj_sample",
    )(x, w_in, w_in, w_in, conv_w, *args)


def _qkv_proj_kernel(x_ref, w_ref, cs_ref, sa_ref, sb_ref, q_o, k_o, v_o, *refs, adim, q_scale):
    wbf = refs[-1]

    @pl.when(pl.program_id(0) == 0)
    def _():
        wbf[...] = w_ref[...].astype(BF16)

    h = jnp.dot(x_ref[...].astype(BF16), wbf[...], preferred_element_type=F32)
    tm = h.shape[0]
    kvd = (h.shape[1] - adim) // 2
    cs, sa, sb = cs_ref[...], sa_ref[...], sb_ref[...]
    roped = [_rope_chunk(h[:, c * LANES:(c + 1) * LANES], cs, sa, sb) for c in range((adim + kvd) // LANES)]
    for c, qc in enumerate(_pair_q_chunks(roped[:adim // LANES])):
        q_o[:, c * LANES:(c + 1) * LANES] = (qc * q_scale).astype(q_o.dtype)
    for c, kc in enumerate(roped[adim // LANES:]):
        k_o[:, c * LANES:(c + 1) * LANES] = kc.astype(k_o.dtype)
    v = h[:, adim + kvd:]
    v_o[...] = v.astype(v_o.dtype)
    if len(refs) > 1:
        kw_o, vw_o = refs[:2]
        for c, kc in enumerate(roped[adim // LANES:]):
            kw_o[:, c * LANES:(c + 1) * LANES] = kc[tm - WINDOW:, :]
        vw_o[...] = v[tm - WINDOW:, :]


def _qkv_proj(x, w_in, l, tabs, tm, cdim, adim, kvd, seq=None, batch=None):
    m, k = x.shape
    n = adim + 2 * kvd
    row = lambda i: (i, 0)
    prompt = seq is not None
    tps = seq // tm if prompt else 1
    tab = pl.BlockSpec((tm, LANES), lambda i: (i % tps, 0))
    dt = BF16 if prompt else F32
    out_specs = [pl.BlockSpec((tm, adim), row), pl.BlockSpec((tm, kvd), row), pl.BlockSpec((tm, kvd), row)]
    out_shape = [jax.ShapeDtypeStruct((m, adim), dt), jax.ShapeDtypeStruct((m, kvd), dt),
                 jax.ShapeDtypeStruct((m, kvd), dt)]
    if prompt:
        win = pl.BlockSpec((None, WINDOW, kvd), lambda i: (i // tps, 0, 0))
        out_specs += [win, win]
        out_shape += [jax.ShapeDtypeStruct((batch, WINDOW, kvd), F32)] * 2
    return pl.pallas_call(
        functools.partial(_qkv_proj_kernel, adim=adim, q_scale=QK_SCALE if prompt else 1.0),
        grid=(m // tm,),
        in_specs=[pl.BlockSpec((tm, k), row),
                  pl.BlockSpec((None, k, n), lambda i: (l, 0, 3 * cdim // n), pipeline_mode=pl.Buffered(1)),
                  tab, tab, tab],
        out_specs=out_specs, out_shape=out_shape,
        scratch_shapes=[pltpu.VMEM((k, n), BF16)],
        compiler_params=_params("arbitrary"),
        name="qkv_proj_prompt" if prompt else "qkv_proj_sample",
    )(x, w_in, *tabs)


def _sink_softmax(s, mask, sink_col):
    s = jnp.where(mask, s, -jnp.inf)
    m = jnp.maximum(jnp.max(s, axis=-1, keepdims=True), sink_col)
    p = jnp.exp(s - m)
    den = jnp.sum(p, axis=-1, keepdims=True) + jnp.exp(sink_col - m)
    return p * (1.0 / den)


def _attn_prompt_kernel(sink_ref, q_ref, kp_ref, kc_ref, vp_ref, vc_ref, o_ref, bias_ref, *, l):
    n = pl.program_id(1)
    w = q_ref.shape[0]
    chunks = q_ref.shape[1] // LANES
    pairs = kc_ref.shape[1] // LANES
    cpp = chunks // pairs
    lo = lax.broadcasted_iota(jnp.int32, (w, LANES), 1) < HEAD_DIM
    qi = lax.broadcasted_iota(jnp.int32, (w, 2 * w), 0)
    sj = lax.broadcasted_iota(jnp.int32, (w, 2 * w), 1)
    mask = (sj <= qi + w) & (sj > qi) & ((n > 0) | (sj >= w))
    bias_ref[...] = jnp.where(mask, 0.0, -jnp.inf)
    zero = jnp.zeros((w, LANES), BF16)
    for p in range(pairs):
        ksl = slice(p * LANES, (p + 1) * LANES)
        kk = jnp.concatenate([kp_ref[:, ksl], kc_ref[:, ksl]], axis=0)
        vv = jnp.concatenate([vp_ref[:, ksl], vc_ref[:, ksl]], axis=0)
        for j in range(cpp):
            c = p * cpp + j
            qc = q_ref[:, c * LANES:(c + 1) * LANES]
            halves = []
            for half, qh in enumerate((jnp.where(lo, qc, zero), jnp.where(lo, zero, qc))):
                sink = sink_ref[l, 2 * c + half]
                s = lax.dot_general(qh, kk, (((1,), (1,)), ((), ())), preferred_element_type=F32) + bias_ref[...]
                m = jnp.maximum(jnp.max(s, axis=-1, keepdims=True), sink)
                pexp = jnp.exp(s - m)
                den = jnp.sum(pexp, axis=-1, keepdims=True) + jnp.exp(sink - m)
                pr = (pexp * (1.0 / den)).astype(BF16)
                halves.append(jnp.dot(pr, vv, preferred_element_type=F32))
            o_ref[:, c * LANES:(c + 1) * LANES] = jnp.where(lo, halves[0], halves[1]).astype(BF16)


def _attn_prompt(sinks, q, k, v, l, batch, seq):
    m, adim = q.shape
    kvd = k.shape[1]
    nb = seq // WINDOW
    cur = lambda b, n: (b * nb + n, 0)
    prev = lambda b, n: (b * nb + jnp.maximum(n - 1, 0), 0)
    return pl.pallas_call(
        functools.partial(_attn_prompt_kernel, l=l),
        grid=(batch, nb),
        in_specs=[pl.BlockSpec(memory_space=pltpu.SMEM),
                  pl.BlockSpec((WINDOW, adim), cur),
                  pl.BlockSpec((WINDOW, kvd), prev), pl.BlockSpec((WINDOW, kvd), cur),
                  pl.BlockSpec((WINDOW, kvd), prev), pl.BlockSpec((WINDOW, kvd), cur)],
        out_specs=pl.BlockSpec((WINDOW, adim), cur),
        out_shape=jax.ShapeDtypeStruct((m, adim), BF16),
        scratch_shapes=[pltpu.VMEM((WINDOW, 2 * WINDOW), F32)],
        compiler_params=_params("arbitrary", "arbitrary"),
        name="attn_prompt",
    )(sinks, q, k, k, v, v)


def _attn_sample_kernel(sink_ref, q_ref, kn_ref, vn_ref, ck_ref, cv_ref,
                        o_ref, kw_ref, vw_ref, kk_ref, vv_ref, qs_ref, oacc_ref, *, l, t):
    nsq, w, kvd = ck_ref.shape
    tot = kk_ref.shape[1]
    chunks = q_ref.shape[1] // LANES
    pairs = kvd // LANES
    cpp = chunks // pairs
    rows = 2 * cpp * t
    lo = lax.broadcasted_iota(jnp.int32, (t, LANES), 1) < HEAD_DIM
    qi = lax.broadcasted_iota(jnp.int32, (rows, tot), 0) % t
    sj = lax.broadcasted_iota(jnp.int32, (rows, tot), 1)
    mask = (sj <= qi + w) & (sj > qi)
    rowid = lax.broadcasted_iota(jnp.int32, (rows, 1), 0) // t
    pad = jnp.zeros((tot - w - t, kvd), F32)
    for g in range(nsq):
        tok = slice(g * t, (g + 1) * t)
        kk_ref[g, 0:w, :] = ck_ref[g]
        vv_ref[g, 0:w, :] = cv_ref[g]
        kk_ref[g, w:w + t, :] = kn_ref[tok, :]
        vv_ref[g, w:w + t, :] = vn_ref[tok, :]
        kk_ref[g, w + t:, :] = pad
        vv_ref[g, w + t:, :] = pad
        kw_ref[g] = pltpu.roll(kk_ref[g], tot - t, 0)[0:w, :]
        vw_ref[g] = pltpu.roll(vv_ref[g], tot - t, 0)[0:w, :]
        for p in range(pairs):
            ksl = slice(p * LANES, (p + 1) * LANES)
            sink_col = jnp.zeros((rows, 1), F32)
            for j in range(cpp):
                c = p * cpp + j
                qc = q_ref[tok, c * LANES:(c + 1) * LANES]
                qs_ref[g, (2 * j) * t:(2 * j + 1) * t, :] = jnp.where(lo, qc, 0.0)
                qs_ref[g, (2 * j + 1) * t:(2 * j + 2) * t, :] = jnp.where(lo, 0.0, qc)
                sink_col = jnp.where(rowid == 2 * j, sink_ref[l, 2 * c], sink_col)
                sink_col = jnp.where(rowid == 2 * j + 1, sink_ref[l, 2 * c + 1], sink_col)
            s = lax.dot_general(qs_ref[g].astype(BF16), kk_ref[g, :, ksl].astype(BF16),
                                (((1,), (1,)), ((), ())), preferred_element_type=F32) * QK_SCALE
            pr = _sink_softmax(s, mask, sink_col)
            qs_ref[g] = jnp.dot(pr.astype(BF16), vv_ref[g, :, ksl].astype(BF16), preferred_element_type=F32)
            for j in range(cpp):
                c = p * cpp + j
                oacc_ref[tok, c * LANES:(c + 1) * LANES] = jnp.where(
                    lo, qs_ref[g, (2 * j) * t:(2 * j + 1) * t, :], qs_ref[g, (2 * j + 1) * t:(2 * j + 2) * t, :])
    o_ref[...] = oacc_ref[...].astype(BF16)


def _attn_sample(sinks, q, kn, vn, cache_k, cache_v, l, t):
    m, adim = q.shape
    nseq, w, kvd = cache_k.shape[1:]
    nsq = SAMPLE_SEQS_PER_STEP
    tot = -(-(w + t) // 8) * 8
    rows = 2 * (adim // kvd) * t
    row = lambda b: (b, 0)
    seq3 = lambda b: (b, 0, 0)
    cache = lambda b: (l, b, 0, 0)
    return pl.pallas_call(
        functools.partial(_attn_sample_kernel, l=l, t=t),
        grid=(nseq // nsq,),
        in_specs=[pl.BlockSpec(memory_space=pltpu.SMEM),
                  pl.BlockSpec((nsq * t, adim), row),
                  pl.BlockSpec((nsq * t, kvd), row), pl.BlockSpec((nsq * t, kvd), row),
                  pl.BlockSpec((None, nsq, w, kvd), cache), pl.BlockSpec((None, nsq, w, kvd), cache)],
        out_specs=[pl.BlockSpec((nsq * t, adim), row),
                   pl.BlockSpec((nsq, w, kvd), seq3), pl.BlockSpec((nsq, w, kvd), seq3)],
        out_shape=[jax.ShapeDtypeStruct((m, adim), BF16),
                   jax.ShapeDtypeStruct((nseq, w, kvd), F32),
                   jax.ShapeDtypeStruct((nseq, w, kvd), F32)],
        scratch_shapes=[pltpu.VMEM((nsq, tot, kvd), F32), pltpu.VMEM((nsq, tot, kvd), F32),
                        pltpu.VMEM((nsq, rows, LANES), F32), pltpu.VMEM((nsq * t, adim), F32)],
        compiler_params=_params("arbitrary"),
        name="attn_sample",
    )(sinks, q, kn, vn, cache_k, cache_v)


def _layer_norm(z, g, b):
    mu = jnp.mean(z, axis=-1, keepdims=True)
    d = z - mu
    var = jnp.mean(d * d, axis=-1, keepdims=True)
    return d * lax.rsqrt(var + LN_EPS) * g + b


def _top2(sg, idx, n):
    m1 = jnp.max(sg, axis=0, keepdims=True)
    i1 = jnp.min(jnp.where(sg == m1, idx, n), axis=0, keepdims=True)
    rest = jnp.where(idx == i1, -jnp.inf, sg)
    m2 = jnp.max(rest, axis=0, keepdims=True)
    i2 = jnp.min(jnp.where(rest == m2, idx, n), axis=0, keepdims=True)
    return m1, i1, m2, i2


def _pack_bf16_pairs(x):
    half = x.shape[1] // 2
    lo = pltpu.bitcast(x[:, :half].astype(BF16).astype(F32), jnp.uint32)
    hi = pltpu.bitcast(x[:, half:].astype(BF16).astype(F32), jnp.uint32)
    return hi | (lo >> 16)


def _unpack_bf16_pairs(w):
    lo = pltpu.bitcast(w << 16, F32).astype(BF16)
    hi = pltpu.bitcast(w & jnp.uint32(0xFFFF0000), F32).astype(BF16)
    return lo, hi


def _mix_out_kernel(conv_ref, attn_ref, x_ref, wo_ref, g_ref, b_ref, wr_ref, br_ref, cnt_in,
                    x1_ref, x1p_ref, route_ref, cnt_ref, carry, wob, wrb, y_scr, *, alpha):
    cdim = conv_ref.shape[1]
    i = pl.program_id(0)

    @pl.when(i == 0)
    def _():
        y_scr[1] = jnp.zeros(y_scr.shape[1:], F32)
        carry[...] = cnt_in[...]
        wrb[...] = wr_ref[...].astype(BF16)
        wob[0:cdim, :] = wo_ref[0:cdim, :].astype(BF16)
        order = _paired_head_order((wo_ref.shape[0] - cdim) // HEAD_DIM)
        for pos, head in enumerate(order):
            wob[cdim + pos * HEAD_DIM:cdim + (pos + 1) * HEAD_DIM, :] = (
                wo_ref[cdim + head * HEAD_DIM:cdim + (head + 1) * HEAD_DIM, :].astype(BF16))

    y_prev = y_scr[(i + 1) % 2]
    y = jnp.dot(conv_ref[...], wob[0:cdim, :], preferred_element_type=F32)
    y_scr[i % 2] = y + jnp.dot(attn_ref[...], wob[cdim:, :], preferred_element_type=F32)

    x1 = _layer_norm(alpha * x_ref[...] + y_prev, g_ref[...], b_ref[...])
    x1_ref[...] = x1
    x1p_ref[...] = _pack_bf16_pairs(x1)
    logits = lax.dot_general(wrb[...], x1.astype(BF16), (((1,), (1,)), ((), ())), preferred_element_type=F32)
    route_ref[...], counts = _route(logits, br_ref[...], carry[...])
    counts = jnp.where(i > 0, counts, carry[...])
    carry[...] = counts
    cnt_ref[...] = counts


def _route(logits, bias, counts):
    ne, tm = logits.shape
    epg = ne // N_EXPERT_GROUPS
    lanes = lambda a: jnp.concatenate([a] * (tm // LANES), axis=1)
    scores = jax.nn.sigmoid(logits)
    sel = scores + lanes(bias)
    eidx = lax.broadcasted_iota(jnp.int32, (ne, tm), 0).astype(F32)
    best = None
    for g in range(N_EXPERT_GROUPS):
        gidx = lax.broadcasted_iota(jnp.int32, (epg, tm), 0).astype(F32) + float(g * epg)
        m1, i1, m2, i2 = _top2(sel[g * epg:(g + 1) * epg], gidx, float(ne))
        cand = (m1 + m2, i1, i2)
        if best is None:
            best = cand
        else:
            upd = cand[0] > best[0]
            best = tuple(jnp.where(upd, new, old) for new, old in zip(cand, best))
    _, e1, e2 = best
    hit1 = eidx == e1
    hit2 = eidx == e2
    w1 = jnp.sum(jnp.where(hit1, scores, 0.0), axis=0, keepdims=True)
    w2 = jnp.sum(jnp.where(hit2, scores, 0.0), axis=0, keepdims=True)
    wsum = w1 + w2

    onehot = jnp.where(hit1 | hit2, 1.0, 0.0)
    earlier = (lax.broadcasted_iota(jnp.int32, (tm, tm), 0) < lax.broadcasted_iota(jnp.int32, (tm, tm), 1))
    before = jnp.dot(onehot.astype(BF16), jnp.where(earlier, 1.0, 0.0).astype(BF16), preferred_element_type=F32)
    before = before + lanes(counts)
    r1 = jnp.sum(jnp.where(hit1, before, 0.0), axis=0, keepdims=True)
    r2 = jnp.sum(jnp.where(hit2, before, 0.0), axis=0, keepdims=True)

    row = lax.broadcasted_iota(jnp.int32, (ROUTE_ROWS, tm), 0)
    out = jnp.zeros((ROUTE_ROWS, tm), F32)
    for c, val in enumerate((e1, e2, w1 / wsum, w2 / wsum, r1, r2)):
        out = jnp.where(row == c, val, out)
    return out, counts + jnp.sum(onehot, axis=1, keepdims=True)


def _mix_out(conv, attn, x, wo, ln_g, ln_b, wr_t, br, cnt_in, l, alpha, tm, total_rows, row_block0, prev=None):
    m, d = x.shape
    cdim = conv.shape[1]
    ne = wr_t.shape[0]
    n = m // tm
    rowi = lambda i: (jnp.minimum(i, n - 1), 0)
    rowp = lambda i: (jnp.maximum(i - 1, 0), 0)
    rowo = lambda i: (jnp.maximum(i - 1, 0) + row_block0, 0)
    const = lambda i: (0, 0)
    lay = lambda i: (l, 0, 0)
    in_specs = [pl.BlockSpec((tm, cdim), rowi), pl.BlockSpec((tm, d - cdim), rowi), pl.BlockSpec((tm, d), rowp),
                pl.BlockSpec((None, d, d), lay, pipeline_mode=pl.Buffered(1)),
                pl.BlockSpec((None, 1, d), lay), pl.BlockSpec((None, 1, d), lay),
                pl.BlockSpec((ne, d), const), pl.BlockSpec((ne, LANES), const),
                pl.BlockSpec((ne, LANES), const)]
    args = [conv, attn, x, wo, ln_g, ln_b, wr_t, br, cnt_in]
    aliases = {}
    if prev is not None:
        in_specs += [pl.BlockSpec(memory_space=pl.ANY)] * len(prev)
        aliases = {len(args) + k: k for k in range(len(prev))}
        args += list(prev)

    def body(*refs):
        _mix_out_kernel(*refs[:9], *refs[-8:], alpha=alpha)

    return pl.pallas_call(
        body,
        grid=(n + 1,),
        in_specs=in_specs,
        out_specs=[pl.BlockSpec((tm, d), rowo), pl.BlockSpec((tm, d // 2), rowo),
                   pl.BlockSpec((ROUTE_ROWS, tm), lambda i: (0, jnp.maximum(i - 1, 0) + row_block0)),
                   pl.BlockSpec((ne, LANES), const)],
        out_shape=[jax.ShapeDtypeStruct((total_rows, d), F32),
                   jax.ShapeDtypeStruct((total_rows, d // 2), jnp.uint32),
                   jax.ShapeDtypeStruct((ROUTE_ROWS, total_rows), F32),
                   jax.ShapeDtypeStruct((ne, LANES), F32)],
        scratch_shapes=[pltpu.VMEM((ne, LANES), F32), pltpu.VMEM((d, d), BF16), pltpu.VMEM((ne, d), BF16),
                        pltpu.VMEM((2, tm, d), F32)],
        input_output_aliases=aliases,
        compiler_params=_params("arbitrary"),
        name="mix_out_sample" if prev is not None else "mix_out_prompt",
    )(*args)


def _scatter_rows_kernel(slot_ref, x_ref, o_ref, sem, *, n_tok):
    ct = x_ref.shape[0]
    c = pl.program_id(0)
    last = pl.num_programs(0) - 1
    tail = n_tok - (pl.cdiv(n_tok, ct) - 1) * ct

    def run(count):
        for j in range(count):
            for k in range(TOP_K):
                pltpu.make_async_copy(x_ref.at[pl.ds(j, 1)],
                                      o_ref.at[pl.ds(slot_ref[TOP_K * j + k], 1)], sem).start(priority=k % 2)
        for _ in range(TOP_K):
            pltpu.make_async_copy(x_ref.at[pl.ds(0, count)], x_ref.at[pl.ds(0, count)], sem).wait()

    if tail == ct:
        run(ct)
    else:
        pl.when(c < last)(lambda: run(ct))
        pl.when(c == last)(lambda: run(tail))


def _scatter_rows(x1, slot_flat, n_slots):
    total, d = x1.shape
    ct = SLOT_BLOCK // TOP_K
    return pl.pallas_call(
        functools.partial(_scatter_rows_kernel, n_tok=total),
        grid=(pl.cdiv(total, ct),),
        in_specs=[pl.BlockSpec((SLOT_BLOCK,), lambda c: (c,), memory_space=pltpu.SMEM),
                  pl.BlockSpec((ct, d), lambda c: (c, 0))],
        out_specs=pl.BlockSpec(memory_space=pl.ANY),
        out_shape=jax.ShapeDtypeStruct((n_slots, d), x1.dtype),
        scratch_shapes=[pltpu.SemaphoreType.DMA(())],
        compiler_params=_params("arbitrary"),
        name="scatter_rows",
    )(slot_flat, x1)


def _expert_kernel(te_ref, nv_ref, nu_ref, x_ref, wg_hbm, wu_hbm, wd_hbm, o_ref,
                   xb_ref, wgu_ref, h_ref, gu_buf, dn_buf, sems, *, l):
    nf, _, fc = h_ref.shape
    nc, _, oc = dn_buf.shape
    steps = nf + nc
    i = pl.program_id(0)
    s = pl.program_id(1)
    n_used = nu_ref[0]

    def copies(pos, e):
        if pos < nf:
            return [pltpu.make_async_copy(w.at[l, e, :, pos * fc:(pos + 1) * fc], gu_buf.at[pos, k], sems.at[pos])
                    for k, w in enumerate((wg_hbm, wu_hbm))]
        c = pos - nf
        return [pltpu.make_async_copy(wd_hbm.at[l, e, :, c * oc:(c + 1) * oc], dn_buf.at[c], sems.at[pos])]

    def start(pos, tile):
        @pl.when(tile < n_used)
        def _():
            for k, cp in enumerate(copies(pos, te_ref[tile])):
                cp.start(priority=(pos + k) % 2)

    def step_body(pos):
        if pos == 0:
            @pl.when(i == 0)
            def _():
                for first in range(WEIGHT_LOOKAHEAD):
                    start(first, i)

        ahead = pos + WEIGHT_LOOKAHEAD
        start(ahead % steps, i + ahead // steps)
        for cp in copies(pos, 0):
            cp.wait()

        if pos == 0:
            rid = lax.broadcasted_iota(jnp.int32, (x_ref.shape[0], 1), 0)
            lo, hi = _unpack_bf16_pairs(jnp.where(rid < nv_ref[i], x_ref[...], jnp.uint32(0)))
            xb_ref[:, :x_ref.shape[1]] = lo
            xb_ref[:, x_ref.shape[1]:] = hi
        if pos < nf:
            wgu_ref[:, :fc] = gu_buf[pos, 0].astype(BF16)
            wgu_ref[:, fc:] = gu_buf[pos, 1].astype(BF16)
            gu = jnp.dot(xb_ref[...], wgu_ref[...], preferred_element_type=F32)
            h_ref[pos] = (jax.nn.silu(gu[:, :fc]) * gu[:, fc:]).astype(BF16)
        else:
            c = pos - nf
            y = jnp.dot(h_ref[0], dn_buf[c, 0:fc, :].astype(BF16), preferred_element_type=F32)
            for f in range(1, nf):
                y = y + jnp.dot(h_ref[f], dn_buf[c, f * fc:(f + 1) * fc, :].astype(BF16),
                                preferred_element_type=F32)
            o_ref[...] = y

    @pl.when(i < n_used)
    def _():
        for pos in range(steps):
            pl.when(s == pos)(functools.partial(step_body, pos))


def _expert_ffn(xs, w_gate, w_up, w_down, tile_e, tile_nv, n_used, l):
    rows = xs.shape[0]
    d, dff = w_gate.shape[2:]
    nf = dff // FF_CHUNK
    nc = d // OUT_CHUNK
    steps = nf + nc
    assert steps > WEIGHT_LOOKAHEAD
    nt = rows // EXPERT_TILE

    def item(i, nu):
        return jnp.minimum(i, jnp.maximum(nu[0] - 1, 0))

    def col(i, s, nu):
        return jnp.maximum(jnp.where(i < nu[0], s, steps - 1) - nf, 0)

    xmap = lambda i, s, te, nv, nu: (item(i, nu), 0)
    omap = lambda i, s, te, nv, nu: (item(i, nu), col(i, s, nu))
    hbm = pl.BlockSpec(memory_space=pl.ANY)
    return pl.pallas_call(
        functools.partial(_expert_kernel, l=l),
        grid_spec=pltpu.PrefetchScalarGridSpec(
            num_scalar_prefetch=3,
            grid=(nt, steps),
            in_specs=[pl.BlockSpec((EXPERT_TILE, xs.shape[1]), xmap), hbm, hbm, hbm],
            out_specs=pl.BlockSpec((EXPERT_TILE, OUT_CHUNK), omap),
            scratch_shapes=[pltpu.VMEM((EXPERT_TILE, d), BF16),
                            pltpu.VMEM((d, 2 * FF_CHUNK), BF16),
                            pltpu.VMEM((nf, EXPERT_TILE, FF_CHUNK), BF16),
                            pltpu.VMEM((nf, 2, d, FF_CHUNK), F32),
                            pltpu.VMEM((nc, dff, OUT_CHUNK), F32),
                            pltpu.SemaphoreType.DMA((steps,))]),
        out_shape=jax.ShapeDtypeStruct((rows, d), F32),
        compiler_params=_params("arbitrary", "arbitrary"),
        name="expert_ffn",
    )(tile_e, tile_nv, n_used, xs, w_gate, w_up, w_down)


def _combine_kernel(slot_cur, slot_nxt, x1_ref, r_ref, g_ref, b_ref, y_hbm, o_ref, ybuf, sems, *, alpha):
    i = pl.program_id(0)
    ct = x1_ref.shape[0]

    def fetch(slot_ref, buf):
        for j in range(ct):
            for k in range(TOP_K):
                pltpu.make_async_copy(y_hbm.at[pl.ds(slot_ref[TOP_K * j + k], 1)],
                                      ybuf.at[buf, k, pl.ds(j, 1)], sems.at[buf]).start()

    @pl.when(i == 0)
    def _():
        fetch(slot_cur, 0)

    for parity in range(2):
        @pl.when((i + 1 < pl.num_programs(0)) & (i % 2 == parity))
        def _(parity=parity):
            fetch(slot_nxt, 1 - parity)

    buf = i % 2
    for k in range(TOP_K):
        pltpu.make_async_copy(ybuf.at[buf, k], ybuf.at[buf, k], sems.at[buf]).wait()
    moe = ybuf[buf, 0] * r_ref[0] + ybuf[buf, 1] * r_ref[1]
    o_ref[...] = _layer_norm(alpha * x1_ref[...] + moe, g_ref[...], b_ref[...])


def _combine(x1, yb, slot_flat, gates, ln_g, ln_b, l, alpha, tm, row0, m):
    d = x1.shape[1]
    nsteps = m // tm
    rb0 = row0 // tm
    sb0 = row0 * TOP_K // SLOT_BLOCK
    sper = pl.cdiv(tm * TOP_K, SLOT_BLOCK)
    rowo = lambda i: (i + rb0, 0)
    lay = lambda i: (l, 0, 0)
    return pl.pallas_call(
        functools.partial(_combine_kernel, alpha=alpha),
        grid=(nsteps,),
        in_specs=[pl.BlockSpec((SLOT_BLOCK,), lambda i: (sb0 + i * sper,), memory_space=pltpu.SMEM),
                  pl.BlockSpec((SLOT_BLOCK,), lambda i: (sb0 + jnp.minimum(i + 1, nsteps - 1) * sper,),
                               memory_space=pltpu.SMEM),
                  pl.BlockSpec((tm, d), rowo),
                  pl.BlockSpec((TOP_K, tm, 1), lambda i: (0, i + rb0, 0)),
                  pl.BlockSpec((None, 1, d), lay), pl.BlockSpec((None, 1, d), lay),
                  pl.BlockSpec(memory_space=pl.ANY)],
        out_specs=pl.BlockSpec((tm, d), lambda i: (i, 0)),
        out_shape=jax.ShapeDtypeStruct((m, d), F32),
        scratch_shapes=[pltpu.VMEM((2, TOP_K, tm, d), F32), pltpu.SemaphoreType.DMA((2,))],
        compiler_params=_params("arbitrary"),
        name="combine",
    )(slot_flat, slot_flat, x1, gates, ln_g, ln_b, yb)


def _rope_tables(pos):
    half = ROT_DIM // 2
    inv = ROPE_THETA ** (-jnp.arange(half, dtype=F32) * 2.0 / ROT_DIM)
    ang = pos.astype(F32)[:, None] * inv[None, :]
    cos, sin = jnp.cos(ang), jnp.sin(ang)
    n = pos.shape[0]
    rest = HEAD_DIM - ROT_DIM
    cs = jnp.concatenate([cos, cos, jnp.ones((n, rest), F32)], axis=1)
    sa = jnp.concatenate([-sin, jnp.zeros((n, half + rest), F32)], axis=1)
    sb = jnp.concatenate([jnp.zeros((n, half), F32), sin, jnp.zeros((n, rest), F32)], axis=1)
    rep = LANES // HEAD_DIM
    return tuple(jnp.tile(t, (1, rep)) for t in (cs, sa, sb))


def kernel(x_prompt, x_sample, state_conv, cache_k, cache_v, w_in, conv_w, attn_sinks, w_o,
           ln1_g, ln1_b, w_router, b_router, w_gate, w_up, w_down, ln2_g, ln2_b):
    batch, seq, d = x_prompt.shape
    nseq, dec_seq, _ = x_sample.shape
    depth = w_in.shape[0]
    cdim = conv_w.shape[2]
    adim = d - cdim
    kvd = N_KV_HEADS * HEAD_DIM
    n_heads = adim // HEAD_DIM
    ne = w_router.shape[1]
    mp, ms = batch * seq, nseq * dec_seq
    total = mp + ms
    assign = total * TOP_K
    alpha = (2 * depth) ** 0.25

    sinks = attn_sinks[:, jnp.asarray(_paired_head_order(n_heads))]
    wr_t = w_router.T
    br = jnp.broadcast_to(b_router[:, None], (ne, LANES))

    tabs_p = _rope_tables(jnp.arange(seq))
    tabs_s = _rope_tables(jnp.tile(PAST_LEN + jnp.arange(dec_seq), nseq))
    cache_k2 = cache_k.reshape(depth, nseq, WINDOW, kvd)
    cache_v2 = cache_v.reshape(depth, nseq, WINDOW, kvd)
    ln1g, ln1b = ln1_g.reshape(depth, 1, d), ln1_b.reshape(depth, 1, d)
    ln2g, ln2b = ln2_g.reshape(depth, 1, d), ln2_b.reshape(depth, 1, d)

    n_items = pl.cdiv(assign, EXPERT_TILE) + ne
    slot_len = pl.cdiv(assign, SLOT_BLOCK) * SLOT_BLOCK
    items = jnp.arange(n_items, dtype=jnp.int32)

    xp = x_prompt.reshape(mp, d)
    xs_ = x_sample.reshape(ms, d)
    outs = [[] for _ in range(6)]
    for l in range(depth):
        conv_p, cst_p = _conv_proj(xp, w_in, conv_w, l, PROJ_ROWS, cdim, seq=seq, batch=batch)
        q_p, k_p, v_p, kw_p, vw_p = _qkv_proj(xp, w_in, l, tabs_p, PROJ_ROWS, cdim, adim, kvd, seq=seq, batch=batch)
        attn_p = _attn_prompt(sinks, q_p, k_p, v_p, l, batch, seq)

        st = state_conv[l]
        st1 = jnp.repeat(st[:, 1], dec_seq, axis=0)
        st2 = jnp.stack([st[:, 0], st[:, 1]] + [st[:, 1]] * (dec_seq - 2), axis=1).reshape(ms, cdim)
        conv_s, inner_s = _conv_proj(xs_, w_in, conv_w, l, ms, cdim, state=(st1, st2), dec_seq=dec_seq)
        q_s, kn_s, vn_s = _qkv_proj(xs_, w_in, l, tabs_s, ms, cdim, adim, kvd)
        attn_s, kw_s, vw_s = _attn_sample(sinks, q_s, kn_s, vn_s, cache_k2, cache_v2, l, dec_seq)

        zeros_cnt = jnp.zeros((ne, LANES), F32)
        *bufs, cnt = _mix_out(conv_p, attn_p, xp, w_o, ln1g, ln1b, wr_t, br, zeros_cnt, l, alpha, MIX_ROWS, total, 0)
        x1, x1p, route, cnt = _mix_out(conv_s, attn_s, xs_, w_o, ln1g, ln1b, wr_t, br, cnt,
                                       l, alpha, ms, total, mp // ms, prev=bufs)

        e = route[0:TOP_K].astype(jnp.int32)
        rank = route[4:4 + TOP_K].astype(jnp.int32)
        gates = route[2:2 + TOP_K][:, :, None]
        counts = cnt[:, 0].astype(jnp.int32)
        tiles_e = (counts + EXPERT_TILE - 1) // EXPERT_TILE
        tile_end = jnp.cumsum(tiles_e)
        tile_start = tile_end - tiles_e
        onehot = e[:, :, None] == jnp.arange(ne)[None, None, :]
        pos = jnp.sum(jnp.where(onehot, (tile_start * EXPERT_TILE)[None, None, :], 0), axis=-1) + rank
        slot_flat = jnp.pad(pos.T.reshape(-1), (0, slot_len - assign))
        n_used = tile_end[-1:].astype(jnp.int32)
        item_e = jnp.minimum(jnp.sum(tile_end[None, :] <= items[:, None], axis=1), ne - 1).astype(jnp.int32)
        item_nv = jnp.clip(counts[item_e] - (items - tile_start[item_e]) * EXPERT_TILE, 0, EXPERT_TILE)

        xsorted = _scatter_rows(x1p, slot_flat, n_items * EXPERT_TILE)
        yb = _expert_ffn(xsorted, w_gate, w_up, w_down, item_e, item_nv.astype(jnp.int32), n_used, l)

        xp = _combine(x1, yb, slot_flat, gates, ln2g, ln2b, l, alpha, COMBINE_ROWS, 0, mp)
        xs_ = _combine(x1, yb, slot_flat, gates, ln2g, ln2b, l, alpha, ms, mp, ms)

        inner3 = inner_s.reshape(nseq, dec_seq, cdim)
        for lst, val in zip(outs, (cst_p, kw_p.reshape(batch, WINDOW, N_KV_HEADS, HEAD_DIM),
                                   vw_p.reshape(batch, WINDOW, N_KV_HEADS, HEAD_DIM),
                                   inner3[:, dec_seq - 2:],
                                   kw_s.reshape(nseq, WINDOW, N_KV_HEADS, HEAD_DIM),
                                   vw_s.reshape(nseq, WINDOW, N_KV_HEADS, HEAD_DIM))):
            lst.append(val)

    return (xp.reshape(batch, seq, d), xs_.reshape(nseq, dec_seq, d), *[jnp.stack(o) for o in outs])
```

```python
import functools
import math

import jax
import jax.numpy as jnp
from jax import lax
from jax.experimental import pallas as pl
from jax.experimental.pallas import tpu as pltpu

F32 = jnp.float32
BF16 = jnp.bfloat16

HEAD_DIM = 64
N_KV_HEADS = 4
ROT_DIM = HEAD_DIM // 4
ROPE_THETA = 500000.0
WINDOW = 128
PAST_LEN = 16384
N_EXPERT_GROUPS = 4
TOP_K = 2
LN_EPS = 1e-5
QK_SCALE = HEAD_DIM ** -0.5
assert math.frexp(QK_SCALE)[0] == 0.5

LANES = 128
VMEM_LIMIT = 56 * 1024 * 1024

EXPERT_TILE = 576
FF_CHUNK = 512
OUT_CHUNK = 1024
WEIGHT_LOOKAHEAD = 3
CONV_GROUP = 512
ROUTE_ROWS = 8
SAMPLE_SEQS_PER_STEP = 4
SLOT_BLOCK = 1024
PROJ_ROWS = 512
MIX_ROWS = 256
COMBINE_ROWS = 512


def _params(*sem):
    return pltpu.CompilerParams(dimension_semantics=sem, vmem_limit_bytes=VMEM_LIMIT)


def _rope_chunk(xc, cs, sa, sb):
    return xc * cs + pltpu.roll(xc, LANES - ROT_DIM // 2, 1) * sa + pltpu.roll(xc, ROT_DIM // 2, 1) * sb


def _conv(inner, gate, w_ref, prev1, prev2):
    return gate * (w_ref[0:1, :] * prev2 + w_ref[1:2, :] * prev1 + w_ref[2:3, :] * inner)


def _paired_head_order(n_heads):
    group = n_heads // N_KV_HEADS
    return [(2 * pair + half) * group + j
            for pair in range(N_KV_HEADS // 2) for j in range(group) for half in range(2)]


def _pair_q_chunks(chunks):
    order = _paired_head_order(2 * len(chunks))
    lo = lax.broadcasted_iota(jnp.int32, chunks[0].shape, 1) < HEAD_DIM
    out = []
    for c in range(len(chunks)):
        halves = []
        for half, head in enumerate(order[2 * c:2 * c + 2]):
            src = chunks[head // 2]
            halves.append(src if head % 2 == half else pltpu.roll(src, HEAD_DIM, 1))
        out.append(jnp.where(lo, halves[0], halves[1]))
    return out


def _conv_proj_kernel(x_ref, wc_ref, wb_ref, wu_ref, cw_ref, *refs, tiles_per_seq, dec_seq):
    if dec_seq is None:
        conv_o, cst_o, wbf, carry = refs
    else:
        st1, st2, conv_o, inner_o, wbf = refs
    i = pl.program_id(1)

    @pl.when(i == 0)
    def _():
        for k, w in enumerate((wc_ref, wb_ref, wu_ref)):
            wbf[k] = w[...].astype(BF16)

    xb = x_ref[...].astype(BF16)
    c, gate, u = (jnp.dot(xb, wbf[k], preferred_element_type=F32) for k in range(3))
    inner = c * u
    tm, cw = inner.shape
    row = lax.broadcasted_iota(jnp.int32, (tm, cw), 0)
    if dec_seq is None:
        @pl.when(i % tiles_per_seq == 0)
        def _():
            carry[...] = jnp.zeros_like(carry)

        c1, c2 = carry[7:8, :], carry[6:7, :]
        prev1 = jnp.where(row == 0, c1, pltpu.roll(inner, 1, 0))
        prev2 = jnp.where(row == 0, c2, jnp.where(row == 1, c1, pltpu.roll(inner, 2, 0)))
        carry[...] = inner[tm - 8:, :]
        cst_o[...] = inner[tm - 2:, :]
    else:
        t = row % dec_seq
        prev1 = jnp.where(t == 0, st1[...], pltpu.roll(inner, 1, 0))
        prev2 = jnp.where(t < 2, st2[...], pltpu.roll(inner, 2, 0))
        inner_o[...] = inner
    conv_o[...] = _conv(inner, gate, cw_ref, prev1, prev2).astype(BF16)


def _conv_proj(x, w_in, conv_w, l, tm, cdim, seq=None, batch=None, state=None, dec_seq=None):
    m, k = x.shape
    cw = CONV_GROUP
    groups = cdim // cw
    sec = lambda s: pl.BlockSpec((None, k, cw), lambda g, i: (l, 0, s * groups + g))
    blk = pl.BlockSpec((tm, cw), lambda g, i: (i, g))
    in_specs = [pl.BlockSpec((tm, k), lambda g, i: (i, 0)), sec(0), sec(1), sec(2),
                pl.BlockSpec((None, 3, cw), lambda g, i: (l, 0, g))]
    scratch = [pltpu.VMEM((3, k, cw), BF16)]
    if state is None:
        tps = seq // tm
        args = ()
        out_specs = [blk, pl.BlockSpec((None, 2, cw), lambda g, i: (i // tps, 0, g))]
        out_shape = [jax.ShapeDtypeStruct((m, cdim), BF16), jax.ShapeDtypeStruct((batch, 2, cdim), F32)]
        scratch.append(pltpu.VMEM((8, cw), F32))
    else:
        tps = None
        args = state
        in_specs += [blk, blk]
        out_specs = [blk, blk]
        out_shape = [jax.ShapeDtypeStruct((m, cdim), BF16), jax.ShapeDtypeStruct((m, cdim), F32)]
    return pl.pallas_call(
        functools.partial(_conv_proj_kernel, tiles_per_seq=tps, dec_seq=dec_seq),
        grid=(groups, m // tm),
        in_specs=in_specs, out_specs=out_specs, out_shape=out_shape, scratch_shapes=scratch,
        compiler_params=_params("arbitrary", "arbitrary"),
        name="conv_proj_prompt" if state is None else "conv_proj_sample",
    )(x, w_in, w_in, w_in, conv_w, *args)


def _qkv_proj_kernel(x_ref, w_ref, cs_ref, sa_ref, sb_ref, q_o, k_o, v_o, *refs, adim, q_scale):
    wbf = refs[-1]

    @pl.when(pl.program_id(0) == 0)
    def _():
        wbf[...] = w_ref[...].astype(BF16)

    h = jnp.dot(x_ref[...].astype(BF16), wbf[...], preferred_element_type=F32)
    tm = h.shape[0]
    kvd = (h.shape[1] - adim) // 2
    cs, sa, sb = cs_ref[...], sa_ref[...], sb_ref[...]
    roped = [_rope_chunk(h[:, c * LANES:(c + 1) * LANES], cs, sa, sb) for c in range((adim + kvd) // LANES)]
    for c, qc in enumerate(_pair_q_chunks(roped[:adim // LANES])):
        q_o[:, c * LANES:(c + 1) * LANES] = (qc * q_scale).astype(q_o.dtype)
    for c, kc in enumerate(roped[adim // LANES:]):
        k_o[:, c * LANES:(c + 1) * LANES] = kc.astype(k_o.dtype)
    v = h[:, adim + kvd:]
    v_o[...] = v.astype(v_o.dtype)
    if len(refs) > 1:
        kw_o, vw_o = refs[:2]
        for c, kc in enumerate(roped[adim // LANES:]):
            kw_o[:, c * LANES:(c + 1) * LANES] = kc[tm - WINDOW:, :]
        vw_o[...] = v[tm - WINDOW:, :]


def _qkv_proj(x, w_in, l, tabs, tm, cdim, adim, kvd, seq=None, batch=None):
    m, k = x.shape
    n = adim + 2 * kvd
    row = lambda i: (i, 0)
    prompt = seq is not None
    tps = seq // tm if prompt else 1
    tab = pl.BlockSpec((tm, LANES), lambda i: (i % tps, 0))
    dt = BF16 if prompt else F32
    out_specs = [pl.BlockSpec((tm, adim), row), pl.BlockSpec((tm, kvd), row), pl.BlockSpec((tm, kvd), row)]
    out_shape = [jax.ShapeDtypeStruct((m, adim), dt), jax.ShapeDtypeStruct((m, kvd), dt),
                 jax.ShapeDtypeStruct((m, kvd), dt)]
    if prompt:
        win = pl.BlockSpec((None, WINDOW, kvd), lambda i: (i // tps, 0, 0))
        out_specs += [win, win]
        out_shape += [jax.ShapeDtypeStruct((batch, WINDOW, kvd), F32)] * 2
    return pl.pallas_call(
        functools.partial(_qkv_proj_kernel, adim=adim, q_scale=QK_SCALE if prompt else 1.0),
        grid=(m // tm,),
        in_specs=[pl.BlockSpec((tm, k), row),
                  pl.BlockSpec((None, k, n), lambda i: (l, 0, 3 * cdim // n), pipeline_mode=pl.Buffered(1)),
                  tab, tab, tab],
        out_specs=out_specs, out_shape=out_shape,
        scratch_shapes=[pltpu.VMEM((k, n), BF16)],
        compiler_params=_params("arbitrary"),
        name="qkv_proj_prompt" if prompt else "qkv_proj_sample",
    )(x, w_in, *tabs)


def _sink_softmax(s, mask, sink_col):
    s = jnp.where(mask, s, -jnp.inf)
    m = jnp.maximum(jnp.max(s, axis=-1, keepdims=True), sink_col)
    p = jnp.exp(s - m)
    den = jnp.sum(p, axis=-1, keepdims=True) + jnp.exp(sink_col - m)
    return p * (1.0 / den)


def _attn_prompt_kernel(sink_ref, q_ref, kp_ref, kc_ref, vp_ref, vc_ref, o_ref, bias_ref, *, l):
    n = pl.program_id(1)
    w = q_ref.shape[0]
    chunks = q_ref.shape[1] // LANES
    pairs = kc_ref.shape[1] // LANES
    cpp = chunks // pairs
    lo = lax.broadcasted_iota(jnp.int32, (w, LANES), 1) < HEAD_DIM
    qi = lax.broadcasted_iota(jnp.int32, (w, 2 * w), 0)
    sj = lax.broadcasted_iota(jnp.int32, (w, 2 * w), 1)
    mask = (sj <= qi + w) & (sj > qi) & ((n > 0) | (sj >= w))
    bias_ref[...] = jnp.where(mask, 0.0, -jnp.inf)
    zero = jnp.zeros((w, LANES), BF16)
    for p in range(pairs):
        ksl = slice(p * LANES, (p + 1) * LANES)
        kk = jnp.concatenate([kp_ref[:, ksl], kc_ref[:, ksl]], axis=0)
        vv = jnp.concatenate([vp_ref[:, ksl], vc_ref[:, ksl]], axis=0)
        for j in range(cpp):
            c = p * cpp + j
            qc = q_ref[:, c * LANES:(c + 1) * LANES]
            halves = []
            for half, qh in enumerate((jnp.where(lo, qc, zero), jnp.where(lo, zero, qc))):
                sink = sink_ref[l, 2 * c + half]
                s = lax.dot_general(qh, kk, (((1,), (1,)), ((), ())), preferred_element_type=F32) + bias_ref[...]
                m = jnp.maximum(jnp.max(s, axis=-1, keepdims=True), sink)
                pexp = jnp.exp(s - m)
                den = jnp.sum(pexp, axis=-1, keepdims=True) + jnp.exp(sink - m)
                pr = (pexp * (1.0 / den)).astype(BF16)
                halves.append(jnp.dot(pr, vv, preferred_element_type=F32))
            o_ref[:, c * LANES:(c + 1) * LANES] = jnp.where(lo, halves[0], halves[1]).astype(BF16)


def _attn_prompt(sinks, q, k, v, l, batch, seq):
    m, adim = q.shape
    kvd = k.shape[1]
    nb = seq // WINDOW
    cur = lambda b, n: (b * nb + n, 0)
    prev = lambda b, n: (b * nb + jnp.maximum(n - 1, 0), 0)
    return pl.pallas_call(
        functools.partial(_attn_prompt_kernel, l=l),
        grid=(batch, nb),
        in_specs=[pl.BlockSpec(memory_space=pltpu.SMEM),
                  pl.BlockSpec((WINDOW, adim), cur),
                  pl.BlockSpec((WINDOW, kvd), prev), pl.BlockSpec((WINDOW, kvd), cur),
                  pl.BlockSpec((WINDOW, kvd), prev), pl.BlockSpec((WINDOW, kvd), cur)],
        out_specs=pl.BlockSpec((WINDOW, adim), cur),
        out_shape=jax.ShapeDtypeStruct((m, adim), BF16),
        scratch_shapes=[pltpu.VMEM((WINDOW, 2 * WINDOW), F32)],
        compiler_params=_params("arbitrary", "arbitrary"),
        name="attn_prompt",
    )(sinks, q, k, k, v, v)


def _attn_sample_kernel(sink_ref, q_ref, kn_ref, vn_ref, ck_ref, cv_ref,
                        o_ref, kw_ref, vw_ref, kk_ref, vv_ref, qs_ref, oacc_ref, *, l, t):
    nsq, w, kvd = ck_ref.shape
    tot = kk_ref.shape[1]
    chunks = q_ref.shape[1] // LANES
    pairs = kvd // LANES
    cpp = chunks // pairs
    rows = 2 * cpp * t
    lo = lax.broadcasted_iota(jnp.int32, (t, LANES), 1) < HEAD_DIM
    qi = lax.broadcasted_iota(jnp.int32, (rows, tot), 0) % t
    sj = lax.broadcasted_iota(jnp.int32, (rows, tot), 1)
    mask = (sj <= qi + w) & (sj > qi)
    rowid = lax.broadcasted_iota(jnp.int32, (rows, 1), 0) // t
    pad = jnp.zeros((tot - w - t, kvd), F32)
    for g in range(nsq):
        tok = slice(g * t, (g + 1) * t)
        kk_ref[g, 0:w, :] = ck_ref[g]
        vv_ref[g, 0:w, :] = cv_ref[g]
        kk_ref[g, w:w + t, :] = kn_ref[tok, :]
        vv_ref[g, w:w + t, :] = vn_ref[tok, :]
        kk_ref[g, w + t:, :] = pad
        vv_ref[g, w + t:, :] = pad
        kw_ref[g] = pltpu.roll(kk_ref[g], tot - t, 0)[0:w, :]
        vw_ref[g] = pltpu.roll(vv_ref[g], tot - t, 0)[0:w, :]
        for p in range(pairs):
            ksl = slice(p * LANES, (p + 1) * LANES)
            sink_col = jnp.zeros((rows, 1), F32)
            for j in range(cpp):
                c = p * cpp + j
                qc = q_ref[tok, c * LANES:(c + 1) * LANES]
                qs_ref[g, (2 * j) * t:(2 * j + 1) * t, :] = jnp.where(lo, qc, 0.0)
                qs_ref[g, (2 * j + 1) * t:(2 * j + 2) * t, :] = jnp.where(lo, 0.0, qc)
                sink_col = jnp.where(rowid == 2 * j, sink_ref[l, 2 * c], sink_col)
                sink_col = jnp.where(rowid == 2 * j + 1, sink_ref[l, 2 * c + 1], sink_col)
            s = lax.dot_general(qs_ref[g].astype(BF16), kk_ref[g, :, ksl].astype(BF16),
                                (((1,), (1,)), ((), ())), preferred_element_type=F32) * QK_SCALE
            pr = _sink_softmax(s, mask, sink_col)
            qs_ref[g] = jnp.dot(pr.astype(BF16), vv_ref[g, :, ksl].astype(BF16), preferred_element_type=F32)
            for j in range(cpp):
                c = p * cpp + j
                oacc_ref[tok, c * LANES:(c + 1) * LANES] = jnp.where(
                    lo, qs_ref[g, (2 * j) * t:(2 * j + 1) * t, :], qs_ref[g, (2 * j + 1) * t:(2 * j + 2) * t, :])
    o_ref[...] = oacc_ref[...].astype(BF16)


def _attn_sample(sinks, q, kn, vn, cache_k, cache_v, l, t):
    m, adim = q.shape
    nseq, w, kvd = cache_k.shape[1:]
    nsq = SAMPLE_SEQS_PER_STEP
    tot = -(-(w + t) // 8) * 8
    rows = 2 * (adim // kvd) * t
    row = lambda b: (b, 0)
    seq3 = lambda b: (b, 0, 0)
    cache = lambda b: (l, b, 0, 0)
    return pl.pallas_call(
        functools.partial(_attn_sample_kernel, l=l, t=t),
        grid=(nseq // nsq,),
        in_specs=[pl.BlockSpec(memory_space=pltpu.SMEM),
                  pl.BlockSpec((nsq * t, adim), row),
                  pl.BlockSpec((nsq * t, kvd), row), pl.BlockSpec((nsq * t, kvd), row),
                  pl.BlockSpec((None, nsq, w, kvd), cache), pl.BlockSpec((None, nsq, w, kvd), cache)],
        out_specs=[pl.BlockSpec((nsq * t, adim), row),
                   pl.BlockSpec((nsq, w, kvd), seq3), pl.BlockSpec((nsq, w, kvd), seq3)],
        out_shape=[jax.ShapeDtypeStruct((m, adim), BF16),
                   jax.ShapeDtypeStruct((nseq, w, kvd), F32),
                   jax.ShapeDtypeStruct((nseq, w, kvd), F32)],
        scratch_shapes=[pltpu.VMEM((nsq, tot, kvd), F32), pltpu.VMEM((nsq, tot, kvd), F32),
                        pltpu.VMEM((nsq, rows, LANES), F32), pltpu.VMEM((nsq * t, adim), F32)],
        compiler_params=_params("arbitrary"),
        name="attn_sample",
    )(sinks, q, kn, vn, cache_k, cache_v)


def _layer_norm(z, g, b):
    mu = jnp.mean(z, axis=-1, keepdims=True)
    d = z - mu
    var = jnp.mean(d * d, axis=-1, keepdims=True)
    return d * lax.rsqrt(var + LN_EPS) * g + b


def _top2(sg, idx, n):
    m1 = jnp.max(sg, axis=0, keepdims=True)
    i1 = jnp.min(jnp.where(sg == m1, idx, n), axis=0, keepdims=True)
    rest = jnp.where(idx == i1, -jnp.inf, sg)
    m2 = jnp.max(rest, axis=0, keepdims=True)
    i2 = jnp.min(jnp.where(rest == m2, idx, n), axis=0, keepdims=True)
    return m1, i1, m2, i2


def _pack_bf16_pairs(x):
    half = x.shape[1] // 2
    lo = pltpu.bitcast(x[:, :half].astype(BF16).astype(F32), jnp.uint32)
    hi = pltpu.bitcast(x[:, half:].astype(BF16).astype(F32), jnp.uint32)
    return hi | (lo >> 16)


def _unpack_bf16_pairs(w):
    lo = pltpu.bitcast(w << 16, F32).astype(BF16)
    hi = pltpu.bitcast(w & jnp.uint32(0xFFFF0000), F32).astype(BF16)
    return lo, hi


def _mix_out_kernel(conv_ref, attn_ref, x_ref, wo_ref, g_ref, b_ref, wr_ref, br_ref, cnt_in,
                    x1_ref, x1p_ref, route_ref, cnt_ref, carry, wob, wrb, y_scr, *, alpha):
    cdim = conv_ref.shape[1]
    i = pl.program_id(0)

    @pl.when(i == 0)
    def _():
        y_scr[1] = jnp.zeros(y_scr.shape[1:], F32)
        carry[...] = cnt_in[...]
        wrb[...] = wr_ref[...].astype(BF16)
        wob[0:cdim, :] = wo_ref[0:cdim, :].astype(BF16)
        order = _paired_head_order((wo_ref.shape[0] - cdim) // HEAD_DIM)
        for pos, head in enumerate(order):
            wob[cdim + pos * HEAD_DIM:cdim + (pos + 1) * HEAD_DIM, :] = (
                wo_ref[cdim + head * HEAD_DIM:cdim + (head + 1) * HEAD_DIM, :].astype(BF16))

    y_prev = y_scr[(i + 1) % 2]
    y = jnp.dot(conv_ref[...], wob[0:cdim, :], preferred_element_type=F32)
    y_scr[i % 2] = y + jnp.dot(attn_ref[...], wob[cdim:, :], preferred_element_type=F32)

    x1 = _layer_norm(alpha * x_ref[...] + y_prev, g_ref[...], b_ref[...])
    x1_ref[...] = x1
    x1p_ref[...] = _pack_bf16_pairs(x1)
    logits = lax.dot_general(wrb[...], x1.astype(BF16), (((1,), (1,)), ((), ())), preferred_element_type=F32)
    route_ref[...], counts = _route(logits, br_ref[...], carry[...])
    counts = jnp.where(i > 0, counts, carry[...])
    carry[...] = counts
    cnt_ref[...] = counts


def _route(logits, bias, counts):
    ne, tm = logits.shape
    epg = ne // N_EXPERT_GROUPS
    lanes = lambda a: jnp.concatenate([a] * (tm // LANES), axis=1)
    scores = jax.nn.sigmoid(logits)
    sel = scores + lanes(bias)
    eidx = lax.broadcasted_iota(jnp.int32, (ne, tm), 0).astype(F32)
    best = None
    for g in range(N_EXPERT_GROUPS):
        gidx = lax.broadcasted_iota(jnp.int32, (epg, tm), 0).astype(F32) + float(g * epg)
        m1, i1, m2, i2 = _top2(sel[g * epg:(g + 1) * epg], gidx, float(ne))
        cand = (m1 + m2, i1, i2)
        if best is None:
            best = cand
        else:
            upd = cand[0] > best[0]
            best = tuple(jnp.where(upd, new, old) for new, old in zip(cand, best))
    _, e1, e2 = best
    hit1 = eidx == e1
    hit2 = eidx == e2
    w1 = jnp.sum(jnp.where(hit1, scores, 0.0), axis=0, keepdims=True)
    w2 = jnp.sum(jnp.where(hit2, scores, 0.0), axis=0, keepdims=True)
    wsum = w1 + w2

    onehot = jnp.where(hit1 | hit2, 1.0, 0.0)
    earlier = (lax.broadcasted_iota(jnp.int32, (tm, tm), 0) < lax.broadcasted_iota(jnp.int32, (tm, tm), 1))
    before = jnp.dot(onehot.astype(BF16), jnp.where(earlier, 1.0, 0.0).astype(BF16), preferred_element_type=F32)
    before = before + lanes(counts)
    r1 = jnp.sum(jnp.where(hit1, before, 0.0), axis=0, keepdims=True)
    r2 = jnp.sum(jnp.where(hit2, before, 0.0), axis=0, keepdims=True)

    row = lax.broadcasted_iota(jnp.int32, (ROUTE_ROWS, tm), 0)
    out = jnp.zeros((ROUTE_ROWS, tm), F32)
    for c, val in enumerate((e1, e2, w1 / wsum, w2 / wsum, r1, r2)):
        out = jnp.where(row == c, val, out)
    return out, counts + jnp.sum(onehot, axis=1, keepdims=True)


def _mix_out(conv, attn, x, wo, ln_g, ln_b, wr_t, br, cnt_in, l, alpha, tm, total_rows, row_block0, prev=None):
    m, d = x.shape
    cdim = conv.shape[1]
    ne = wr_t.shape[0]
    n = m // tm
    rowi = lambda i: (jnp.minimum(i, n - 1), 0)
    rowp = lambda i: (jnp.maximum(i - 1, 0), 0)
    rowo = lambda i: (jnp.maximum(i - 1, 0) + row_block0, 0)
    const = lambda i: (0, 0)
    lay = lambda i: (l, 0, 0)
    in_specs = [pl.BlockSpec((tm, cdim), rowi), pl.BlockSpec((tm, d - cdim), rowi), pl.BlockSpec((tm, d), rowp),
                pl.BlockSpec((None, d, d), lay, pipeline_mode=pl.Buffered(1)),
                pl.BlockSpec((None, 1, d), lay), pl.BlockSpec((None, 1, d), lay),
                pl.BlockSpec((ne, d), const), pl.BlockSpec((ne, LANES), const),
                pl.BlockSpec((ne, LANES), const)]
    args = [conv, attn, x, wo, ln_g, ln_b, wr_t, br, cnt_in]
    aliases = {}
    if prev is not None:
        in_specs += [pl.BlockSpec(memory_space=pl.ANY)] * len(prev)
        aliases = {len(args) + k: k for k in range(len(prev))}
        args += list(prev)

    def body(*refs):
        _mix_out_kernel(*refs[:9], *refs[-8:], alpha=alpha)

    return pl.pallas_call(
        body,
        grid=(n + 1,),
        in_specs=in_specs,
        out_specs=[pl.BlockSpec((tm, d), rowo), pl.BlockSpec((tm, d // 2), rowo),
                   pl.BlockSpec((ROUTE_ROWS, tm), lambda i: (0, jnp.maximum(i - 1, 0) + row_block0)),
                   pl.BlockSpec((ne, LANES), const)],
        out_shape=[jax.ShapeDtypeStruct((total_rows, d), F32),
                   jax.ShapeDtypeStruct((total_rows, d // 2), jnp.uint32),
                   jax.ShapeDtypeStruct((ROUTE_ROWS, total_rows), F32),
                   jax.ShapeDtypeStruct((ne, LANES), F32)],
        scratch_shapes=[pltpu.VMEM((ne, LANES), F32), pltpu.VMEM((d, d), BF16), pltpu.VMEM((ne, d), BF16),
                        pltpu.VMEM((2, tm, d), F32)],
        input_output_aliases=aliases,
        compiler_params=_params("arbitrary"),
        name="mix_out_sample" if prev is not None else "mix_out_prompt",
    )(*args)


def _scatter_rows_kernel(slot_ref, x_ref, o_ref, sem, *, n_tok):
    ct = x_ref.shape[0]
    c = pl.program_id(0)
    last = pl.num_programs(0) - 1
    tail = n_tok - (pl.cdiv(n_tok, ct) - 1) * ct

    def run(count):
        for j in range(count):
            for k in range(TOP_K):
                pltpu.make_async_copy(x_ref.at[pl.ds(j, 1)],
                                      o_ref.at[pl.ds(slot_ref[TOP_K * j + k], 1)], sem).start(priority=k % 2)
        for _ in range(TOP_K):
            pltpu.make_async_copy(x_ref.at[pl.ds(0, count)], x_ref.at[pl.ds(0, count)], sem).wait()

    if tail == ct:
        run(ct)
    else:
        pl.when(c < last)(lambda: run(ct))
        pl.when(c == last)(lambda: run(tail))


def _scatter_rows(x1, slot_flat, n_slots):
    total, d = x1.shape
    ct = SLOT_BLOCK // TOP_K
    return pl.pallas_call(
        functools.partial(_scatter_rows_kernel, n_tok=total),
        grid=(pl.cdiv(total, ct),),
        in_specs=[pl.BlockSpec((SLOT_BLOCK,), lambda c: (c,), memory_space=pltpu.SMEM),
                  pl.BlockSpec((ct, d), lambda c: (c, 0))],
        out_specs=pl.BlockSpec(memory_space=pl.ANY),
        out_shape=jax.ShapeDtypeStruct((n_slots, d), x1.dtype),
        scratch_shapes=[pltpu.SemaphoreType.DMA(())],
        compiler_params=_params("arbitrary"),
        name="scatter_rows",
    )(slot_flat, x1)


def _expert_kernel(te_ref, nv_ref, nu_ref, tok_ref, x_hbm, wg_hbm, wu_hbm, wd_hbm, o_ref,
                   xb_ref, wgu_ref, h_ref, gu_buf, dn_buf, sems, xg, gsem, *, l):
    nf, _, fc = h_ref.shape
    nc, _, oc = dn_buf.shape
    steps = nf + nc
    i = pl.program_id(0)
    s = pl.program_id(1)
    n_used = nu_ref[0]

    def copies(pos, e):
        if pos < nf:
            return [pltpu.make_async_copy(w.at[l, e, :, pos * fc:(pos + 1) * fc], gu_buf.at[pos, k], sems.at[pos])
                    for k, w in enumerate((wg_hbm, wu_hbm))]
        c = pos - nf
        return [pltpu.make_async_copy(wd_hbm.at[l, e, :, c * oc:(c + 1) * oc], dn_buf.at[c], sems.at[pos])]

    def start(pos, tile):
        @pl.when(tile < n_used)
        def _():
            for k, cp in enumerate(copies(pos, te_ref[tile])):
                cp.start(priority=(pos + k) % 2)

    def gather(tile, slot):
        base = tile * EXPERT_TILE
        for j in range(EXPERT_TILE):
            pltpu.make_async_copy(x_hbm.at[pl.ds(tok_ref[base + j], 1)], xg.at[slot, pl.ds(j, 1)],
                                  gsem.at[slot]).start(priority=j % 2)

    def gather_for(tile):
        @pl.when(tile < n_used)
        def _():
            for slot in range(2):
                pl.when(tile % 2 == slot)(functools.partial(gather, tile, slot))

    def step_body(pos):
        if pos == 0:
            @pl.when(i == 0)
            def _():
                for first in range(WEIGHT_LOOKAHEAD):
                    start(first, i)
                gather_for(i)

        ahead = pos + WEIGHT_LOOKAHEAD
        start(ahead % steps, i + ahead // steps)
        for cp in copies(pos, 0):
            cp.wait()

        if pos == 0:
            for slot in range(2):
                @pl.when(i % 2 == slot)
                def _(slot=slot):
                    pltpu.make_async_copy(xg.at[slot], xg.at[slot], gsem.at[slot]).wait()
            half = xg.shape[2]
            rid = lax.broadcasted_iota(jnp.int32, (xg.shape[1], 1), 0)
            lo, hi = _unpack_bf16_pairs(jnp.where(rid < nv_ref[i], xg[i % 2], jnp.uint32(0)))
            xb_ref[:, :half] = lo
            xb_ref[:, half:] = hi
        if pos < nf:
            wgu_ref[:, :fc] = gu_buf[pos, 0].astype(BF16)
            wgu_ref[:, fc:] = gu_buf[pos, 1].astype(BF16)
            gu = jnp.dot(xb_ref[...], wgu_ref[...], preferred_element_type=F32)
            h_ref[pos] = (jax.nn.silu(gu[:, :fc]) * gu[:, fc:]).astype(BF16)
        else:
            c = pos - nf
            y = jnp.dot(h_ref[0], dn_buf[c, 0:fc, :].astype(BF16), preferred_element_type=F32)
            for f in range(1, nf):
                y = y + jnp.dot(h_ref[f], dn_buf[c, f * fc:(f + 1) * fc, :].astype(BF16),
                                preferred_element_type=F32)
            o_ref[...] = y
        if pos == 1:
            gather_for(i + 1)

    @pl.when(i < n_used)
    def _():
        for pos in range(steps):
            pl.when(s == pos)(functools.partial(step_body, pos))


def _expert_ffn(x1p, tok_of_slot, w_gate, w_up, w_down, tile_e, tile_nv, n_used, l):
    rows = tok_of_slot.shape[0]
    d, dff = w_gate.shape[2:]
    nf = dff // FF_CHUNK
    nc = d // OUT_CHUNK
    steps = nf + nc
    assert steps > WEIGHT_LOOKAHEAD
    nt = rows // EXPERT_TILE

    def item(i, nu):
        return jnp.minimum(i, jnp.maximum(nu[0] - 1, 0))

    def col(i, s, nu):
        return jnp.maximum(jnp.where(i < nu[0], s, steps - 1) - nf, 0)

    omap = lambda i, s, te, nv, nu, tok: (item(i, nu), col(i, s, nu))
    hbm = pl.BlockSpec(memory_space=pl.ANY)
    return pl.pallas_call(
        functools.partial(_expert_kernel, l=l),
        grid_spec=pltpu.PrefetchScalarGridSpec(
            num_scalar_prefetch=4,
            grid=(nt, steps),
            in_specs=[hbm, hbm, hbm, hbm],
            out_specs=pl.BlockSpec((EXPERT_TILE, OUT_CHUNK), omap),
            scratch_shapes=[pltpu.VMEM((EXPERT_TILE, d), BF16),
                            pltpu.VMEM((d, 2 * FF_CHUNK), BF16),
                            pltpu.VMEM((nf, EXPERT_TILE, FF_CHUNK), BF16),
                            pltpu.VMEM((nf, 2, d, FF_CHUNK), F32),
                            pltpu.VMEM((nc, dff, OUT_CHUNK), F32),
                            pltpu.SemaphoreType.DMA((steps,)),
                            pltpu.VMEM((2, EXPERT_TILE, x1p.shape[1]), x1p.dtype),
                            pltpu.SemaphoreType.DMA((2,))]),
        out_shape=jax.ShapeDtypeStruct((rows, d), F32),
        compiler_params=_params("arbitrary", "arbitrary"),
        name="expert_ffn",
    )(tile_e, tile_nv, n_used, tok_of_slot, x1p, w_gate, w_up, w_down)


def _combine_kernel(slot_cur, slot_nxt, x1_ref, r_ref, g_ref, b_ref, y_hbm, o_ref, ybuf, sems, *, alpha):
    i = pl.program_id(0)
    ct = x1_ref.shape[0]

    def fetch(slot_ref, buf):
        for j in range(ct):
            for k in range(TOP_K):
                pltpu.make_async_copy(y_hbm.at[pl.ds(slot_ref[TOP_K * j + k], 1)],
                                      ybuf.at[buf, k, pl.ds(j, 1)], sems.at[buf]).start()

    @pl.when(i == 0)
    def _():
        fetch(slot_cur, 0)

    for parity in range(2):
        @pl.when((i + 1 < pl.num_programs(0)) & (i % 2 == parity))
        def _(parity=parity):
            fetch(slot_nxt, 1 - parity)

    buf = i % 2
    for k in range(TOP_K):
        pltpu.make_async_copy(ybuf.at[buf, k], ybuf.at[buf, k], sems.at[buf]).wait()
    moe = ybuf[buf, 0] * r_ref[0] + ybuf[buf, 1] * r_ref[1]
    o_ref[...] = _layer_norm(alpha * x1_ref[...] + moe, g_ref[...], b_ref[...])


def _combine(x1, yb, slot_flat, gates, ln_g, ln_b, l, alpha, tm, row0, m):
    d = x1.shape[1]
    nsteps = m // tm
    rb0 = row0 // tm
    sb0 = row0 * TOP_K // SLOT_BLOCK
    sper = pl.cdiv(tm * TOP_K, SLOT_BLOCK)
    rowo = lambda i: (i + rb0, 0)
    lay = lambda i: (l, 0, 0)
    return pl.pallas_call(
        functools.partial(_combine_kernel, alpha=alpha),
        grid=(nsteps,),
        in_specs=[pl.BlockSpec((SLOT_BLOCK,), lambda i: (sb0 + i * sper,), memory_space=pltpu.SMEM),
                  pl.BlockSpec((SLOT_BLOCK,), lambda i: (sb0 + jnp.minimum(i + 1, nsteps - 1) * sper,),
                               memory_space=pltpu.SMEM),
                  pl.BlockSpec((tm, d), rowo),
                  pl.BlockSpec((TOP_K, tm, 1), lambda i: (0, i + rb0, 0)),
                  pl.BlockSpec((None, 1, d), lay), pl.BlockSpec((None, 1, d), lay),
                  pl.BlockSpec(memory_space=pl.ANY)],
        out_specs=pl.BlockSpec((tm, d), lambda i: (i, 0)),
        out_shape=jax.ShapeDtypeStruct((m, d), F32),
        scratch_shapes=[pltpu.VMEM((2, TOP_K, tm, d), F32), pltpu.SemaphoreType.DMA((2,))],
        compiler_params=_params("arbitrary"),
        name="combine",
    )(slot_flat, slot_flat, x1, gates, ln_g, ln_b, yb)


def _rope_tables(pos):
    half = ROT_DIM // 2
    inv = ROPE_THETA ** (-jnp.arange(half, dtype=F32) * 2.0 / ROT_DIM)
    ang = pos.astype(F32)[:, None] * inv[None, :]
    cos, sin = jnp.cos(ang), jnp.sin(ang)
    n = pos.shape[0]
    rest = HEAD_DIM - ROT_DIM
    cs = jnp.concatenate([cos, cos, jnp.ones((n, rest), F32)], axis=1)
    sa = jnp.concatenate([-sin, jnp.zeros((n, half + rest), F32)], axis=1)
    sb = jnp.concatenate([jnp.zeros((n, half), F32), sin, jnp.zeros((n, rest), F32)], axis=1)
    rep = LANES // HEAD_DIM
    return tuple(jnp.tile(t, (1, rep)) for t in (cs, sa, sb))


def kernel(x_prompt, x_sample, state_conv, cache_k, cache_v, w_in, conv_w, attn_sinks, w_o,
           ln1_g, ln1_b, w_router, b_router, w_gate, w_up, w_down, ln2_g, ln2_b):
    batch, seq, d = x_prompt.shape
    nseq, dec_seq, _ = x_sample.shape
    depth = w_in.shape[0]
    cdim = conv_w.shape[2]
    adim = d - cdim
    kvd = N_KV_HEADS * HEAD_DIM
    n_heads = adim // HEAD_DIM
    ne = w_router.shape[1]
    mp, ms = batch * seq, nseq * dec_seq
    total = mp + ms
    assign = total * TOP_K
    alpha = (2 * depth) ** 0.25

    sinks = attn_sinks[:, jnp.asarray(_paired_head_order(n_heads))]
    wr_t = w_router.T
    br = jnp.broadcast_to(b_router[:, None], (ne, LANES))

    tabs_p = _rope_tables(jnp.arange(seq))
    tabs_s = _rope_tables(jnp.tile(PAST_LEN + jnp.arange(dec_seq), nseq))
    cache_k2 = cache_k.reshape(depth, nseq, WINDOW, kvd)
    cache_v2 = cache_v.reshape(depth, nseq, WINDOW, kvd)
    ln1g, ln1b = ln1_g.reshape(depth, 1, d), ln1_b.reshape(depth, 1, d)
    ln2g, ln2b = ln2_g.reshape(depth, 1, d), ln2_b.reshape(depth, 1, d)

    n_items = pl.cdiv(assign, EXPERT_TILE) + ne
    slot_len = pl.cdiv(assign, SLOT_BLOCK) * SLOT_BLOCK
    items = jnp.arange(n_items, dtype=jnp.int32)

    xp = x_prompt.reshape(mp, d)
    xs_ = x_sample.reshape(ms, d)
    outs = [[] for _ in range(6)]
    for l in range(depth):
        conv_p, cst_p = _conv_proj(xp, w_in, conv_w, l, PROJ_ROWS, cdim, seq=seq, batch=batch)
        q_p, k_p, v_p, kw_p, vw_p = _qkv_proj(xp, w_in, l, tabs_p, PROJ_ROWS, cdim, adim, kvd, seq=seq, batch=batch)
        attn_p = _attn_prompt(sinks, q_p, k_p, v_p, l, batch, seq)

        st = state_conv[l]
        st1 = jnp.repeat(st[:, 1], dec_seq, axis=0)
        st2 = jnp.stack([st[:, 0], st[:, 1]] + [st[:, 1]] * (dec_seq - 2), axis=1).reshape(ms, cdim)
        conv_s, inner_s = _conv_proj(xs_, w_in, conv_w, l, ms, cdim, state=(st1, st2), dec_seq=dec_seq)
        q_s, kn_s, vn_s = _qkv_proj(xs_, w_in, l, tabs_s, ms, cdim, adim, kvd)
        attn_s, kw_s, vw_s = _attn_sample(sinks, q_s, kn_s, vn_s, cache_k2, cache_v2, l, dec_seq)

        zeros_cnt = jnp.zeros((ne, LANES), F32)
        *bufs, cnt = _mix_out(conv_p, attn_p, xp, w_o, ln1g, ln1b, wr_t, br, zeros_cnt, l, alpha, MIX_ROWS, total, 0)
        x1, x1p, route, cnt = _mix_out(conv_s, attn_s, xs_, w_o, ln1g, ln1b, wr_t, br, cnt,
                                       l, alpha, ms, total, mp // ms, prev=bufs)

        e = route[0:TOP_K].astype(jnp.int32)
        rank = route[4:4 + TOP_K].astype(jnp.int32)
        gates = route[2:2 + TOP_K][:, :, None]
        counts = cnt[:, 0].astype(jnp.int32)
        tiles_e = (counts + EXPERT_TILE - 1) // EXPERT_TILE
        tile_end = jnp.cumsum(tiles_e)
        tile_start = tile_end - tiles_e
        onehot = e[:, :, None] == jnp.arange(ne)[None, None, :]
        pos = jnp.sum(jnp.where(onehot, (tile_start * EXPERT_TILE)[None, None, :], 0), axis=-1) + rank
        slot_flat = jnp.pad(pos.T.reshape(-1), (0, slot_len - assign))
        n_used = tile_end[-1:].astype(jnp.int32)
        item_e = jnp.minimum(jnp.sum(tile_end[None, :] <= items[:, None], axis=1), ne - 1).astype(jnp.int32)
        item_nv = jnp.clip(counts[item_e] - (items - tile_start[item_e]) * EXPERT_TILE, 0, EXPERT_TILE)

        tok_of_slot = jnp.zeros((n_items * EXPERT_TILE,), jnp.int32).at[slot_flat[:assign]].set(
            jnp.arange(assign, dtype=jnp.int32) // TOP_K)
        yb = _expert_ffn(x1p, tok_of_slot, w_gate, w_up, w_down, item_e, item_nv.astype(jnp.int32), n_used, l)

        xp = _combine(x1, yb, slot_flat, gates, ln2g, ln2b, l, alpha, COMBINE_ROWS, 0, mp)
        xs_ = _combine(x1, yb, slot_flat, gates, ln2g, ln2b, l, alpha, ms, mp, ms)

        inner3 = inner_s.reshape(nseq, dec_seq, cdim)
        for lst, val in zip(outs, (cst_p, kw_p.reshape(batch, WINDOW, N_KV_HEADS, HEAD_DIM),
                                   vw_p.reshape(batch, WINDOW, N_KV_HEADS, HEAD_DIM),
                                   inner3[:, dec_seq - 2:],
                                   kw_s.reshape(nseq, WINDOW, N_KV_HEADS, HEAD_DIM),
                                   vw_s.reshape(nseq, WINDOW, N_KV_HEADS, HEAD_DIM))):
            lst.append(val)

    return (xp.reshape(batch, seq, d), xs_.reshape(nseq, dec_seq, d), *[jnp.stack(o) for o in outs])
```
